```python
import jax, jax.numpy as jnp
from jax import lax
import numpy as np

D_MODEL = 1024
BATCH = 8
SEQ = 2048
DEPTH = 1

SB_HEADS = 8
SB_HEAD_DIM = 64
SB_BLOCK = 128
DN_HEADS = 4
DN_KEY_DIM = 128
DN_VALUE_DIM = 128
DN_CONV = 4
DN_CHUNK = 64
D_FF = 2816
N_MOD = 9
EPS = 1e-6

SB_W = SB_HEADS * SB_HEAD_DIM
DN_QK_W = DN_HEADS * DN_KEY_DIM
DN_V_W = DN_HEADS * DN_VALUE_DIM
IN_SPLITS = (SB_W, SB_W, SB_W, DN_QK_W, DN_QK_W, DN_V_W, DN_V_W, DN_HEADS, DN_HEADS, D_MODEL, D_MODEL)
IN_WIDTH = 3 * SB_W + 2 * DN_QK_W + 2 * DN_V_W + 2 * DN_HEADS + 2 * D_MODEL

kernel_name = "hybrid_stickbreak_gdn_macaron_adaln"


def rms_norm(x, gain):
    xf = x.astype(jnp.float32)
    y = xf * lax.rsqrt(jnp.mean(xf * xf, axis=-1, keepdims=True) + EPS)
    return (y * gain.astype(jnp.float32)).astype(x.dtype)


def l2_norm(x):
    xf = x.astype(jnp.float32)
    return xf * lax.rsqrt(jnp.sum(xf * xf, axis=-1, keepdims=True) + EPS)


def modulate(x, shift, scale):
    return x * (1.0 + scale) + shift


def swiglu(u, w_in, w_out):
    a, b = jnp.split(u @ w_in, 2, axis=-1)
    return (jax.nn.silu(a) * b) @ w_out


def causal_depthwise_conv(x, w):
    k = w.shape[0]
    return lax.conv_general_dilated(
        x, w[:, None, :].astype(x.dtype), window_strides=(1,), padding=[(k - 1, 0)],
        dimension_numbers=("NWC", "WIO", "NWC"), feature_group_count=x.shape[-1])


def stick_breaking_attention(q, k, v):
    seq = q.shape[2]
    scale = SB_HEAD_DIM ** -0.5
    outs = []
    for blk in range(seq // SB_BLOCK):
        start = blk * SB_BLOCK
        end = start + SB_BLOCK
        qb = q[:, :, start:end]
        kb = k[:, :, :end]
        vb = v[:, :, :end]
        z = jnp.einsum("bhqd,bhkd->bhqk", qb, kb, preferred_element_type=jnp.float32) * scale
        t_pos = start + jnp.arange(SB_BLOCK)[:, None]
        s_pos = jnp.arange(end)[None, :]
        causal = s_pos < t_pos
        log_beta = jax.nn.log_sigmoid(z)
        log_keep = jnp.where(causal, jax.nn.log_sigmoid(-z), 0.0)
        later = lax.cumsum(log_keep, axis=3, reverse=True) - log_keep
        w = jnp.where(causal, jnp.exp(log_beta + later), 0.0)
        outs.append(jnp.einsum("bhqk,bhkd->bhqd", w.astype(v.dtype), vb))
    return jnp.concatenate(outs, axis=2)


def gated_delta_rule_chunked(q, k, v, g, beta):
    bsz, seq, nh, dk = q.shape
    dv = v.shape[-1]
    c = DN_CHUNK
    n = seq // c
    f32 = jnp.float32
    q = q.astype(f32) * (dk ** -0.5)
    chunk = lambda a: a.astype(f32).transpose(0, 2, 1, 3).reshape(bsz, nh, n, c, a.shape[-1])
    q, k, v = chunk(q), chunk(k), chunk(v)
    g = jnp.cumsum(g.astype(f32).transpose(0, 2, 1).reshape(bsz, nh, n, c), axis=-1)
    beta = beta.astype(f32).transpose(0, 2, 1).reshape(bsz, nh, n, c)
    kb = k * beta[..., None]
    vb = v * beta[..., None]
    incl = jnp.tril(jnp.ones((c, c), dtype=bool))
    strict = jnp.tril(jnp.ones((c, c), dtype=bool), -1)
    diff = g[..., :, None] - g[..., None, :]
    decay = jnp.where(incl, jnp.exp(jnp.where(incl, diff, 0.0)), 0.0)
    lower = jnp.where(strict, jnp.einsum("bhnid,bhnjd->bhnij", kb, k) * decay, 0.0)
    rhs = jnp.concatenate([vb, kb * jnp.exp(g)[..., None]], axis=-1)
    sol = lax.linalg.triangular_solve(lower, rhs, left_side=True, lower=True, unit_diagonal=True)
    u, w = sol[..., :dv], sol[..., dv:]
    a_qk = jnp.where(incl, jnp.einsum("bhnid,bhnjd->bhnij", q, k) * decay, 0.0)

    def step(state, inp):
        q_c, k_c, u_c, w_c, g_c, a_c = inp
        v_new = u_c - jnp.matmul(w_c, state)
        o = jnp.matmul(q_c * jnp.exp(g_c)[..., None], state) + jnp.matmul(a_c, v_new)
        g_last = g_c[..., -1]
        k_dec = k_c * jnp.exp(g_last[..., None] - g_c)[..., None]
        state = state * jnp.exp(g_last)[..., None, None] + jnp.einsum("bhcd,bhce->bhde", k_dec, v_new)
        return state, o

    xs = tuple(jnp.moveaxis(a, 2, 0) for a in (q, k, u, w, g, a_qk))
    state0 = jnp.zeros((bsz, nh, dk, dv), f32)
    _, o = lax.scan(step, state0, xs)
    o = jnp.moveaxis(o, 0, 2).reshape(bsz, nh, seq, dv)
    return o.transpose(0, 2, 1, 3)


def hybrid_mixer(u, w_in, g_q_sb, g_k_sb, w_conv, a_log, dt_bias, g_dn_out, w_up_sb, w_up_dn, w_out):
    bsz, seq, _ = u.shape
    offsets = [int(o) for o in np.cumsum(IN_SPLITS)[:-1]]
    (q_sb, k_sb, v_sb, q_dn, k_dn, v_dn, z_dn, b_dn, a_dn, r_sb, r_dn) = jnp.split(u @ w_in, offsets, axis=-1)

    to_heads = lambda t: t.reshape(bsz, seq, SB_HEADS, SB_HEAD_DIM)
    q_a = rms_norm(to_heads(q_sb), g_q_sb).transpose(0, 2, 1, 3)
    k_a = rms_norm(to_heads(k_sb), g_k_sb).transpose(0, 2, 1, 3)
    v_a = to_heads(v_sb).transpose(0, 2, 1, 3)
    y_sb = stick_breaking_attention(q_a, k_a, v_a)
    y_sb = y_sb.transpose(0, 2, 1, 3).reshape(bsz, seq, SB_W) @ w_up_sb

    qkv = jax.nn.silu(causal_depthwise_conv(jnp.concatenate([q_dn, k_dn, v_dn], axis=-1), w_conv))
    q_b, k_b, v_b = jnp.split(qkv, [DN_QK_W, 2 * DN_QK_W], axis=-1)
    q_b = l2_norm(q_b.reshape(bsz, seq, DN_HEADS, DN_KEY_DIM))
    k_b = l2_norm(k_b.reshape(bsz, seq, DN_HEADS, DN_KEY_DIM))
    v_b = v_b.reshape(bsz, seq, DN_HEADS, DN_VALUE_DIM)
    beta = jax.nn.sigmoid(b_dn.astype(jnp.float32))
    g = -jnp.exp(a_log.astype(jnp.float32)) * jax.nn.softplus(a_dn.astype(jnp.float32) + dt_bias.astype(jnp.float32))
    o = gated_delta_rule_chunked(q_b, k_b, v_b, g, beta)
    o = rms_norm(o, g_dn_out) * jax.nn.silu(z_dn.reshape(bsz, seq, DN_HEADS, DN_VALUE_DIM).astype(jnp.float32))
    y_dn = o.reshape(bsz, seq, DN_V_W).astype(u.dtype) @ w_up_dn

    merged = jax.nn.sigmoid(r_sb) * y_sb + jax.nn.sigmoid(r_dn) * y_dn
    return merged @ w_out


def setup_inputs(seed: int = 0) -> dict:
    key = jax.random.key(seed)
    ks = jax.random.split(key, 24)
    nrm = lambda k, shape, fan_in: jax.random.normal(k, shape, jnp.float32) * (fan_in ** -0.5)
    gain = lambda k, shape: 1.0 + 0.02 * jax.random.normal(k, shape, jnp.float32)
    dt = jnp.exp(jax.random.uniform(ks[14], (DEPTH, DN_HEADS), jnp.float32, np.log(1e-3), np.log(1e-1)))
    return {
        "x": jax.random.normal(ks[0], (BATCH, SEQ, D_MODEL), jnp.float32),
        "c": jax.random.normal(ks[1], (BATCH, D_MODEL), jnp.float32),
        "w_ada": nrm(ks[2], (DEPTH, D_MODEL, N_MOD * D_MODEL), D_MODEL),
        "b_ada": 0.02 * jax.random.normal(ks[3], (DEPTH, N_MOD * D_MODEL), jnp.float32),
        "g_ffn1": gain(ks[4], (DEPTH, D_MODEL)),
        "w_ffn1_in": nrm(ks[5], (DEPTH, D_MODEL, 2 * D_FF), D_MODEL),
        "w_ffn1_out": nrm(ks[6], (DEPTH, D_FF, D_MODEL), D_FF),
        "g_mix": gain(ks[7], (DEPTH, D_MODEL)),
        "w_in": nrm(ks[8], (DEPTH, D_MODEL, IN_WIDTH), D_MODEL),
        "g_q_sb": gain(ks[9], (DEPTH, SB_HEAD_DIM)),
        "g_k_sb": gain(ks[10], (DEPTH, SB_HEAD_DIM)),
        "w_conv": nrm(ks[11], (DEPTH, DN_CONV, 2 * DN_QK_W + DN_V_W), DN_CONV),
        "a_log": jnp.log(jax.random.uniform(ks[12], (DEPTH, DN_HEADS), jnp.float32, 1.0, 16.0)),
        "dt_bias": dt + jnp.log(-jnp.expm1(-dt)),
        "g_dn_out": gain(ks[13], (DEPTH, DN_VALUE_DIM)),
        "w_up_sb": nrm(ks[15], (DEPTH, SB_W, D_MODEL), SB_W),
        "w_up_dn": nrm(ks[16], (DEPTH, DN_V_W, D_MODEL), DN_V_W),
        "w_out": nrm(ks[17], (DEPTH, D_MODEL, D_MODEL), D_MODEL),
        "g_ffn2": gain(ks[18], (DEPTH, D_MODEL)),
        "w_ffn2_in": nrm(ks[19], (DEPTH, D_MODEL, 2 * D_FF), D_MODEL),
        "w_ffn2_out": nrm(ks[20], (DEPTH, D_FF, D_MODEL), D_FF),
    }


def reference(x, c, w_ada, b_ada, g_ffn1, w_ffn1_in, w_ffn1_out, g_mix, w_in, g_q_sb, g_k_sb, w_conv,
              a_log, dt_bias, g_dn_out, w_up_sb, w_up_dn, w_out, g_ffn2, w_ffn2_in, w_ffn2_out):
    h = x
    cond = jax.nn.silu(c)
    for l in range(DEPTH):
        mod = cond @ w_ada[l] + b_ada[l]
        sh1, sc1, gt1, sh2, sc2, gt2, sh3, sc3, gt3 = [m[:, None, :] for m in jnp.split(mod, N_MOD, axis=-1)]
        u = modulate(rms_norm(h, g_ffn1[l]), sh1, sc1)
        h = h + 0.5 * gt1 * swiglu(u, w_ffn1_in[l], w_ffn1_out[l])
        u = modulate(rms_norm(h, g_mix[l]), sh2, sc2)
        h = h + gt2 * hybrid_mixer(u, w_in[l], g_q_sb[l], g_k_sb[l], w_conv[l], a_log[l], dt_bias[l],
                                   g_dn_out[l], w_up_sb[l], w_up_dn[l], w_out[l])
        u = modulate(rms_norm(h, g_ffn2[l]), sh3, sc3)
        h = h + 0.5 * gt3 * swiglu(u, w_ffn2_in[l], w_ffn2_out[l])
    return h.astype(x.dtype)
```

```python
import functools

import jax
import jax.numpy as jnp
from jax import lax
from jax.experimental import pallas as pl
from jax.experimental.pallas import tpu as pltpu

F32 = jnp.float32
BF16 = jnp.bfloat16

D_MODEL = 1024
D_FF = 2816
N_MOD = 9
EPS = 1e-6

SB_HEADS = 8
SB_HEAD_DIM = 64
SB_W = SB_HEADS * SB_HEAD_DIM
DN_HEADS = 4
DN_DIM = 128
DN_W = DN_HEADS * DN_DIM
DN_CONV = 4
DN_CHUNK = 64

LANES = 128
SUBLANES = 8
MXU_DIM = 256
VMEM_LIMIT = 56 * 1024 * 1024

TOKEN_TILE = 512
FF_CHUNK = MXU_DIM
SB_TILE = 128
DN_STEP = 4 * DN_CHUNK

_dot = functools.partial(jnp.dot, preferred_element_type=F32)


def _dot_nt(a, b):
    return lax.dot_general(a, b, (((1,), (1,)), ((), ())), preferred_element_type=F32)


def _dot_tn(a, b):
    return lax.dot_general(a, b, (((0,), (0,)), ((), ())), preferred_element_type=F32)


def _sigmoid(x):
    return 1.0 / (1.0 + jnp.exp(-x))


def _softplus(x):
    return jnp.maximum(x, 0.0) + jnp.log(1.0 + jnp.exp(-jnp.abs(x)))


def _split_bf16(x, pieces):
    out = []
    for _ in range(pieces - 1):
        p = x.astype(BF16)
        out.append(p)
        x = x - p.astype(F32)
    out.append(x.astype(BF16))
    return out


def _rms_mod(x, gain, shift, scale):
    y = x * lax.rsqrt(jnp.mean(x * x, axis=-1, keepdims=True) + EPS)
    return (y * gain) * (1.0 + scale) + shift


def _resident(shape, index_map):
    return pl.BlockSpec(shape, index_map, pipeline_mode=pl.Buffered(1))


def _params(*semantics):
    return pltpu.CompilerParams(dimension_semantics=semantics, vmem_limit_bytes=VMEM_LIMIT)


def _mod_kernel(c_ref, w_ref, b_ref, o_ref):
    c = c_ref[...]
    cond = (c * _sigmoid(c)).astype(BF16)
    o_ref[...] = _dot(cond, w_ref[...].astype(BF16)) + b_ref[...]


def _modulation(c, w_ada, b_ada):
    bsz = c.shape[0]
    n = w_ada.shape[1]
    tn = D_MODEL
    return pl.pallas_call(
        _mod_kernel,
        grid=(n // tn,),
        in_specs=[pl.BlockSpec((bsz, D_MODEL), lambda j: (0, 0)),
                  pl.BlockSpec((D_MODEL, tn), lambda j: (0, j)),
                  pl.BlockSpec((1, tn), lambda j: (0, j))],
        out_specs=pl.BlockSpec((bsz, tn), lambda j: (0, j)),
        out_shape=jax.ShapeDtypeStruct((bsz, n), F32),
        compiler_params=_params("arbitrary"),
        name="adaln_mod",
    )(c, w_ada, b_ada.reshape(1, n))


def _ffn_kernel(x_ref, mod_ref, g_ref, wa_ref, wb_ref, wo_ref, out_ref, u_ref, h_ref, *, mod_row):
    shift = mod_ref[0, mod_row:mod_row + 1, :]
    scale = mod_ref[0, mod_row + 1:mod_row + 2, :]
    u_ref[...] = _rms_mod(x_ref[...], g_ref[...], shift, scale).astype(BF16)
    for c in range(D_FF // FF_CHUNK):
        cols = slice(c * FF_CHUNK, (c + 1) * FF_CHUNK)
        a = _dot(u_ref[...], wa_ref[:, cols])
        b = _dot(u_ref[...], wb_ref[:, cols])
        h_ref[:, cols] = (a * _sigmoid(a) * b).astype(BF16)
    for n in range(D_MODEL // MXU_DIM):
        cols = slice(n * MXU_DIM, (n + 1) * MXU_DIM)
        gate = mod_ref[0, mod_row + 2:mod_row + 3, cols]
        out_ref[:, cols] = x_ref[:, cols] + (0.5 * gate) * _dot(h_ref[...], wo_ref[:, cols])


def _ffn(x2, mod3, gain, w_in, w_out, mod_row, seq):
    tokens = x2.shape[0]
    tm = TOKEN_TILE
    tiles_per_seq = seq // tm
    return pl.pallas_call(
        functools.partial(_ffn_kernel, mod_row=mod_row),
        grid=(tokens // tm,),
        in_specs=[pl.BlockSpec((tm, D_MODEL), lambda i: (i, 0)),
                  pl.BlockSpec((1, N_MOD, D_MODEL), lambda i: (i // tiles_per_seq, 0, 0)),
                  _resident((1, D_MODEL), lambda i: (0, 0)),
                  _resident((D_MODEL, D_FF), lambda i: (0, 0)),
                  _resident((D_MODEL, D_FF), lambda i: (0, 1)),
                  _resident((D_FF, D_MODEL), lambda i: (0, 0))],
        out_specs=pl.BlockSpec((tm, D_MODEL), lambda i: (i, 0)),
        out_shape=jax.ShapeDtypeStruct((tokens, D_MODEL), F32),
        scratch_shapes=[pltpu.VMEM((tm, D_MODEL), BF16), pltpu.VMEM((tm, D_FF), BF16)],
        compiler_params=_params("parallel"),
        name=f"ffn_row{mod_row}",
    )(x2, mod3, gain.reshape(1, D_MODEL), w_in, w_in, w_out)


_PROJ_Q, _PROJ_K, _PROJ_V = 0, SB_W, 2 * SB_W
_PROJ_DN = 3 * SB_W
_PROJ_Z = _PROJ_DN + 3 * DN_W
_PROJ_MAIN_W = _PROJ_Z + DN_W


def _proj_kernel(h_ref, mod_ref, g_ref, wm_ref, wab_ref, wabt_ref, gq_ref, gk_ref, grp_ref,
                 q_ref, k_ref, v_ref, dn_ref, z_ref, ab_ref, abt_ref, u_ref):
    u_ref[...] = _rms_mod(h_ref[...], g_ref[...], mod_ref[0, 3:4, :], mod_ref[0, 4:5, :]).astype(BF16)

    def head_norm(t, gain_row):
        hi, lo = _split_bf16(t * t, 2)
        ss = _dot(hi, grp_ref[...]) + _dot(lo, grp_ref[...])
        return t * lax.rsqrt(ss * (1.0 / SB_HEAD_DIM) + EPS) * gain_row

    q = _dot(u_ref[...], wm_ref[:, _PROJ_Q:_PROJ_Q + SB_W])
    q_ref[...] = (head_norm(q, gq_ref[...]) * (SB_HEAD_DIM ** -0.5)).astype(BF16)
    k = _dot(u_ref[...], wm_ref[:, _PROJ_K:_PROJ_K + SB_W])
    k_ref[...] = head_norm(k, gk_ref[...]).astype(BF16)
    v_ref[...] = _dot(u_ref[...], wm_ref[:, _PROJ_V:_PROJ_V + SB_W]).astype(BF16)
    for c in range(3 * DN_W // MXU_DIM):
        cols = slice(c * MXU_DIM, (c + 1) * MXU_DIM)
        dn_ref[:, cols] = _dot(u_ref[...], wm_ref[:, _PROJ_DN + c * MXU_DIM:_PROJ_DN + (c + 1) * MXU_DIM])
    z_ref[...] = _dot(u_ref[...], wm_ref[:, _PROJ_Z:_PROJ_Z + DN_W])
    ab_ref[...] = _dot(u_ref[...], wab_ref[...])
    abt_ref[0] = _dot_nt(wabt_ref[...], u_ref[...])


def _projection(h2, mod3, g_mix, w_main, w_ab, w_abt, gq_row, gk_row, seq):
    tokens = h2.shape[0]
    bsz = tokens // seq
    tm = TOKEN_TILE
    tiles_per_seq = seq // tm
    grp = jnp.kron(jnp.eye(SB_HEADS, dtype=F32), jnp.ones((SB_HEAD_DIM, SB_HEAD_DIM), F32)).astype(BF16)
    row = lambda w: pl.BlockSpec((tm, w), lambda i: (i, 0))
    return pl.pallas_call(
        _proj_kernel,
        grid=(tokens // tm,),
        in_specs=[row(D_MODEL),
                  pl.BlockSpec((1, N_MOD, D_MODEL), lambda i: (i // tiles_per_seq, 0, 0)),
                  _resident((1, D_MODEL), lambda i: (0, 0)),
                  _resident((D_MODEL, _PROJ_MAIN_W), lambda i: (0, 0)),
                  _resident((D_MODEL, LANES), lambda i: (0, 0)),
                  _resident((2 * SUBLANES, D_MODEL), lambda i: (0, 0)),
                  _resident((1, SB_W), lambda i: (0, 0)),
                  _resident((1, SB_W), lambda i: (0, 0)),
                  _resident((SB_W, SB_W), lambda i: (0, 0))],
        out_specs=[row(SB_W), row(SB_W), row(SB_W), row(3 * DN_W), row(DN_W), row(LANES),
                   pl.BlockSpec((1, 2 * SUBLANES, tm), lambda i: (i // tiles_per_seq, 0, i % tiles_per_seq))],
        out_shape=[jax.ShapeDtypeStruct((tokens, SB_W), BF16),
                   jax.ShapeDtypeStruct((tokens, SB_W), BF16),
                   jax.ShapeDtypeStruct((tokens, SB_W), BF16),
                   jax.ShapeDtypeStruct((tokens, 3 * DN_W), F32),
                   jax.ShapeDtypeStruct((tokens, DN_W), F32),
                   jax.ShapeDtypeStruct((tokens, LANES), F32),
                   jax.ShapeDtypeStruct((bsz, 2 * SUBLANES, seq), F32)],
        scratch_shapes=[pltpu.VMEM((tm, D_MODEL), BF16)],
        compiler_params=_params("parallel"),
        name="mixer_in_proj",
    )(h2, mod3, g_mix.reshape(1, D_MODEL), w_main, w_ab, w_abt, gq_row, gk_row, grp)


def _sb_kernel(q_ref, k_ref, v_ref, o_ref, acc_ref, carry_ref):
    t = SB_TILE
    i = pl.program_id(2)
    q = q_ref[0]
    head_a = lax.broadcasted_iota(jnp.int32, (t, LANES), 1) < SB_HEAD_DIM
    zero = jnp.zeros_like(q)
    q2 = jnp.concatenate([jnp.where(head_a, q, zero), jnp.where(head_a, zero, q)], axis=0)
    r = lax.broadcasted_iota(jnp.int32, (2 * t, 2 * t), 0) & (t - 1)
    c = lax.broadcasted_iota(jnp.int32, (2 * t, 2 * t), 1)
    suffix = jnp.where((c >= t) | (r > c), 1.0, 0.0).astype(BF16)
    row = lax.broadcasted_iota(jnp.int32, (2 * t, t), 0) & (t - 1)
    col = lax.broadcasted_iota(jnp.int32, (2 * t, t), 1)
    causal = col < row

    def block(kb, diagonal):
        start = pl.multiple_of(kb * t, t)
        k = k_ref[0, pl.ds(start, t), :]
        v = v_ref[0, pl.ds(start, t), :]
        z = _dot_nt(q2, k)
        log_beta = jnp.minimum(z, 0.0) - jnp.log(1.0 + jnp.exp(-jnp.abs(z)))
        log_keep = log_beta - z
        if diagonal:
            log_keep = jnp.where(causal, log_keep, 0.0)
        hi, lo = _split_bf16(log_keep, 2)
        sums = _dot(jnp.concatenate([hi, lo], axis=1), suffix)
        carry = carry_ref[...]
        w = jnp.exp(log_beta + sums[:, :t] + carry)
        if diagonal:
            w = jnp.where(causal, w, 0.0)
        carry_ref[...] = carry + sums[:, t:]
        zero_v = jnp.zeros_like(v)
        v2 = jnp.concatenate([jnp.where(head_a, v, zero_v), jnp.where(head_a, zero_v, v)], axis=0)
        w2 = jnp.concatenate([w[:t], w[t:]], axis=1).astype(BF16)
        acc_ref[...] += _dot(w2, v2)

    acc_ref[...] = jnp.zeros_like(acc_ref)
    carry_ref[...] = jnp.zeros_like(carry_ref)
    block(i, True)

    def body(n, _):
        block(i - 1 - n, False)
        return 0

    lax.fori_loop(0, i, body, 0)
    o_ref[0] = acc_ref[...].astype(o_ref.dtype)


def _sb_attention(q, k, v):
    bsz, seq, _ = q.shape
    t = SB_TILE
    pairs = SB_W // LANES
    return pl.pallas_call(
        _sb_kernel,
        grid=(bsz, pairs, seq // t),
        in_specs=[pl.BlockSpec((1, t, LANES), lambda b, p, i: (b, i, p)),
                  pl.BlockSpec((1, seq, LANES), lambda b, p, i: (b, 0, p)),
                  pl.BlockSpec((1, seq, LANES), lambda b, p, i: (b, 0, p))],
        out_specs=pl.BlockSpec((1, t, LANES), lambda b, p, i: (b, i, p)),
        out_shape=jax.ShapeDtypeStruct((bsz, seq, SB_W), BF16),
        scratch_shapes=[pltpu.VMEM((t, LANES), F32), pltpu.VMEM((2 * t, LANES), F32)],
        compiler_params=_params("parallel", "parallel", "arbitrary"),
        name="stickbreak_attn",
    )(q, k, v)


def _gdn_kernel(qkv_ref, z_ref, ab_ref, abt_ref, wc_ref, prow_ref, pcol_ref, gout_ref, o_ref,
                state_ref, xbuf_ref, vn_ref, osc_ref):
    ts = DN_STEP
    hist = SUBLANES
    j = pl.program_id(1)

    @pl.when(j == 0)
    def _():
        state_ref[...] = jnp.zeros_like(state_ref)
        xbuf_ref[0:hist, :] = jnp.zeros((hist, 3 * DN_W), F32)

    xbuf_ref[hist:hist + ts, :] = qkv_ref[0]

    def conv_silu(col0):
        cols = slice(col0, col0 + DN_DIM)
        acc = xbuf_ref[hist:hist + ts, cols] * wc_ref[DN_CONV - 1:DN_CONV, cols]
        for tap in range(DN_CONV - 1):
            lo = hist - (DN_CONV - 1) + tap
            acc = acc + xbuf_ref[lo:lo + ts, cols] * wc_ref[tap:tap + 1, cols]
        return acc * _sigmoid(acc)

    ri = lax.broadcasted_iota(jnp.int32, (ts, ts), 0)
    ci = lax.broadcasted_iota(jnp.int32, (ts, ts), 1)
    same_chunk = (ri >> 6) == (ci >> 6)
    m_incl = same_chunk & (ci <= ri)
    m_strict = same_chunk & (ci < ri)
    eye = jnp.where(ri == ci, 1.0, 0.0)
    tri = jnp.where(m_incl, 1.0, 0.0).astype(BF16)
    tri_t = jnp.where(same_chunk & (ri <= ci), 1.0, 0.0).astype(BF16)

    ab = ab_ref[...]
    beta_all = _sigmoid(ab)
    g_all = -jnp.exp(prow_ref[0:1, :]) * _softplus(ab + prow_ref[1:2, :])
    abt = abt_ref[0]
    g_rows = -jnp.exp(pcol_ref[:, 0:1]) * _softplus(abt + pcol_ref[:, 1:2])
    gcum_all = sum(_dot(tri, p) for p in _split_bf16(g_all, 3))
    gcum_rows = sum(_dot(p, tri_t) for p in _split_bf16(g_rows, 3))

    for h in range(DN_HEADS):
        q = conv_silu(h * DN_DIM)
        k = conv_silu(DN_W + h * DN_DIM)
        v = conv_silu(2 * DN_W + h * DN_DIM)
        q = q * lax.rsqrt(jnp.sum(q * q, axis=-1, keepdims=True) + EPS) * (DN_DIM ** -0.5)
        k = k * lax.rsqrt(jnp.sum(k * k, axis=-1, keepdims=True) + EPS)
        beta = beta_all[:, h:h + 1]
        gc = gcum_all[:, DN_HEADS + h:DN_HEADS + h + 1]
        gr = gcum_rows[DN_HEADS + h:DN_HEADS + h + 1, :]
        decay = jnp.where(m_incl, jnp.exp(jnp.where(m_incl, gc - gr, 0.0)), 0.0)
        kb = k * beta
        vb = v * beta
        k16 = k.astype(BF16)
        lower = jnp.where(m_strict, _dot_nt(kb.astype(BF16), k16) * decay, 0.0)
        a_qk = _dot_nt(q.astype(BF16), k16) * decay
        eg = jnp.exp(gc)

        m8 = jnp.where((ri >> 3) == (ci >> 3), -lower, 0.0).astype(BF16)
        inv = eye + m8.astype(F32)
        p = _dot(m8, m8)
        inv = inv + _dot(inv.astype(BF16), p.astype(BF16))
        p = _dot(p.astype(BF16), p.astype(BF16))
        inv = inv + _dot(inv.astype(BF16), p.astype(BF16))
        for lg in (3, 4, 5):
            off = ((ri >> (lg + 1)) == (ci >> (lg + 1))) & (((ri >> lg) & 1) == 1) & (((ci >> lg) & 1) == 0)
            cb = jnp.where(off, lower, 0.0).astype(BF16)
            inv16 = inv.astype(BF16)
            inv = inv - _dot(inv16, _dot(cb, inv16).astype(BF16))
        rhs = jnp.concatenate([vb, kb * eg], axis=1).astype(BF16)
        sol = _dot(inv.astype(BF16), rhs)
        u = sol[:, :DN_DIM]
        w = sol[:, DN_DIM:]
        qg = q * eg

        vn_ref[...] = jnp.zeros_like(vn_ref)
        for c in range(ts // DN_CHUNK):
            rows = slice(c * DN_CHUNK, (c + 1) * DN_CHUNK)
            state = state_ref[h]
            wq = jnp.concatenate([w[rows], qg[rows]], axis=0).astype(BF16)
            res = _dot(wq, state.astype(BF16))
            v_new = u[rows] - res[:DN_CHUNK]
            vn_ref[rows, :] = v_new
            osc_ref[rows, :] = res[DN_CHUNK:] + _dot(a_qk[rows].astype(BF16), vn_ref[...].astype(BF16))
            g_last = gc[(c + 1) * DN_CHUNK - 1:(c + 1) * DN_CHUNK, :]
            k_dec = k[rows] * jnp.exp(g_last - gc[rows])
            state_ref[h] = state * jnp.exp(g_last) + _dot_tn(k_dec.astype(BF16), v_new.astype(BF16))

        o = osc_ref[...]
        o = o * lax.rsqrt(jnp.mean(o * o, axis=-1, keepdims=True) + EPS) * gout_ref[...]
        zh = z_ref[0, :, h * DN_DIM:(h + 1) * DN_DIM]
        o_ref[0, :, h * DN_DIM:(h + 1) * DN_DIM] = (o * (zh * _sigmoid(zh))).astype(o_ref.dtype)

    xbuf_ref[0:hist, :] = xbuf_ref[ts:ts + hist, :]


def _gated_deltanet(qkv, z, ab, abt, w_conv, prow, pcol, g_out):
    bsz, seq, _ = qkv.shape
    ts = DN_STEP
    return pl.pallas_call(
        _gdn_kernel,
        grid=(bsz, seq // ts),
        in_specs=[pl.BlockSpec((1, ts, 3 * DN_W), lambda b, j: (b, j, 0)),
                  pl.BlockSpec((1, ts, DN_W), lambda b, j: (b, j, 0)),
                  pl.BlockSpec((ts, LANES), lambda b, j: (b * (seq // ts) + j, 0)),
                  pl.BlockSpec((1, 2 * SUBLANES, ts), lambda b, j: (b, 0, j)),
                  pl.BlockSpec((DN_CONV, 3 * DN_W), lambda b, j: (0, 0)),
                  pl.BlockSpec((2, LANES), lambda b, j: (0, 0)),
                  pl.BlockSpec((2 * SUBLANES, LANES), lambda b, j: (0, 0)),
                  pl.BlockSpec((1, DN_DIM), lambda b, j: (0, 0))],
        out_specs=pl.BlockSpec((1, ts, DN_W), lambda b, j: (b, j, 0)),
        out_shape=jax.ShapeDtypeStruct((bsz, seq, DN_W), BF16),
        scratch_shapes=[pltpu.VMEM((DN_HEADS, DN_DIM, DN_DIM), F32),
                        pltpu.VMEM((ts + SUBLANES, 3 * DN_W), F32),
                        pltpu.VMEM((ts, DN_DIM), F32),
                        pltpu.VMEM((ts, DN_DIM), F32)],
        compiler_params=_params("parallel", "arbitrary"),
        name="gated_deltanet",
    )(qkv, z, ab, abt, w_conv, prow, pcol, g_out.reshape(1, DN_DIM))


def _merge_kernel(h_ref, attn_ref, dn_ref, mod_ref, g_ref, wr_ref, wus_ref, wud_ref, wo_ref, out_ref,
                  u_ref, m_ref):
    u_ref[...] = _rms_mod(h_ref[...], g_ref[...], mod_ref[0, 3:4, :], mod_ref[0, 4:5, :]).astype(BF16)
    for n in range(D_MODEL // MXU_DIM):
        cols = slice(n * MXU_DIM, (n + 1) * MXU_DIM)
        r_sb = _dot(u_ref[...], wr_ref[:, cols])
        r_dn = _dot(u_ref[...], wr_ref[:, D_MODEL + n * MXU_DIM:D_MODEL + (n + 1) * MXU_DIM])
        y_sb = _dot(attn_ref[...], wus_ref[:, cols])
        y_dn = _dot(dn_ref[...], wud_ref[:, cols])
        m_ref[:, cols] = (_sigmoid(r_sb) * y_sb + _sigmoid(r_dn) * y_dn).astype(BF16)
    for n in range(D_MODEL // MXU_DIM):
        cols = slice(n * MXU_DIM, (n + 1) * MXU_DIM)
        out_ref[:, cols] = h_ref[:, cols] + mod_ref[0, 5:6, cols] * _dot(m_ref[...], wo_ref[:, cols])


def _merge(h2, attn, dn, mod3, g_mix, w_r, w_up_sb, w_up_dn, w_out, seq):
    tokens = h2.shape[0]
    tm = TOKEN_TILE
    tiles_per_seq = seq // tm
    row = lambda w: pl.BlockSpec((tm, w), lambda i: (i, 0))
    return pl.pallas_call(
        _merge_kernel,
        grid=(tokens // tm,),
        in_specs=[row(D_MODEL), row(SB_W), row(DN_W),
                  pl.BlockSpec((1, N_MOD, D_MODEL), lambda i: (i // tiles_per_seq, 0, 0)),
                  _resident((1, D_MODEL), lambda i: (0, 0)),
                  _resident((D_MODEL, 2 * D_MODEL), lambda i: (0, 0)),
                  _resident((SB_W, D_MODEL), lambda i: (0, 0)),
                  _resident((DN_W, D_MODEL), lambda i: (0, 0)),
                  _resident((D_MODEL, D_MODEL), lambda i: (0, 0))],
        out_specs=row(D_MODEL),
        out_shape=jax.ShapeDtypeStruct((tokens, D_MODEL), F32),
        scratch_shapes=[pltpu.VMEM((tm, D_MODEL), BF16), pltpu.VMEM((tm, D_MODEL), BF16)],
        compiler_params=_params("parallel"),
        name="mixer_merge",
    )(h2, attn, dn, mod3, g_mix.reshape(1, D_MODEL), w_r, w_up_sb, w_up_dn, w_out)


def _layer(h2, mod3, seq, g_ffn1, w_ffn1_in, w_ffn1_out, g_mix, w_in, g_q_sb, g_k_sb, w_conv, a_log,
           dt_bias, g_dn_out, w_up_sb, w_up_dn, w_out, g_ffn2, w_ffn2_in, w_ffn2_out):
    tokens = h2.shape[0]
    bsz = tokens // seq
    h2 = _ffn(h2, mod3, g_ffn1, w_ffn1_in.astype(BF16), w_ffn1_out.astype(BF16), 0, seq)

    gate0 = _PROJ_MAIN_W
    gate1 = gate0 + 2 * DN_HEADS
    w_main = w_in[:, :gate0].astype(BF16)
    w_gate = w_in[:, gate0:gate1]
    w_ab = jnp.pad(w_gate, ((0, 0), (0, LANES - 2 * DN_HEADS))).astype(BF16)
    w_abt = jnp.pad(w_gate.T, ((0, 2 * SUBLANES - 2 * DN_HEADS), (0, 0))).astype(BF16)
    w_r = w_in[:, gate1:].astype(BF16)
    gq_row = jnp.tile(g_q_sb, SB_HEADS).reshape(1, SB_W)
    gk_row = jnp.tile(g_k_sb, SB_HEADS).reshape(1, SB_W)
    q, k, v, qkv_dn, z_dn, ab, abt = _projection(h2, mod3, g_mix, w_main, w_ab, w_abt, gq_row, gk_row, seq)

    attn = _sb_attention(q.reshape(bsz, seq, SB_W), k.reshape(bsz, seq, SB_W), v.reshape(bsz, seq, SB_W))

    lanes = slice(DN_HEADS, 2 * DN_HEADS)
    prow = jnp.zeros((2, LANES), F32).at[0, lanes].set(a_log).at[1, lanes].set(dt_bias)
    pcol = jnp.zeros((2 * SUBLANES, LANES), F32).at[lanes, 0].set(a_log).at[lanes, 1].set(dt_bias)
    dn = _gated_deltanet(qkv_dn.reshape(bsz, seq, 3 * DN_W), z_dn.reshape(bsz, seq, DN_W), ab, abt,
                         w_conv, prow, pcol, g_dn_out)

    h2 = _merge(h2, attn.reshape(tokens, SB_W), dn.reshape(tokens, DN_W), mod3, g_mix, w_r,
                w_up_sb.astype(BF16), w_up_dn.astype(BF16), w_out.astype(BF16), seq)
    return _ffn(h2, mod3, g_ffn2, w_ffn2_in.astype(BF16), w_ffn2_out.astype(BF16), 6, seq)


def kernel(x, c, w_ada, b_ada, g_ffn1, w_ffn1_in, w_ffn1_out, g_mix, w_in, g_q_sb, g_k_sb, w_conv, a_log,
           dt_bias, g_dn_out, w_up_sb, w_up_dn, w_out, g_ffn2, w_ffn2_in, w_ffn2_out):
    bsz, seq, d = x.shape
    depth = w_ada.shape[0]
    h2 = x.reshape(bsz * seq, d)
    for l in range(depth):
        mod3 = _modulation(c, w_ada[l], b_ada[l]).reshape(bsz, N_MOD, d)
        h2 = _layer(h2, mod3, seq, g_ffn1[l], w_ffn1_in[l], w_ffn1_out[l], g_mix[l], w_in[l], g_q_sb[l],
                    g_k_sb[l], w_conv[l], a_log[l], dt_bias[l], g_dn_out[l], w_up_sb[l], w_up_dn[l],
                    w_out[l], g_ffn2[l], w_ffn2_in[l], w_ffn2_out[l])
    return h2.reshape(bsz, seq, d).astype(x.dtype)
```

```python
import functools

import jax
import jax.numpy as jnp
from jax import lax
from jax.experimental import pallas as pl
from jax.experimental.pallas import tpu as pltpu

F32 = jnp.float32
BF16 = jnp.bfloat16

D_MODEL = 1024
D_FF = 2816
N_MOD = 9
EPS = 1e-6

SB_HEADS = 8
SB_HEAD_DIM = 64
SB_W = SB_HEADS * SB_HEAD_DIM
DN_HEADS = 4
DN_DIM = 128
DN_W = DN_HEADS * DN_DIM
DN_CONV = 4
DN_CHUNK = 64

LANES = 128
SUBLANES = 8
MXU_DIM = 256
VMEM_LIMIT = 56 * 1024 * 1024

TOKEN_TILE = 512
FF_CHUNK = MXU_DIM
SB_TILE = MXU_DIM
MASKED = -1e30
DN_STEP = 4 * DN_CHUNK

_dot = functools.partial(jnp.dot, preferred_element_type=F32)


def _dot_nt(a, b):
    return lax.dot_general(a, b, (((1,), (1,)), ((), ())), preferred_element_type=F32)


def _dot_tn(a, b):
    return lax.dot_general(a, b, (((0,), (0,)), ((), ())), preferred_element_type=F32)


def _sigmoid(x):
    return 1.0 / (1.0 + jnp.exp(-x))


def _softplus(x):
    return jnp.maximum(x, 0.0) + jnp.log(1.0 + jnp.exp(-jnp.abs(x)))


def _split_bf16(x, pieces):
    out = []
    for _ in range(pieces - 1):
        p = x.astype(BF16)
        out.append(p)
        x = x - p.astype(F32)
    out.append(x.astype(BF16))
    return out


def _rms_mod(x, gain, shift, scale):
    y = x * lax.rsqrt(jnp.mean(x * x, axis=-1, keepdims=True) + EPS)
    return (y * gain) * (1.0 + scale) + shift


def _resident(shape, index_map):
    return pl.BlockSpec(shape, index_map, pipeline_mode=pl.Buffered(1))


def _params(*semantics):
    return pltpu.CompilerParams(dimension_semantics=semantics, vmem_limit_bytes=VMEM_LIMIT)


def _mod_kernel(c_ref, w_ref, b_ref, o_ref):
    c = c_ref[...]
    cond = (c * _sigmoid(c)).astype(BF16)
    o_ref[...] = _dot(cond, w_ref[...].astype(BF16)) + b_ref[...]


def _modulation(c, w_ada, b_ada):
    bsz = c.shape[0]
    n = w_ada.shape[1]
    tn = D_MODEL
    return pl.pallas_call(
        _mod_kernel,
        grid=(n // tn,),
        in_specs=[pl.BlockSpec((bsz, D_MODEL), lambda j: (0, 0)),
                  pl.BlockSpec((D_MODEL, tn), lambda j: (0, j)),
                  pl.BlockSpec((1, tn), lambda j: (0, j))],
        out_specs=pl.BlockSpec((bsz, tn), lambda j: (0, j)),
        out_shape=jax.ShapeDtypeStruct((bsz, n), F32),
        compiler_params=_params("arbitrary"),
        name="adaln_mod",
    )(c, w_ada, b_ada.reshape(1, n))


def _ffn_kernel(x_ref, mod_ref, g_ref, wa_ref, wb_ref, wo_ref, out_ref, u_ref, h_ref, *, mod_row):
    shift = mod_ref[0, mod_row:mod_row + 1, :]
    scale = mod_ref[0, mod_row + 1:mod_row + 2, :]
    u_ref[...] = _rms_mod(x_ref[...], g_ref[...], shift, scale).astype(BF16)
    for c in range(D_FF // FF_CHUNK):
        cols = slice(c * FF_CHUNK, (c + 1) * FF_CHUNK)
        a = _dot(u_ref[...], wa_ref[:, cols])
        b = _dot(u_ref[...], wb_ref[:, cols])
        h_ref[:, cols] = (a * _sigmoid(a) * b).astype(BF16)
    for n in range(D_MODEL // MXU_DIM):
        cols = slice(n * MXU_DIM, (n + 1) * MXU_DIM)
        gate = mod_ref[0, mod_row + 2:mod_row + 3, cols]
        out_ref[:, cols] = x_ref[:, cols] + (0.5 * gate) * _dot(h_ref[...], wo_ref[:, cols])


def _ffn(x2, mod3, gain, w_in, w_out, mod_row, seq):
    tokens = x2.shape[0]
    tm = TOKEN_TILE
    tiles_per_seq = seq // tm
    return pl.pallas_call(
        functools.partial(_ffn_kernel, mod_row=mod_row),
        grid=(tokens // tm,),
        in_specs=[pl.BlockSpec((tm, D_MODEL), lambda i: (i, 0)),
                  pl.BlockSpec((1, N_MOD, D_MODEL), lambda i: (i // tiles_per_seq, 0, 0)),
                  _resident((1, D_MODEL), lambda i: (0, 0)),
                  _resident((D_MODEL, D_FF), lambda i: (0, 0)),
                  _resident((D_MODEL, D_FF), lambda i: (0, 1)),
                  _resident((D_FF, D_MODEL), lambda i: (0, 0))],
        out_specs=pl.BlockSpec((tm, D_MODEL), lambda i: (i, 0)),
        out_shape=jax.ShapeDtypeStruct((tokens, D_MODEL), F32),
        scratch_shapes=[pltpu.VMEM((tm, D_MODEL), BF16), pltpu.VMEM((tm, D_FF), BF16)],
        compiler_params=_params("parallel"),
        name=f"ffn_row{mod_row}",
    )(x2, mod3, gain.reshape(1, D_MODEL), w_in, w_in, w_out)


_PROJ_Q, _PROJ_K, _PROJ_V = 0, SB_W, 2 * SB_W
_PROJ_DN = 3 * SB_W
_PROJ_Z = _PROJ_DN + 3 * DN_W
_PROJ_MAIN_W = _PROJ_Z + DN_W


def _proj_kernel(h_ref, mod_ref, g_ref, wm_ref, wab_ref, wabt_ref, gq_ref, gk_ref, grp_ref,
                 q_ref, k_ref, v_ref, dn_ref, z_ref, ab_ref, abt_ref, u_ref):
    u_ref[...] = _rms_mod(h_ref[...], g_ref[...], mod_ref[0, 3:4, :], mod_ref[0, 4:5, :]).astype(BF16)

    def head_norm(t, gain_row):
        hi, lo = _split_bf16(t * t, 2)
        ss = _dot(hi, grp_ref[...]) + _dot(lo, grp_ref[...])
        return t * lax.rsqrt(ss * (1.0 / SB_HEAD_DIM) + EPS) * gain_row

    q = _dot(u_ref[...], wm_ref[:, _PROJ_Q:_PROJ_Q + SB_W])
    q_ref[...] = (head_norm(q, gq_ref[...]) * (SB_HEAD_DIM ** -0.5)).astype(BF16)
    k = _dot(u_ref[...], wm_ref[:, _PROJ_K:_PROJ_K + SB_W])
    k_ref[...] = head_norm(k, gk_ref[...]).astype(BF16)
    v_ref[...] = _dot(u_ref[...], wm_ref[:, _PROJ_V:_PROJ_V + SB_W]).astype(BF16)
    for c in range(3 * DN_W // MXU_DIM):
        cols = slice(c * MXU_DIM, (c + 1) * MXU_DIM)
        dn_ref[:, cols] = _dot(u_ref[...], wm_ref[:, _PROJ_DN + c * MXU_DIM:_PROJ_DN + (c + 1) * MXU_DIM])
    z_ref[...] = _dot(u_ref[...], wm_ref[:, _PROJ_Z:_PROJ_Z + DN_W])
    ab_ref[...] = _dot(u_ref[...], wab_ref[...])
    abt_ref[0] = _dot_nt(wabt_ref[...], u_ref[...])


def _projection(h2, mod3, g_mix, w_main, w_ab, w_abt, gq_row, gk_row, seq):
    tokens = h2.shape[0]
    bsz = tokens // seq
    tm = TOKEN_TILE
    tiles_per_seq = seq // tm
    grp = jnp.kron(jnp.eye(SB_HEADS, dtype=F32), jnp.ones((SB_HEAD_DIM, SB_HEAD_DIM), F32)).astype(BF16)
    row = lambda w: pl.BlockSpec((tm, w), lambda i: (i, 0))
    return pl.pallas_call(
        _proj_kernel,
        grid=(tokens // tm,),
        in_specs=[row(D_MODEL),
                  pl.BlockSpec((1, N_MOD, D_MODEL), lambda i: (i // tiles_per_seq, 0, 0)),
                  _resident((1, D_MODEL), lambda i: (0, 0)),
                  _resident((D_MODEL, _PROJ_MAIN_W), lambda i: (0, 0)),
                  _resident((D_MODEL, LANES), lambda i: (0, 0)),
                  _resident((2 * SUBLANES, D_MODEL), lambda i: (0, 0)),
                  _resident((1, SB_W), lambda i: (0, 0)),
                  _resident((1, SB_W), lambda i: (0, 0)),
                  _resident((SB_W, SB_W), lambda i: (0, 0))],
        out_specs=[row(SB_W), row(SB_W), row(SB_W), row(3 * DN_W), row(DN_W), row(LANES),
                   pl.BlockSpec((1, 2 * SUBLANES, tm), lambda i: (i // tiles_per_seq, 0, i % tiles_per_seq))],
        out_shape=[jax.ShapeDtypeStruct((tokens, SB_W), BF16),
                   jax.ShapeDtypeStruct((tokens, SB_W), BF16),
                   jax.ShapeDtypeStruct((tokens, SB_W), BF16),
                   jax.ShapeDtypeStruct((tokens, 3 * DN_W), F32),
                   jax.ShapeDtypeStruct((tokens, DN_W), F32),
                   jax.ShapeDtypeStruct((tokens, LANES), F32),
                   jax.ShapeDtypeStruct((bsz, 2 * SUBLANES, seq), F32)],
        scratch_shapes=[pltpu.VMEM((tm, D_MODEL), BF16)],
        compiler_params=_params("parallel"),
        name="mixer_in_proj",
    )(h2, mod3, g_mix.reshape(1, D_MODEL), w_main, w_ab, w_abt, gq_row, gk_row, grp)


def _sb_kernel(q_ref, k_ref, v_ref, o_ref, acc_ref, carry_ref, lb_ref, lk_ref):
    t = SB_TILE
    i = pl.program_id(2)
    q = q_ref[0]
    zero_q = jnp.zeros_like(q)
    head_a = lax.broadcasted_iota(jnp.int32, (t, LANES), 1) < SB_HEAD_DIM
    q2 = jnp.concatenate([jnp.where(head_a, q, zero_q), jnp.where(head_a, zero_q, q)], axis=0)
    suffix = jnp.where(lax.broadcasted_iota(jnp.int32, (t, t), 0) > lax.broadcasted_iota(jnp.int32, (t, t), 1),
                       1.0, 0.0).astype(BF16)
    causal = (lax.broadcasted_iota(jnp.int32, (2 * t, t), 1)
              < (lax.broadcasted_iota(jnp.int32, (2 * t, t), 0) & (t - 1)))

    def logits(kb, slot, diagonal=False):
        k = k_ref[0, pl.ds(pl.multiple_of(kb * t, t), t), :]
        z = _dot_nt(q2, k)
        log_beta = jnp.minimum(z, 0.0) - jnp.log(1.0 + jnp.exp(-jnp.abs(z)))
        log_keep = log_beta - z
        if diagonal:
            log_beta = jnp.where(causal, log_beta, MASKED)
            log_keep = jnp.where(causal, log_keep, 0.0)
        lb_ref[slot] = log_beta
        lk_ref[slot] = log_keep.astype(BF16)

    def weigh(kb, slot):
        v = v_ref[0, pl.ds(pl.multiple_of(kb * t, t), t), :]
        log_keep = lk_ref[slot]
        sums = _dot(log_keep, suffix)
        carry = carry_ref[...]
        w = jnp.exp(lb_ref[slot] + sums + carry)
        carry_ref[...] = carry + (sums[:, 0:1] + log_keep[:, 0:1].astype(F32))
        zero_v = jnp.zeros_like(v)
        v2 = jnp.concatenate([jnp.where(head_a, v, zero_v), jnp.where(head_a, zero_v, v)], axis=0)
        w2 = jnp.concatenate([w[:t], w[t:]], axis=1).astype(BF16)
        acc_ref[...] += _dot(w2, v2)

    acc_ref[...] = jnp.zeros_like(acc_ref)
    carry_ref[...] = jnp.zeros_like(carry_ref)
    logits(i, 0, diagonal=True)

    def pair(m, _):
        kb = i - 2 * m
        weigh(kb, 0)
        logits(kb - 1, 1)
        weigh(kb - 1, 1)
        logits(kb - 2, 0)
        return 0

    lax.fori_loop(0, i // 2, pair, 0)

    @pl.when(i % 2 == 1)
    def _():
        weigh(1, 0)
        logits(0, 1)
        weigh(0, 1)

    @pl.when(i % 2 == 0)
    def _():
        weigh(0, 0)

    o_ref[0] = acc_ref[...].astype(o_ref.dtype)


def _sb_attention(q, k, v):
    bsz, seq, _ = q.shape
    tq = SB_TILE
    pairs = SB_W // LANES
    return pl.pallas_call(
        _sb_kernel,
        grid=(bsz, pairs, seq // tq),
        in_specs=[pl.BlockSpec((1, tq, LANES), lambda b, p, i: (b, i, p)),
                  pl.BlockSpec((1, seq, LANES), lambda b, p, i: (b, 0, p)),
                  pl.BlockSpec((1, seq, LANES), lambda b, p, i: (b, 0, p))],
        out_specs=pl.BlockSpec((1, tq, LANES), lambda b, p, i: (b, i, p)),
        out_shape=jax.ShapeDtypeStruct((bsz, seq, SB_W), BF16),
        scratch_shapes=[pltpu.VMEM((tq, LANES), F32), pltpu.VMEM((2 * tq, 1), F32),
                        pltpu.VMEM((2, 2 * tq, tq), F32), pltpu.VMEM((2, 2 * tq, tq), BF16)],
        compiler_params=_params("parallel", "parallel", "arbitrary"),
        name="stickbreak_attn",
    )(q, k, v)


def _gdn_kernel(qkv_ref, z_ref, ab_ref, abt_ref, wc_ref, prow_ref, pcol_ref, gout_ref, o_ref,
                state_ref, xbuf_ref, vn_ref, osc_ref):
    ts = DN_STEP
    hist = SUBLANES
    j = pl.program_id(1)

    @pl.when(j == 0)
    def _():
        state_ref[...] = jnp.zeros_like(state_ref)
        xbuf_ref[0:hist, :] = jnp.zeros((hist, 3 * DN_W), F32)

    xbuf_ref[hist:hist + ts, :] = qkv_ref[0]

    def conv_silu(col0):
        cols = slice(col0, col0 + DN_DIM)
        acc = xbuf_ref[hist:hist + ts, cols] * wc_ref[DN_CONV - 1:DN_CONV, cols]
        for tap in range(DN_CONV - 1):
            lo = hist - (DN_CONV - 1) + tap
            acc = acc + xbuf_ref[lo:lo + ts, cols] * wc_ref[tap:tap + 1, cols]
        return acc * _sigmoid(acc)

    ri = lax.broadcasted_iota(jnp.int32, (ts, ts), 0)
    ci = lax.broadcasted_iota(jnp.int32, (ts, ts), 1)
    same_chunk = (ri >> 6) == (ci >> 6)
    m_incl = same_chunk & (ci <= ri)
    m_strict = same_chunk & (ci < ri)
    eye = jnp.where(ri == ci, 1.0, 0.0)
    tri = jnp.where(m_incl, 1.0, 0.0).astype(BF16)
    tri_t = jnp.where(same_chunk & (ri <= ci), 1.0, 0.0).astype(BF16)

    ab = ab_ref[...]
    beta_all = _sigmoid(ab)
    g_all = -jnp.exp(prow_ref[0:1, :]) * _softplus(ab + prow_ref[1:2, :])
    abt = abt_ref[0]
    g_rows = -jnp.exp(pcol_ref[:, 0:1]) * _softplus(abt + pcol_ref[:, 1:2])
    gcum_all = sum(_dot(tri, p) for p in _split_bf16(g_all, 3))
    gcum_rows = sum(_dot(p, tri_t) for p in _split_bf16(g_rows, 3))

    heads = range(DN_HEADS)
    q = [conv_silu(h * DN_DIM) for h in heads]
    k = [conv_silu(DN_W + h * DN_DIM) for h in heads]
    v = [conv_silu(2 * DN_W + h * DN_DIM) for h in heads]
    q = [x * lax.rsqrt(jnp.sum(x * x, axis=-1, keepdims=True) + EPS) * (DN_DIM ** -0.5) for x in q]
    k = [x * lax.rsqrt(jnp.sum(x * x, axis=-1, keepdims=True) + EPS) for x in k]
    beta = [beta_all[:, h:h + 1] for h in heads]
    gc = [gcum_all[:, DN_HEADS + h:DN_HEADS + h + 1] for h in heads]
    gr = [gcum_rows[DN_HEADS + h:DN_HEADS + h + 1, :] for h in heads]
    decay = [jnp.where(m_incl, jnp.exp(jnp.where(m_incl, gc[h] - gr[h], 0.0)), 0.0) for h in heads]
    kb = [k[h] * beta[h] for h in heads]
    vb = [v[h] * beta[h] for h in heads]
    eg = [jnp.exp(gc[h]) for h in heads]
    qg = [q[h] * eg[h] for h in heads]
    kq = [_dot_nt(jnp.concatenate([kb[h], q[h]], axis=0).astype(BF16), k[h].astype(BF16)) for h in heads]
    lower = [jnp.where(m_strict, kq[h][:ts] * decay[h], 0.0) for h in heads]
    a_qk = [(kq[h][ts:] * decay[h]).astype(BF16) for h in heads]

    blk8 = (ri >> 3) == (ci >> 3)
    m8 = [jnp.where(blk8, -lower[h], 0.0).astype(BF16) for h in heads]
    inv = [eye + m8[h].astype(F32) for h in heads]
    p = [_dot(m8[h], m8[h]).astype(BF16) for h in heads]
    inv = [inv[h] + _dot(inv[h].astype(BF16), p[h]) for h in heads]
    p = [_dot(p[h], p[h]).astype(BF16) for h in heads]
    inv = [inv[h] + _dot(inv[h].astype(BF16), p[h]) for h in heads]
    for lg in (3, 4, 5):
        off = ((ri >> (lg + 1)) == (ci >> (lg + 1))) & (((ri >> lg) & 1) == 1) & (((ci >> lg) & 1) == 0)
        inv16 = [inv[h].astype(BF16) for h in heads]
        cb = [_dot(jnp.where(off, lower[h], 0.0).astype(BF16), inv16[h]).astype(BF16) for h in heads]
        inv = [inv[h] - _dot(inv16[h], cb[h]) for h in heads]
    sol = [_dot(inv[h].astype(BF16), jnp.concatenate([vb[h], kb[h] * eg[h]], axis=1).astype(BF16))
           for h in heads]

    vn_ref[...] = jnp.zeros_like(vn_ref)
    for c in range(ts // DN_CHUNK):
        rows = slice(c * DN_CHUNK, (c + 1) * DN_CHUNK)
        for h in heads:
            state = state_ref[h]
            wq = jnp.concatenate([sol[h][rows, DN_DIM:], qg[h][rows]], axis=0).astype(BF16)
            res = _dot(wq, state.astype(BF16))
            v_new = sol[h][rows, :DN_DIM] - res[:DN_CHUNK]
            vn_ref[h, rows, :] = v_new
            osc_ref[h, rows, :] = res[DN_CHUNK:] + _dot(a_qk[h][rows], vn_ref[h].astype(BF16))
            g_last = gc[h][(c + 1) * DN_CHUNK - 1:(c + 1) * DN_CHUNK, :]
            k_dec = k[h][rows] * jnp.exp(g_last - gc[h][rows])
            state_ref[h] = state * jnp.exp(g_last) + _dot_tn(k_dec.astype(BF16), v_new.astype(BF16))

    for h in heads:
        o = osc_ref[h]
        o = o * lax.rsqrt(jnp.mean(o * o, axis=-1, keepdims=True) + EPS) * gout_ref[...]
        zh = z_ref[0, :, h * DN_DIM:(h + 1) * DN_DIM]
        o_ref[0, :, h * DN_DIM:(h + 1) * DN_DIM] = (o * (zh * _sigmoid(zh))).astype(o_ref.dtype)

    xbuf_ref[0:hist, :] = xbuf_ref[ts:ts + hist, :]


def _gated_deltanet(qkv, z, ab, abt, w_conv, prow, pcol, g_out):
    bsz, seq, _ = qkv.shape
    ts = DN_STEP
    return pl.pallas_call(
        _gdn_kernel,
        grid=(bsz, seq // ts),
        in_specs=[pl.BlockSpec((1, ts, 3 * DN_W), lambda b, j: (b, j, 0)),
                  pl.BlockSpec((1, ts, DN_W), lambda b, j: (b, j, 0)),
                  pl.BlockSpec((ts, LANES), lambda b, j: (b * (seq // ts) + j, 0)),
                  pl.BlockSpec((1, 2 * SUBLANES, ts), lambda b, j: (b, 0, j)),
                  pl.BlockSpec((DN_CONV, 3 * DN_W), lambda b, j: (0, 0)),
                  pl.BlockSpec((2, LANES), lambda b, j: (0, 0)),
                  pl.BlockSpec((2 * SUBLANES, LANES), lambda b, j: (0, 0)),
                  pl.BlockSpec((1, DN_DIM), lambda b, j: (0, 0))],
        out_specs=pl.BlockSpec((1, ts, DN_W), lambda b, j: (b, j, 0)),
        out_shape=jax.ShapeDtypeStruct((bsz, seq, DN_W), BF16),
        scratch_shapes=[pltpu.VMEM((DN_HEADS, DN_DIM, DN_DIM), F32),
                        pltpu.VMEM((ts + SUBLANES, 3 * DN_W), F32),
                        pltpu.VMEM((DN_HEADS, ts, DN_DIM), F32),
                        pltpu.VMEM((DN_HEADS, ts, DN_DIM), F32)],
        compiler_params=_params("parallel", "arbitrary"),
        name="gated_deltanet",
    )(qkv, z, ab, abt, w_conv, prow, pcol, g_out.reshape(1, DN_DIM))


def _merge_kernel(h_ref, attn_ref, dn_ref, mod_ref, g_ref, wr_ref, wus_ref, wud_ref, wo_ref, out_ref,
                  u_ref, m_ref):
    u_ref[...] = _rms_mod(h_ref[...], g_ref[...], mod_ref[0, 3:4, :], mod_ref[0, 4:5, :]).astype(BF16)
    for n in range(D_MODEL // MXU_DIM):
        cols = slice(n * MXU_DIM, (n + 1) * MXU_DIM)
        r_sb = _dot(u_ref[...], wr_ref[:, cols])
        r_dn = _dot(u_ref[...], wr_ref[:, D_MODEL + n * MXU_DIM:D_MODEL + (n + 1) * MXU_DIM])
        y_sb = _dot(attn_ref[...], wus_ref[:, cols])
        y_dn = _dot(dn_ref[...], wud_ref[:, cols])
        m_ref[:, cols] = (_sigmoid(r_sb) * y_sb + _sigmoid(r_dn) * y_dn).astype(BF16)
    for n in range(D_MODEL // MXU_DIM):
        cols = slice(n * MXU_DIM, (n + 1) * MXU_DIM)
        out_ref[:, cols] = h_ref[:, cols] + mod_ref[0, 5:6, cols] * _dot(m_ref[...], wo_ref[:, cols])


def _merge(h2, attn, dn, mod3, g_mix, w_r, w_up_sb, w_up_dn, w_out, seq):
    tokens = h2.shape[0]
    tm = TOKEN_TILE
    tiles_per_seq = seq // tm
    row = lambda w: pl.BlockSpec((tm, w), lambda i: (i, 0))
    return pl.pallas_call(
        _merge_kernel,
        grid=(tokens // tm,),
        in_specs=[row(D_MODEL), row(SB_W), row(DN_W),
                  pl.BlockSpec((1, N_MOD, D_MODEL), lambda i: (i // tiles_per_seq, 0, 0)),
                  _resident((1, D_MODEL), lambda i: (0, 0)),
                  _resident((D_MODEL, 2 * D_MODEL), lambda i: (0, 0)),
                  _resident((SB_W, D_MODEL), lambda i: (0, 0)),
                  _resident((DN_W, D_MODEL), lambda i: (0, 0)),
                  _resident((D_MODEL, D_MODEL), lambda i: (0, 0))],
        out_specs=row(D_MODEL),
        out_shape=jax.ShapeDtypeStruct((tokens, D_MODEL), F32),
        scratch_shapes=[pltpu.VMEM((tm, D_MODEL), BF16), pltpu.VMEM((tm, D_MODEL), BF16)],
        compiler_params=_params("parallel"),
        name="mixer_merge",
    )(h2, attn, dn, mod3, g_mix.reshape(1, D_MODEL), w_r, w_up_sb, w_up_dn, w_out)


def _layer(h2, mod3, seq, g_ffn1, w_ffn1_in, w_ffn1_out, g_mix, w_in, g_q_sb, g_k_sb, w_conv, a_log,
           dt_bias, g_dn_out, w_up_sb, w_up_dn, w_out, g_ffn2, w_ffn2_in, w_ffn2_out):
    tokens = h2.shape[0]
    bsz = tokens // seq
    h2 = _ffn(h2, mod3, g_ffn1, w_ffn1_in.astype(BF16), w_ffn1_out.astype(BF16), 0, seq)

    gate0 = _PROJ_MAIN_W
    gate1 = gate0 + 2 * DN_HEADS
    w_main = w_in[:, :gate0].astype(BF16)
    w_gate = w_in[:, gate0:gate1]
    w_ab = jnp.pad(w_gate, ((0, 0), (0, LANES - 2 * DN_HEADS))).astype(BF16)
    w_abt = jnp.pad(w_gate.T, ((0, 2 * SUBLANES - 2 * DN_HEADS), (0, 0))).astype(BF16)
    w_r = w_in[:, gate1:].astype(BF16)
    gq_row = jnp.tile(g_q_sb, SB_HEADS).reshape(1, SB_W)
    gk_row = jnp.tile(g_k_sb, SB_HEADS).reshape(1, SB_W)
    q, k, v, qkv_dn, z_dn, ab, abt = _projection(h2, mod3, g_mix, w_main, w_ab, w_abt, gq_row, gk_row, seq)

    attn = _sb_attention(q.reshape(bsz, seq, SB_W), k.reshape(bsz, seq, SB_W), v.reshape(bsz, seq, SB_W))

    lanes = slice(DN_HEADS, 2 * DN_HEADS)
    prow = jnp.zeros((2, LANES), F32).at[0, lanes].set(a_log).at[1, lanes].set(dt_bias)
    pcol = jnp.zeros((2 * SUBLANES, LANES), F32).at[lanes, 0].set(a_log).at[lanes, 1].set(dt_bias)
    dn = _gated_deltanet(qkv_dn.reshape(bsz, seq, 3 * DN_W), z_dn.reshape(bsz, seq, DN_W), ab, abt,
                         w_conv, prow, pcol, g_dn_out)

    h2 = _merge(h2, attn.reshape(tokens, SB_W), dn.reshape(tokens, DN_W), mod3, g_mix, w_r,
                w_up_sb.astype(BF16), w_up_dn.astype(BF16), w_out.astype(BF16), seq)
    return _ffn(h2, mod3, g_ffn2, w_ffn2_in.astype(BF16), w_ffn2_out.astype(BF16), 6, seq)


def kernel(x, c, w_ada, b_ada, g_ffn1, w_ffn1_in, w_ffn1_out, g_mix, w_in, g_q_sb, g_k_sb, w_conv, a_log,
           dt_bias, g_dn_out, w_up_sb, w_up_dn, w_out, g_ffn2, w_ffn2_in, w_ffn2_out):
    bsz, seq, d = x.shape
    depth = w_ada.shape[0]
    h2 = x.reshape(bsz * seq, d)
    for l in range(depth):
        mod3 = _modulation(c, w_ada[l], b_ada[l]).reshape(bsz, N_MOD, d)
        h2 = _layer(h2, mod3, seq, g_ffn1[l], w_ffn1_in[l], w_ffn1_out[l], g_mix[l], w_in[l], g_q_sb[l],
                    g_k_sb[l], w_conv[l], a_log[l], dt_bias[l], g_dn_out[l], w_up_sb[l], w_up_dn[l],
                    w_out[l], g_ffn2[l], w_ffn2_in[l], w_ffn2_out[l])
    return h2.reshape(bsz, seq, d).astype(x.dtype)
```

```python
import functools

import jax
import jax.numpy as jnp
from jax import lax
from jax.experimental import pallas as pl
from jax.experimental.pallas import tpu as pltpu

F32 = jnp.float32
BF16 = jnp.bfloat16

D_MODEL = 1024
D_FF = 2816
N_MOD = 9
EPS = 1e-6

SB_HEADS = 8
SB_HEAD_DIM = 64
SB_W = SB_HEADS * SB_HEAD_DIM
DN_HEADS = 4
DN_DIM = 128
DN_W = DN_HEADS * DN_DIM
DN_CONV = 4
DN_CHUNK = 64

LANES = 128
SUBLANES = 8
MXU_DIM = 256
VMEM_LIMIT = 56 * 1024 * 1024

TOKEN_TILE = 512
FF_CHUNK = MXU_DIM
SB_TILE = MXU_DIM
MASKED = -1e30
SB_DEAD = -104.0
DN_STEP = 4 * DN_CHUNK

_dot = functools.partial(jnp.dot, preferred_element_type=F32)


def _dot_nt(a, b):
    return lax.dot_general(a, b, (((1,), (1,)), ((), ())), preferred_element_type=F32)


def _dot_tn(a, b):
    return lax.dot_general(a, b, (((0,), (0,)), ((), ())), preferred_element_type=F32)


def _sigmoid(x):
    return 1.0 / (1.0 + jnp.exp(-x))


def _softplus(x):
    return jnp.maximum(x, 0.0) + jnp.log(1.0 + jnp.exp(-jnp.abs(x)))


def _split_bf16(x, pieces):
    out = []
    for _ in range(pieces - 1):
        p = x.astype(BF16)
        out.append(p)
        x = x - p.astype(F32)
    out.append(x.astype(BF16))
    return out


def _rms_mod(x, gain, shift, scale):
    y = x * lax.rsqrt(jnp.mean(x * x, axis=-1, keepdims=True) + EPS)
    return (y * gain) * (1.0 + scale) + shift


def _resident(shape, index_map):
    return pl.BlockSpec(shape, index_map, pipeline_mode=pl.Buffered(1))


def _params(*semantics):
    return pltpu.CompilerParams(dimension_semantics=semantics, vmem_limit_bytes=VMEM_LIMIT)


def _mod_kernel(c_ref, w_ref, b_ref, o_ref):
    c = c_ref[...]
    cond = (c * _sigmoid(c)).astype(BF16)
    o_ref[...] = _dot(cond, w_ref[...].astype(BF16)) + b_ref[...]


def _modulation(c, w_ada, b_ada):
    bsz = c.shape[0]
    n = w_ada.shape[1]
    tn = D_MODEL
    return pl.pallas_call(
        _mod_kernel,
        grid=(n // tn,),
        in_specs=[pl.BlockSpec((bsz, D_MODEL), lambda j: (0, 0)),
                  pl.BlockSpec((D_MODEL, tn), lambda j: (0, j)),
                  pl.BlockSpec((1, tn), lambda j: (0, j))],
        out_specs=pl.BlockSpec((bsz, tn), lambda j: (0, j)),
        out_shape=jax.ShapeDtypeStruct((bsz, n), F32),
        compiler_params=_params("arbitrary"),
        name="adaln_mod",
    )(c, w_ada, b_ada.reshape(1, n))


def _ffn_kernel(x_ref, mod_ref, g_ref, wa_ref, wb_ref, wo_ref, out_ref, u_ref, h_ref, *, mod_row):
    shift = mod_ref[0, mod_row:mod_row + 1, :]
    scale = mod_ref[0, mod_row + 1:mod_row + 2, :]
    u_ref[...] = _rms_mod(x_ref[...], g_ref[...], shift, scale).astype(BF16)
    for c in range(D_FF // FF_CHUNK):
        cols = slice(c * FF_CHUNK, (c + 1) * FF_CHUNK)
        a = _dot(u_ref[...], wa_ref[:, cols])
        b = _dot(u_ref[...], wb_ref[:, cols])
        h_ref[:, cols] = (a * _sigmoid(a) * b).astype(BF16)
    for n in range(D_MODEL // MXU_DIM):
        cols = slice(n * MXU_DIM, (n + 1) * MXU_DIM)
        gate = mod_ref[0, mod_row + 2:mod_row + 3, cols]
        out_ref[:, cols] = x_ref[:, cols] + (0.5 * gate) * _dot(h_ref[...], wo_ref[:, cols])


def _ffn(x2, mod3, gain, w_in, w_out, mod_row, seq):
    tokens = x2.shape[0]
    tm = TOKEN_TILE
    tiles_per_seq = seq // tm
    return pl.pallas_call(
        functools.partial(_ffn_kernel, mod_row=mod_row),
        grid=(tokens // tm,),
        in_specs=[pl.BlockSpec((tm, D_MODEL), lambda i: (i, 0)),
                  pl.BlockSpec((1, N_MOD, D_MODEL), lambda i: (i // tiles_per_seq, 0, 0)),
                  _resident((1, D_MODEL), lambda i: (0, 0)),
                  _resident((D_MODEL, D_FF), lambda i: (0, 0)),
                  _resident((D_MODEL, D_FF), lambda i: (0, 1)),
                  _resident((D_FF, D_MODEL), lambda i: (0, 0))],
        out_specs=pl.BlockSpec((tm, D_MODEL), lambda i: (i, 0)),
        out_shape=jax.ShapeDtypeStruct((tokens, D_MODEL), F32),
        scratch_shapes=[pltpu.VMEM((tm, D_MODEL), BF16), pltpu.VMEM((tm, D_FF), BF16)],
        compiler_params=_params("parallel"),
        name=f"ffn_row{mod_row}",
    )(x2, mod3, gain.reshape(1, D_MODEL), w_in, w_in, w_out)


_PROJ_Q, _PROJ_K, _PROJ_V = 0, SB_W, 2 * SB_W
_PROJ_DN = 3 * SB_W
_PROJ_Z = _PROJ_DN + 3 * DN_W
_PROJ_MAIN_W = _PROJ_Z + DN_W


def _proj_kernel(h_ref, mod_ref, g_ref, wm_ref, wab_ref, wabt_ref, gq_ref, gk_ref, grp_ref, wc_ref,
                 q_ref, k_ref, v_ref, dn_ref, z_ref, ab_ref, abt_ref, u_ref, xbuf_ref, *, tiles_per_seq):
    tm = TOKEN_TILE
    hist = SUBLANES
    u_ref[...] = _rms_mod(h_ref[...], g_ref[...], mod_ref[0, 3:4, :], mod_ref[0, 4:5, :]).astype(BF16)

    @pl.when(pl.program_id(0) % tiles_per_seq == 0)
    def _():
        xbuf_ref[0:hist, :] = jnp.zeros((hist, 3 * DN_W), F32)

    def head_norm(t, gain_row):
        hi, lo = _split_bf16(t * t, 2)
        ss = _dot(hi, grp_ref[...]) + _dot(lo, grp_ref[...])
        return t * lax.rsqrt(ss * (1.0 / SB_HEAD_DIM) + EPS) * gain_row

    def conv_group(g):
        cols = slice(g * DN_DIM, (g + 1) * DN_DIM)
        acc = xbuf_ref[hist:hist + tm, cols] * wc_ref[DN_CONV - 1:DN_CONV, cols]
        for tap in range(DN_CONV - 1):
            lo = hist - (DN_CONV - 1) + tap
            acc = acc + xbuf_ref[lo:lo + tm, cols] * wc_ref[tap:tap + 1, cols]
        y = acc * _sigmoid(acc)
        if g < 2 * DN_HEADS:
            y = y * lax.rsqrt(jnp.sum(y * y, axis=-1, keepdims=True) + EPS)
        if g < DN_HEADS:
            y = y * (DN_DIM ** -0.5)
        dn_ref[:, cols] = y

    for c in range(3 * DN_W // MXU_DIM):
        cols = slice(c * MXU_DIM, (c + 1) * MXU_DIM)
        xbuf_ref[hist:hist + tm, cols] = _dot(
            u_ref[...], wm_ref[:, _PROJ_DN + c * MXU_DIM:_PROJ_DN + (c + 1) * MXU_DIM])
    for g in range(0, 3):
        conv_group(g)
    q = _dot(u_ref[...], wm_ref[:, _PROJ_Q:_PROJ_Q + SB_W])
    q_ref[...] = (head_norm(q, gq_ref[...]) * (SB_HEAD_DIM ** -0.5)).astype(BF16)
    for g in range(3, 6):
        conv_group(g)
    k = _dot(u_ref[...], wm_ref[:, _PROJ_K:_PROJ_K + SB_W])
    k_ref[...] = head_norm(k, gk_ref[...]).astype(BF16)
    for g in range(6, 9):
        conv_group(g)
    v_ref[...] = _dot(u_ref[...], wm_ref[:, _PROJ_V:_PROJ_V + SB_W]).astype(BF16)
    for g in range(9, 12):
        conv_group(g)
    xbuf_ref[0:hist, :] = xbuf_ref[tm:tm + hist, :]
    z_ref[...] = _dot(u_ref[...], wm_ref[:, _PROJ_Z:_PROJ_Z + DN_W])
    ab_ref[...] = _dot(u_ref[...], wab_ref[...])
    abt_ref[0] = _dot_nt(wabt_ref[...], u_ref[...])


def _projection(h2, mod3, g_mix, w_main, w_ab, w_abt, gq_row, gk_row, w_conv, seq):
    tokens = h2.shape[0]
    bsz = tokens // seq
    tm = TOKEN_TILE
    tiles_per_seq = seq // tm
    grp = jnp.kron(jnp.eye(SB_HEADS, dtype=F32), jnp.ones((SB_HEAD_DIM, SB_HEAD_DIM), F32)).astype(BF16)
    row = lambda w: pl.BlockSpec((tm, w), lambda i: (i, 0))
    return pl.pallas_call(
        functools.partial(_proj_kernel, tiles_per_seq=tiles_per_seq),
        grid=(tokens // tm,),
        in_specs=[row(D_MODEL),
                  pl.BlockSpec((1, N_MOD, D_MODEL), lambda i: (i // tiles_per_seq, 0, 0)),
                  _resident((1, D_MODEL), lambda i: (0, 0)),
                  _resident((D_MODEL, _PROJ_MAIN_W), lambda i: (0, 0)),
                  _resident((D_MODEL, LANES), lambda i: (0, 0)),
                  _resident((2 * SUBLANES, D_MODEL), lambda i: (0, 0)),
                  _resident((1, SB_W), lambda i: (0, 0)),
                  _resident((1, SB_W), lambda i: (0, 0)),
                  _resident((SB_W, SB_W), lambda i: (0, 0)),
                  _resident((DN_CONV, 3 * DN_W), lambda i: (0, 0))],
        out_specs=[row(SB_W), row(SB_W), row(SB_W), row(3 * DN_W), row(DN_W), row(LANES),
                   pl.BlockSpec((1, 2 * SUBLANES, tm), lambda i: (i // tiles_per_seq, 0, i % tiles_per_seq))],
        out_shape=[jax.ShapeDtypeStruct((tokens, SB_W), BF16),
                   jax.ShapeDtypeStruct((tokens, SB_W), BF16),
                   jax.ShapeDtypeStruct((tokens, SB_W), BF16),
                   jax.ShapeDtypeStruct((tokens, 3 * DN_W), F32),
                   jax.ShapeDtypeStruct((tokens, DN_W), F32),
                   jax.ShapeDtypeStruct((tokens, LANES), F32),
                   jax.ShapeDtypeStruct((bsz, 2 * SUBLANES, seq), F32)],
        scratch_shapes=[pltpu.VMEM((tm, D_MODEL), BF16), pltpu.VMEM((tm + SUBLANES, 3 * DN_W), F32)],
        compiler_params=_params("arbitrary"),
        name="mixer_in_proj",
    )(h2, mod3, g_mix.reshape(1, D_MODEL), w_main, w_ab, w_abt, gq_row, gk_row, grp, w_conv)


def _sb_kernel(q_ref, k_ref, v_ref, o_ref, acc_ref, carry_ref, lb_ref, lk_ref):
    t = SB_TILE
    i = pl.program_id(2)
    q = q_ref[0]
    zero_q = jnp.zeros_like(q)
    head_a = lax.broadcasted_iota(jnp.int32, (t, LANES), 1) < SB_HEAD_DIM
    q2 = jnp.concatenate([jnp.where(head_a, q, zero_q), jnp.where(head_a, zero_q, q)], axis=0)
    suffix = jnp.where(lax.broadcasted_iota(jnp.int32, (t, t), 0) > lax.broadcasted_iota(jnp.int32, (t, t), 1),
                       1.0, 0.0).astype(BF16)
    causal = (lax.broadcasted_iota(jnp.int32, (2 * t, t), 1)
              < (lax.broadcasted_iota(jnp.int32, (2 * t, t), 0) & (t - 1)))

    def logits(kb, slot, diagonal=False):
        k = k_ref[0, pl.ds(pl.multiple_of(kb * t, t), t), :]
        z = _dot_nt(q2, k)
        log_beta = jnp.minimum(z, 0.0) - jnp.log(1.0 + jnp.exp(-jnp.abs(z)))
        log_keep = log_beta - z
        if diagonal:
            log_beta = jnp.where(causal, log_beta, MASKED)
            log_keep = jnp.where(causal, log_keep, 0.0)
        lb_ref[slot] = log_beta
        lk_ref[slot] = log_keep.astype(BF16)

    def weigh(kb, slot):
        v = v_ref[0, pl.ds(pl.multiple_of(kb * t, t), t), :]
        log_keep = lk_ref[slot]
        sums = _dot(log_keep, suffix)
        carry = carry_ref[...]
        w = jnp.exp(lb_ref[slot] + sums + carry)
        carry_ref[...] = carry + (sums[:, 0:1] + log_keep[:, 0:1].astype(F32))
        zero_v = jnp.zeros_like(v)
        v2 = jnp.concatenate([jnp.where(head_a, v, zero_v), jnp.where(head_a, zero_v, v)], axis=0)
        w2 = jnp.concatenate([w[:t], w[t:]], axis=1).astype(BF16)
        acc_ref[...] += _dot(w2, v2)

    acc_ref[...] = jnp.zeros_like(acc_ref)
    carry_ref[...] = jnp.zeros_like(carry_ref)
    logits(i, 0, diagonal=True)

    def pair(state):
        m, _ = state
        kb = i - 2 * m
        weigh(kb, 0)
        logits(kb - 1, 1)
        weigh(kb - 1, 1)
        logits(kb - 2, 0)
        return m + 1, jnp.max(carry_ref[...]) > SB_DEAD

    _, alive = lax.while_loop(lambda s: jnp.logical_and(s[0] < i // 2, s[1]), pair, (0, True))

    @pl.when(jnp.logical_and(alive, i % 2 == 1))
    def _():
        weigh(1, 0)
        logits(0, 1)
        weigh(0, 1)

    @pl.when(jnp.logical_and(alive, i % 2 == 0))
    def _():
        weigh(0, 0)

    o_ref[0] = acc_ref[...].astype(o_ref.dtype)


def _sb_attention(q, k, v):
    bsz, seq, _ = q.shape
    tq = SB_TILE
    pairs = SB_W // LANES
    return pl.pallas_call(
        _sb_kernel,
        grid=(bsz, pairs, seq // tq),
        in_specs=[pl.BlockSpec((1, tq, LANES), lambda b, p, i: (b, i, p)),
                  pl.BlockSpec((1, seq, LANES), lambda b, p, i: (b, 0, p)),
                  pl.BlockSpec((1, seq, LANES), lambda b, p, i: (b, 0, p))],
        out_specs=pl.BlockSpec((1, tq, LANES), lambda b, p, i: (b, i, p)),
        out_shape=jax.ShapeDtypeStruct((bsz, seq, SB_W), BF16),
        scratch_shapes=[pltpu.VMEM((tq, LANES), F32), pltpu.VMEM((2 * tq, 1), F32),
                        pltpu.VMEM((2, 2 * tq, tq), F32), pltpu.VMEM((2, 2 * tq, tq), BF16)],
        compiler_params=_params("parallel", "parallel", "arbitrary"),
        name="stickbreak_attn",
    )(q, k, v)


def _gdn_kernel(qkv_ref, z_ref, ab_ref, abt_ref, prow_ref, pcol_ref, gout_ref, o_ref,
                state_ref, lhs_ref, add_ref, gam_ref):
    ts = DN_STEP
    chunks = ts // DN_CHUNK
    j = pl.program_id(1)
    fill = j % 2

    @pl.when(j == 0)
    def _():
        state_ref[...] = jnp.zeros_like(state_ref)
        lhs_ref[1] = jnp.zeros(lhs_ref.shape[1:], lhs_ref.dtype)
        add_ref[1] = jnp.zeros(add_ref.shape[1:], add_ref.dtype)
        gam_ref[1] = jnp.zeros(gam_ref.shape[1:], gam_ref.dtype)

    _gdn_scan(j, z_ref, gout_ref, o_ref, state_ref, lhs_ref, add_ref, gam_ref)

    tb = 2 * DN_CHUNK
    ri = lax.broadcasted_iota(jnp.int32, (tb, tb), 0)
    ci = lax.broadcasted_iota(jnp.int32, (tb, tb), 1)
    same_chunk = (ri >> 6) == (ci >> 6)
    m_incl = same_chunk & (ci <= ri)
    m_strict = same_chunk & (ci < ri)
    eye = jnp.where(ri == ci, 1.0, 0.0)
    tri = jnp.where(m_incl, 1.0, 0.0).astype(BF16)
    tri_t = jnp.where(same_chunk & (ri <= ci), 1.0, 0.0).astype(BF16)
    units = [(h, r) for h in range(DN_HEADS) for r in range(ts // tb)]
    rows_of = lambda r: slice(r * tb, (r + 1) * tb)

    ab = ab_ref[...]
    beta_all = _sigmoid(ab)
    g_all = -jnp.exp(prow_ref[0:1, :]) * _softplus(ab + prow_ref[1:2, :])
    abt = abt_ref[0]
    g_rows = -jnp.exp(pcol_ref[:, 0:1]) * _softplus(abt + pcol_ref[:, 1:2])
    g_all3 = _split_bf16(g_all, 3)
    g_rows3 = _split_bf16(g_rows, 3)
    gcum_all = [sum(_dot(tri, p[rows_of(r)]) for p in g_all3) for r in range(ts // tb)]
    gcum_rows = [sum(_dot(p[:, rows_of(r)], tri_t) for p in g_rows3) for r in range(ts // tb)]

    def col(h, part):
        return slice(part * DN_W + h * DN_DIM, part * DN_W + (h + 1) * DN_DIM)

    q = {u: qkv_ref[0, rows_of(u[1]), col(u[0], 0)] for u in units}
    k = {u: qkv_ref[0, rows_of(u[1]), col(u[0], 1)] for u in units}
    v = {u: qkv_ref[0, rows_of(u[1]), col(u[0], 2)] for u in units}
    beta = {(h, r): beta_all[rows_of(r), h:h + 1] for h, r in units}
    gc = {(h, r): gcum_all[r][:, DN_HEADS + h:DN_HEADS + h + 1] for h, r in units}
    gr = {(h, r): gcum_rows[r][DN_HEADS + h:DN_HEADS + h + 1, :] for h, r in units}
    decay = {u: jnp.where(m_incl, jnp.exp(jnp.where(m_incl, gc[u] - gr[u], 0.0)), 0.0) for u in units}
    kb = {u: k[u] * beta[u] for u in units}
    vb = {u: v[u] * beta[u] for u in units}
    eg = {u: jnp.exp(gc[u]) for u in units}
    qg = {u: q[u] * eg[u] for u in units}
    kq = {u: _dot_nt(jnp.concatenate([kb[u], q[u]], axis=0).astype(BF16), k[u].astype(BF16)) for u in units}
    lower = {u: jnp.where(m_strict, kq[u][:tb] * decay[u], 0.0) for u in units}
    a_qk = {u: (kq[u][tb:] * decay[u]).astype(BF16) for u in units}

    blk8 = (ri >> 3) == (ci >> 3)
    m8 = {u: jnp.where(blk8, -lower[u], 0.0).astype(BF16) for u in units}
    inv = {u: eye + m8[u].astype(F32) for u in units}
    p = {u: _dot(m8[u], m8[u]).astype(BF16) for u in units}
    inv = {u: inv[u] + _dot(inv[u].astype(BF16), p[u]) for u in units}
    p = {u: _dot(p[u], p[u]).astype(BF16) for u in units}
    inv = {u: inv[u] + _dot(inv[u].astype(BF16), p[u]) for u in units}
    for lg in (3, 4, 5):
        off = ((ri >> (lg + 1)) == (ci >> (lg + 1))) & (((ri >> lg) & 1) == 1) & (((ci >> lg) & 1) == 0)
        inv16 = {u: inv[u].astype(BF16) for u in units}
        cb = {u: _dot(jnp.where(off, lower[u], 0.0).astype(BF16), inv16[u]).astype(BF16) for u in units}
        inv = {u: inv[u] - _dot(inv16[u], cb[u]) for u in units}
    sol = {u: _dot(inv[u].astype(BF16), jnp.concatenate([vb[u], kb[u] * eg[u]], axis=1).astype(BF16))
           for u in units}

    a_sol = {u: _dot(a_qk[u], sol[u].astype(BF16)) for u in units}
    for u in units:
        h, r = u
        q_eff = (qg[u] - a_sol[u][:, DN_DIM:]).astype(BF16)
        for cc in range(tb // DN_CHUNK):
            c = r * (tb // DN_CHUNK) + cc
            rows = slice(cc * DN_CHUNK, (cc + 1) * DN_CHUNK)
            g_last = gc[u][(cc + 1) * DN_CHUNK - 1:(cc + 1) * DN_CHUNK, :]
            k_dec = k[u][rows] * jnp.exp(g_last - gc[u][rows])
            kd_sol = _dot_tn(k_dec.astype(BF16), sol[u][rows].astype(BF16))
            lhs_ref[fill, h, c, 0:DN_CHUNK, :] = q_eff[rows]
            lhs_ref[fill, h, c, DN_CHUNK:, :] = kd_sol[:, DN_DIM:].astype(BF16)
            add_ref[fill, h, c, 0:DN_CHUNK, :] = a_sol[u][rows, :DN_DIM]
            add_ref[fill, h, c, DN_CHUNK:, :] = kd_sol[:, :DN_DIM]
            gam_ref[fill, h * chunks + c:h * chunks + c + 1, :] = jnp.broadcast_to(jnp.exp(g_last), (1, LANES))


def _gdn_scan(j, z_ref, gout_ref, o_ref, state_ref, lhs_ref, add_ref, gam_ref):
    chunks = DN_STEP // DN_CHUNK
    drain = (j + 1) % 2
    out = [[] for _ in range(DN_HEADS)]
    for c in range(chunks):
        for h in range(DN_HEADS):
            state = state_ref[h]
            res = _dot(lhs_ref[drain, h, c], state.astype(BF16))
            add = add_ref[drain, h, c]
            out[h].append(res[:DN_CHUNK] + add[:DN_CHUNK])
            gamma = gam_ref[drain, h * chunks + c:h * chunks + c + 1, :]
            state_ref[h] = state * gamma - res[DN_CHUNK:] + add[DN_CHUNK:]
    for h in range(DN_HEADS):
        o = jnp.concatenate(out[h], axis=0)
        o = o * lax.rsqrt(jnp.mean(o * o, axis=-1, keepdims=True) + EPS) * gout_ref[...]
        zh = z_ref[0, :, h * DN_DIM:(h + 1) * DN_DIM]
        o_ref[0, :, h * DN_DIM:(h + 1) * DN_DIM] = (o * (zh * _sigmoid(zh))).astype(o_ref.dtype)


def _gated_deltanet(qkv, z, ab, abt, prow, pcol, g_out):
    bsz, seq, _ = qkv.shape
    ts = DN_STEP
    steps = seq // ts
    chunks = ts // DN_CHUNK
    fill_step = lambda j: jnp.minimum(j, steps - 1)
    drain_step = lambda j: jnp.maximum(j - 1, 0)
    return pl.pallas_call(
        _gdn_kernel,
        grid=(bsz, steps + 1),
        in_specs=[pl.BlockSpec((1, ts, 3 * DN_W), lambda b, j: (b, fill_step(j), 0)),
                  pl.BlockSpec((1, ts, DN_W), lambda b, j: (b, drain_step(j), 0)),
                  pl.BlockSpec((ts, LANES), lambda b, j: (b * steps + fill_step(j), 0)),
                  pl.BlockSpec((1, 2 * SUBLANES, ts), lambda b, j: (b, 0, fill_step(j))),
                  pl.BlockSpec((2, LANES), lambda b, j: (0, 0)),
                  pl.BlockSpec((2 * SUBLANES, LANES), lambda b, j: (0, 0)),
                  pl.BlockSpec((1, DN_DIM), lambda b, j: (0, 0))],
        out_specs=pl.BlockSpec((1, ts, DN_W), lambda b, j: (b, drain_step(j), 0)),
        out_shape=jax.ShapeDtypeStruct((bsz, seq, DN_W), BF16),
        scratch_shapes=[pltpu.VMEM((DN_HEADS, DN_DIM, DN_DIM), F32),
                        pltpu.VMEM((2, DN_HEADS, chunks, DN_CHUNK + DN_DIM, DN_DIM), BF16),
                        pltpu.VMEM((2, DN_HEADS, chunks, DN_CHUNK + DN_DIM, DN_DIM), F32),
                        pltpu.VMEM((2, DN_HEADS * chunks, LANES), F32)],
        compiler_params=_params("parallel", "arbitrary"),
        name="gated_deltanet",
    )(qkv, z, ab, abt, prow, pcol, g_out.reshape(1, DN_DIM))


def _merge_kernel(h_ref, attn_ref, dn_ref, mod_ref, g_ref, wr_ref, wus_ref, wud_ref, wo_ref, out_ref,
                  u_ref, m_ref):
    u_ref[...] = _rms_mod(h_ref[...], g_ref[...], mod_ref[0, 3:4, :], mod_ref[0, 4:5, :]).astype(BF16)
    for n in range(D_MODEL // MXU_DIM):
        cols = slice(n * MXU_DIM, (n + 1) * MXU_DIM)
        r_sb = _dot(u_ref[...], wr_ref[:, cols])
        r_dn = _dot(u_ref[...], wr_ref[:, D_MODEL + n * MXU_DIM:D_MODEL + (n + 1) * MXU_DIM])
        y_sb = _dot(attn_ref[...], wus_ref[:, cols])
        y_dn = _dot(dn_ref[...], wud_ref[:, cols])
        m_ref[:, cols] = (_sigmoid(r_sb) * y_sb + _sigmoid(r_dn) * y_dn).astype(BF16)
    for n in range(D_MODEL // MXU_DIM):
        cols = slice(n * MXU_DIM, (n + 1) * MXU_DIM)
        out_ref[:, cols] = h_ref[:, cols] + mod_ref[0, 5:6, cols] * _dot(m_ref[...], wo_ref[:, cols])


def _merge(h2, attn, dn, mod3, g_mix, w_r, w_up_sb, w_up_dn, w_out, seq):
    tokens = h2.shape[0]
    tm = TOKEN_TILE
    tiles_per_seq = seq // tm
    row = lambda w: pl.BlockSpec((tm, w), lambda i: (i, 0))
    return pl.pallas_call(
        _merge_kernel,
        grid=(tokens // tm,),
        in_specs=[row(D_MODEL), row(SB_W), row(DN_W),
                  pl.BlockSpec((1, N_MOD, D_MODEL), lambda i: (i // tiles_per_seq, 0, 0)),
                  _resident((1, D_MODEL), lambda i: (0, 0)),
                  _resident((D_MODEL, 2 * D_MODEL), lambda i: (0, 0)),
                  _resident((SB_W, D_MODEL), lambda i: (0, 0)),
                  _resident((DN_W, D_MODEL), lambda i: (0, 0)),
                  _resident((D_MODEL, D_MODEL), lambda i: (0, 0))],
        out_specs=row(D_MODEL),
        out_shape=jax.ShapeDtypeStruct((tokens, D_MODEL), F32),
        scratch_shapes=[pltpu.VMEM((tm, D_MODEL), BF16), pltpu.VMEM((tm, D_MODEL), BF16)],
        compiler_params=_params("parallel"),
        name="mixer_merge",
    )(h2, attn, dn, mod3, g_mix.reshape(1, D_MODEL), w_r, w_up_sb, w_up_dn, w_out)


def _layer(h2, mod3, seq, g_ffn1, w_ffn1_in, w_ffn1_out, g_mix, w_in, g_q_sb, g_k_sb, w_conv, a_log,
           dt_bias, g_dn_out, w_up_sb, w_up_dn, w_out, g_ffn2, w_ffn2_in, w_ffn2_out):
    tokens = h2.shape[0]
    bsz = tokens // seq
    h2 = _ffn(h2, mod3, g_ffn1, w_ffn1_in.astype(BF16), w_ffn1_out.astype(BF16), 0, seq)

    gate0 = _PROJ_MAIN_W
    gate1 = gate0 + 2 * DN_HEADS
    w_main = w_in[:, :gate0].astype(BF16)
    w_gate = w_in[:, gate0:gate1]
    w_ab = jnp.pad(w_gate, ((0, 0), (0, LANES - 2 * DN_HEADS))).astype(BF16)
    w_abt = jnp.pad(w_gate.T, ((0, 2 * SUBLANES - 2 * DN_HEADS), (0, 0))).astype(BF16)
    w_r = w_in[:, gate1:].astype(BF16)
    gq_row = jnp.tile(g_q_sb, SB_HEADS).reshape(1, SB_W)
    gk_row = jnp.tile(g_k_sb, SB_HEADS).reshape(1, SB_W)
    q, k, v, qkv_dn, z_dn, ab, abt = _projection(h2, mod3, g_mix, w_main, w_ab, w_abt, gq_row, gk_row,
                                                 w_conv, seq)

    attn = _sb_attention(q.reshape(bsz, seq, SB_W), k.reshape(bsz, seq, SB_W), v.reshape(bsz, seq, SB_W))

    lanes = slice(DN_HEADS, 2 * DN_HEADS)
    prow = jnp.zeros((2, LANES), F32).at[0, lanes].set(a_log).at[1, lanes].set(dt_bias)
    pcol = jnp.zeros((2 * SUBLANES, LANES), F32).at[lanes, 0].set(a_log).at[lanes, 1].set(dt_bias)
    dn = _gated_deltanet(qkv_dn.reshape(bsz, seq, 3 * DN_W), z_dn.reshape(bsz, seq, DN_W), ab, abt,
                         prow, pcol, g_dn_out)

    h2 = _merge(h2, attn.reshape(tokens, SB_W), dn.reshape(tokens, DN_W), mod3, g_mix, w_r,
                w_up_sb.astype(BF16), w_up_dn.astype(BF16), w_out.astype(BF16), seq)
    return _ffn(h2, mod3, g_ffn2, w_ffn2_in.astype(BF16), w_ffn2_out.astype(BF16), 6, seq)


def kernel(x, c, w_ada, b_ada, g_ffn1, w_ffn1_in, w_ffn1_out, g_mix, w_in, g_q_sb, g_k_sb, w_conv, a_log,
           dt_bias, g_dn_out, w_up_sb, w_up_dn, w_out, g_ffn2, w_ffn2_in, w_ffn2_out):
    bsz, seq, d = x.shape
    depth = w_ada.shape[0]
    h2 = x.reshape(bsz * seq, d)
    for l in range(depth):
        mod3 = _modulation(c, w_ada[l], b_ada[l]).reshape(bsz, N_MOD, d)
        h2 = _layer(h2, mod3, seq, g_ffn1[l], w_ffn1_in[l], w_ffn1_out[l], g_mix[l], w_in[l], g_q_sb[l],
                    g_k_sb[l], w_conv[l], a_log[l], dt_bias[l], g_dn_out[l], w_up_sb[l], w_up_dn[l],
                    w_out[l], g_ffn2[l], w_ffn2_in[l], w_ffn2_out[l])
    return h2.reshape(bsz, seq, d).astype(x.dtype)
```

```python
import functools

import jax
import jax.numpy as jnp
from jax import lax
from jax.experimental import pallas as pl
from jax.experimental.pallas import tpu as pltpu

F32 = jnp.float32
BF16 = jnp.bfloat16

D_MODEL = 1024
D_FF = 2816
N_MOD = 9
EPS = 1e-6

SB_HEADS = 8
SB_HEAD_DIM = 64
SB_W = SB_HEADS * SB_HEAD_DIM
DN_HEADS = 4
DN_DIM = 128
DN_W = DN_HEADS * DN_DIM
DN_CONV = 4
DN_CHUNK = 64

LANES = 128
SUBLANES = 8
MXU_DIM = 256
VMEM_LIMIT = 56 * 1024 * 1024

TOKEN_TILE = 512
FF_CHUNK = MXU_DIM
SB_TILE = MXU_DIM
MASKED = -1e30
SB_DEAD = -104.0
DN_STEP = 4 * DN_CHUNK

_dot = functools.partial(jnp.dot, preferred_element_type=F32)


def _dot_nt(a, b):
    return lax.dot_general(a, b, (((1,), (1,)), ((), ())), preferred_element_type=F32)


def _dot_tn(a, b):
    return lax.dot_general(a, b, (((0,), (0,)), ((), ())), preferred_element_type=F32)


def _sigmoid(x):
    return 1.0 / (1.0 + jnp.exp(-x))


def _softplus(x):
    return jnp.maximum(x, 0.0) + jnp.log(1.0 + jnp.exp(-jnp.abs(x)))


def _split_bf16(x, pieces):
    out = []
    for _ in range(pieces - 1):
        p = x.astype(BF16)
        out.append(p)
        x = x - p.astype(F32)
    out.append(x.astype(BF16))
    return out


def _rms_mod(x, gain, shift, scale):
    y = x * lax.rsqrt(jnp.mean(x * x, axis=-1, keepdims=True) + EPS)
    return (y * gain) * (1.0 + scale) + shift


def _resident(shape, index_map):
    return pl.BlockSpec(shape, index_map, pipeline_mode=pl.Buffered(1))


def _params(*semantics):
    return pltpu.CompilerParams(dimension_semantics=semantics, vmem_limit_bytes=VMEM_LIMIT)


def _mod_kernel(c_ref, w_ref, b_ref, o_ref):
    c = c_ref[...]
    cond = (c * _sigmoid(c)).astype(BF16)
    o_ref[...] = _dot(cond, w_ref[...].astype(BF16)) + b_ref[...]


def _modulation(c, w_ada, b_ada):
    bsz = c.shape[0]
    n = w_ada.shape[1]
    tn = D_MODEL
    return pl.pallas_call(
        _mod_kernel,
        grid=(n // tn,),
        in_specs=[pl.BlockSpec((bsz, D_MODEL), lambda j: (0, 0)),
                  pl.BlockSpec((D_MODEL, tn), lambda j: (0, j)),
                  pl.BlockSpec((1, tn), lambda j: (0, j))],
        out_specs=pl.BlockSpec((bsz, tn), lambda j: (0, j)),
        out_shape=jax.ShapeDtypeStruct((bsz, n), F32),
        compiler_params=_params("arbitrary"),
        name="adaln_mod",
    )(c, w_ada, b_ada.reshape(1, n))


def _ffn_kernel(x_ref, mod_ref, g_ref, wa_ref, wb_ref, wo_ref, out_ref, u_ref, h_ref, *, mod_row):
    shift = mod_ref[0, mod_row:mod_row + 1, :]
    scale = mod_ref[0, mod_row + 1:mod_row + 2, :]
    u_ref[...] = _rms_mod(x_ref[...], g_ref[...], shift, scale).astype(BF16)
    for c in range(D_FF // FF_CHUNK):
        cols = slice(c * FF_CHUNK, (c + 1) * FF_CHUNK)
        a = _dot(u_ref[...], wa_ref[:, cols])
        b = _dot(u_ref[...], wb_ref[:, cols])
        h_ref[:, cols] = (a * _sigmoid(a) * b).astype(BF16)
    for n in range(D_MODEL // MXU_DIM):
        cols = slice(n * MXU_DIM, (n + 1) * MXU_DIM)
        gate = mod_ref[0, mod_row + 2:mod_row + 3, cols]
        out_ref[:, cols] = x_ref[:, cols] + (0.5 * gate) * _dot(h_ref[...], wo_ref[:, cols])


def _ffn(x2, mod3, gain, w_in, w_out, mod_row, seq):
    tokens = x2.shape[0]
    tm = TOKEN_TILE
    tiles_per_seq = seq // tm
    return pl.pallas_call(
        functools.partial(_ffn_kernel, mod_row=mod_row),
        grid=(tokens // tm,),
        in_specs=[pl.BlockSpec((tm, D_MODEL), lambda i: (i, 0)),
                  pl.BlockSpec((1, N_MOD, D_MODEL), lambda i: (i // tiles_per_seq, 0, 0)),
                  _resident((1, D_MODEL), lambda i: (0, 0)),
                  _resident((D_MODEL, D_FF), lambda i: (0, 0)),
                  _resident((D_MODEL, D_FF), lambda i: (0, 1)),
                  _resident((D_FF, D_MODEL), lambda i: (0, 0))],
        out_specs=pl.BlockSpec((tm, D_MODEL), lambda i: (i, 0)),
        out_shape=jax.ShapeDtypeStruct((tokens, D_MODEL), F32),
        scratch_shapes=[pltpu.VMEM((tm, D_MODEL), BF16), pltpu.VMEM((tm, D_FF), BF16)],
        compiler_params=_params("parallel"),
        name=f"ffn_row{mod_row}",
    )(x2, mod3, gain.reshape(1, D_MODEL), w_in, w_in, w_out)


_PROJ_Q, _PROJ_K, _PROJ_V = 0, SB_W, 2 * SB_W
_PROJ_DN = 3 * SB_W
_PROJ_Z = _PROJ_DN + 3 * DN_W
_PROJ_MAIN_W = _PROJ_Z + DN_W


def _proj_kernel(h_ref, mod_ref, g_ref, wm_ref, wab_ref, wabt_ref, gq_ref, gk_ref, grp_ref, wc_ref,
                 q_ref, k_ref, v_ref, dn_ref, z_ref, ab_ref, abt_ref, u_ref, xbuf_ref, *, tiles_per_seq):
    tm = TOKEN_TILE
    hist = SUBLANES
    u_ref[...] = _rms_mod(h_ref[...], g_ref[...], mod_ref[0, 3:4, :], mod_ref[0, 4:5, :]).astype(BF16)

    @pl.when(pl.program_id(0) % tiles_per_seq == 0)
    def _():
        xbuf_ref[0:hist, :] = jnp.zeros((hist, 3 * DN_W), F32)

    def head_norm(t, gain_row):
        hi, lo = _split_bf16(t * t, 2)
        ss = _dot(hi, grp_ref[...]) + _dot(lo, grp_ref[...])
        return t * lax.rsqrt(ss * (1.0 / SB_HEAD_DIM) + EPS) * gain_row

    def conv_group(g):
        cols = slice(g * DN_DIM, (g + 1) * DN_DIM)
        acc = xbuf_ref[hist:hist + tm, cols] * wc_ref[DN_CONV - 1:DN_CONV, cols]
        for tap in range(DN_CONV - 1):
            lo = hist - (DN_CONV - 1) + tap
            acc = acc + xbuf_ref[lo:lo + tm, cols] * wc_ref[tap:tap + 1, cols]
        y = acc * _sigmoid(acc)
        if g < 2 * DN_HEADS:
            y = y * lax.rsqrt(jnp.sum(y * y, axis=-1, keepdims=True) + EPS)
        if g < DN_HEADS:
            y = y * (DN_DIM ** -0.5)
        dn_ref[:, cols] = y

    q = _dot(u_ref[...], wm_ref[:, _PROJ_Q:_PROJ_Q + SB_W])
    q_ref[...] = (head_norm(q, gq_ref[...]) * (SB_HEAD_DIM ** -0.5)).astype(BF16)
    k = _dot(u_ref[...], wm_ref[:, _PROJ_K:_PROJ_K + SB_W])
    k_ref[...] = head_norm(k, gk_ref[...]).astype(BF16)
    v_ref[...] = _dot(u_ref[...], wm_ref[:, _PROJ_V:_PROJ_V + SB_W]).astype(BF16)
    for c in range(3 * DN_W // MXU_DIM):
        cols = slice(c * MXU_DIM, (c + 1) * MXU_DIM)
        xbuf_ref[hist:hist + tm, cols] = _dot(
            u_ref[...], wm_ref[:, _PROJ_DN + c * MXU_DIM:_PROJ_DN + (c + 1) * MXU_DIM])
    for g in range(3 * DN_HEADS):
        conv_group(g)
    xbuf_ref[0:hist, :] = xbuf_ref[tm:tm + hist, :]
    z_ref[...] = _dot(u_ref[...], wm_ref[:, _PROJ_Z:_PROJ_Z + DN_W])
    ab_ref[...] = _dot(u_ref[...], wab_ref[...])
    abt_ref[0] = _dot_nt(wabt_ref[...], u_ref[...])


def _projection(h2, mod3, g_mix, w_main, w_ab, w_abt, gq_row, gk_row, w_conv, seq):
    tokens = h2.shape[0]
    bsz = tokens // seq
    tm = TOKEN_TILE
    tiles_per_seq = seq // tm
    grp = jnp.kron(jnp.eye(SB_HEADS, dtype=F32), jnp.ones((SB_HEAD_DIM, SB_HEAD_DIM), F32)).astype(BF16)
    row = lambda w: pl.BlockSpec((tm, w), lambda i: (i, 0))
    return pl.pallas_call(
        functools.partial(_proj_kernel, tiles_per_seq=tiles_per_seq),
        grid=(tokens // tm,),
        in_specs=[row(D_MODEL),
                  pl.BlockSpec((1, N_MOD, D_MODEL), lambda i: (i // tiles_per_seq, 0, 0)),
                  _resident((1, D_MODEL), lambda i: (0, 0)),
                  _resident((D_MODEL, _PROJ_MAIN_W), lambda i: (0, 0)),
                  _resident((D_MODEL, LANES), lambda i: (0, 0)),
                  _resident((2 * SUBLANES, D_MODEL), lambda i: (0, 0)),
                  _resident((1, SB_W), lambda i: (0, 0)),
                  _resident((1, SB_W), lambda i: (0, 0)),
                  _resident((SB_W, SB_W), lambda i: (0, 0)),
                  _resident((DN_CONV, 3 * DN_W), lambda i: (0, 0))],
        out_specs=[row(SB_W), row(SB_W), row(SB_W), row(3 * DN_W), row(DN_W), row(LANES),
                   pl.BlockSpec((1, 2 * SUBLANES, tm), lambda i: (i // tiles_per_seq, 0, i % tiles_per_seq))],
        out_shape=[jax.ShapeDtypeStruct((tokens, SB_W), BF16),
                   jax.ShapeDtypeStruct((tokens, SB_W), BF16),
                   jax.ShapeDtypeStruct((tokens, SB_W), BF16),
                   jax.ShapeDtypeStruct((tokens, 3 * DN_W), F32),
                   jax.ShapeDtypeStruct((tokens, DN_W), F32),
                   jax.ShapeDtypeStruct((tokens, LANES), F32),
                   jax.ShapeDtypeStruct((bsz, 2 * SUBLANES, seq), F32)],
        scratch_shapes=[pltpu.VMEM((tm, D_MODEL), BF16), pltpu.VMEM((tm + SUBLANES, 3 * DN_W), F32)],
        compiler_params=_params("arbitrary"),
        name="mixer_in_proj",
    )(h2, mod3, g_mix.reshape(1, D_MODEL), w_main, w_ab, w_abt, gq_row, gk_row, grp, w_conv)


def _sb_kernel(q_ref, k_ref, v_ref, o_ref, acc_ref, carry_ref, lb_ref, lk_ref, tot_ref):
    t = SB_TILE
    i = pl.program_id(2)
    q = q_ref[0]
    zero_q = jnp.zeros_like(q)
    head_a = lax.broadcasted_iota(jnp.int32, (t, LANES), 1) < SB_HEAD_DIM
    q2 = jnp.concatenate([jnp.where(head_a, q, zero_q), jnp.where(head_a, zero_q, q)], axis=0)
    suffix = jnp.where(lax.broadcasted_iota(jnp.int32, (t, t), 0) > lax.broadcasted_iota(jnp.int32, (t, t), 1),
                       1.0, 0.0).astype(BF16)
    causal = (lax.broadcasted_iota(jnp.int32, (2 * t, t), 1)
              < (lax.broadcasted_iota(jnp.int32, (2 * t, t), 0) & (t - 1)))

    def logits(kb, slot, diagonal=False):
        k = k_ref[0, pl.ds(pl.multiple_of(kb * t, t), t), :]
        z = _dot_nt(q2, k)
        log_beta = jnp.minimum(z, 0.0) - jnp.log(1.0 + jnp.exp(-jnp.abs(z)))
        log_keep = log_beta - z
        if diagonal:
            log_beta = jnp.where(causal, log_beta, MASKED)
            log_keep = jnp.where(causal, log_keep, 0.0)
        lb_ref[slot] = log_beta
        lk_ref[slot] = log_keep.astype(BF16)
        tot_ref[slot] = jnp.sum(log_keep, axis=1, keepdims=True)

    def weigh(kb, slot):
        v = v_ref[0, pl.ds(pl.multiple_of(kb * t, t), t), :]
        sums = _dot(lk_ref[slot], suffix)
        w = jnp.exp(lb_ref[slot] + sums + carry_ref[...])
        zero_v = jnp.zeros_like(v)
        v2 = jnp.concatenate([jnp.where(head_a, v, zero_v), jnp.where(head_a, zero_v, v)], axis=0)
        w2 = jnp.concatenate([w[:t], w[t:]], axis=1).astype(BF16)
        acc_ref[...] += _dot(w2, v2)

    acc_ref[...] = jnp.zeros_like(acc_ref)
    carry_ref[...] = jnp.zeros_like(carry_ref)
    logits(i, 0, diagonal=True)

    def advance(slot):
        carry_ref[...] = carry_ref[...] + tot_ref[slot]

    def more(kb):
        return jnp.logical_and(kb >= 0, jnp.max(carry_ref[...]) > SB_DEAD).astype(jnp.int32)

    @pl.when(i == 0)
    def _():
        weigh(0, 0)

    @pl.when(i == 1)
    def _():
        weigh(1, 0)
        logits(0, 1)
        advance(0)
        weigh(0, 1)

    @pl.when(i >= 2)
    def _():
        weigh(i, 0)
        logits(i - 1, 1)
        advance(0)
        weigh(i - 1, 1)
        logits(i - 2, 0)
        advance(1)
        weigh(i - 2, 0)
        logits(jnp.maximum(i - 3, 0), 1)
        advance(0)

        def step(state):
            n, _ = state
            kb = i - 3 - n
            slot = (n + 1) % 2
            weigh(kb, slot)
            logits(jnp.maximum(kb - 1, 0), 1 - slot)
            advance(slot)
            return n + 1, more(kb - 1)

        lax.while_loop(lambda s: s[1] > 0, step, (jnp.int32(0), more(i - 3)))

    o_ref[0] = acc_ref[...].astype(o_ref.dtype)


def _sb_attention(q, k, v):
    bsz, seq, _ = q.shape
    tq = SB_TILE
    pairs = SB_W // LANES
    return pl.pallas_call(
        _sb_kernel,
        grid=(bsz, pairs, seq // tq),
        in_specs=[pl.BlockSpec((1, tq, LANES), lambda b, p, i: (b, i, p)),
                  pl.BlockSpec((1, seq, LANES), lambda b, p, i: (b, 0, p)),
                  pl.BlockSpec((1, seq, LANES), lambda b, p, i: (b, 0, p))],
        out_specs=pl.BlockSpec((1, tq, LANES), lambda b, p, i: (b, i, p)),
        out_shape=jax.ShapeDtypeStruct((bsz, seq, SB_W), BF16),
        scratch_shapes=[pltpu.VMEM((tq, LANES), F32), pltpu.VMEM((2 * tq, 1), F32),
                        pltpu.VMEM((2, 2 * tq, tq), F32), pltpu.VMEM((2, 2 * tq, tq), BF16),
                        pltpu.VMEM((2, 2 * tq, 1), F32)],
        compiler_params=_params("parallel", "parallel", "arbitrary"),
        name="stickbreak_attn",
    )(q, k, v)


def _gdn_kernel(qkv_ref, z_ref, ab_ref, abt_ref, prow_ref, pcol_ref, gout_ref, o_ref,
                state_ref, lhs_ref, add_ref, gam_ref, *, steps_per_seq):
    ts = DN_STEP
    chunks = ts // DN_CHUNK
    j = pl.program_id(0)
    fill = j % 2

    @pl.when(j == 0)
    def _():
        lhs_ref[1] = jnp.zeros(lhs_ref.shape[1:], lhs_ref.dtype)
        add_ref[1] = jnp.zeros(add_ref.shape[1:], add_ref.dtype)
        gam_ref[1] = jnp.zeros(gam_ref.shape[1:], gam_ref.dtype)

    @pl.when(jnp.logical_or(j == 0, (j + steps_per_seq - 1) % steps_per_seq == 0))
    def _():
        state_ref[...] = jnp.zeros_like(state_ref)

    _gdn_scan(j, z_ref, gout_ref, o_ref, state_ref, lhs_ref, add_ref, gam_ref)

    tb = 2 * DN_CHUNK
    ri = lax.broadcasted_iota(jnp.int32, (tb, tb), 0)
    ci = lax.broadcasted_iota(jnp.int32, (tb, tb), 1)
    same_chunk = (ri >> 6) == (ci >> 6)
    m_incl = same_chunk & (ci <= ri)
    m_strict = same_chunk & (ci < ri)
    eye = jnp.where(ri == ci, 1.0, 0.0)
    tri = jnp.where(m_incl, 1.0, 0.0).astype(BF16)
    tri_t = jnp.where(same_chunk & (ri <= ci), 1.0, 0.0).astype(BF16)
    units = [(h, r) for h in range(DN_HEADS) for r in range(ts // tb)]
    rows_of = lambda r: slice(r * tb, (r + 1) * tb)

    ab = ab_ref[...]
    beta_all = _sigmoid(ab)
    g_all = -jnp.exp(prow_ref[0:1, :]) * _softplus(ab + prow_ref[1:2, :])
    abt = abt_ref[0]
    g_rows = -jnp.exp(pcol_ref[:, 0:1]) * _softplus(abt + pcol_ref[:, 1:2])
    g_all3 = _split_bf16(g_all, 3)
    g_rows3 = _split_bf16(g_rows, 3)
    gcum_all = [sum(_dot(tri, p[rows_of(r)]) for p in g_all3) for r in range(ts // tb)]
    gcum_rows = [sum(_dot(p[:, rows_of(r)], tri_t) for p in g_rows3) for r in range(ts // tb)]

    def col(h, part):
        return slice(part * DN_W + h * DN_DIM, part * DN_W + (h + 1) * DN_DIM)

    q = {u: qkv_ref[rows_of(u[1]), col(u[0], 0)] for u in units}
    k = {u: qkv_ref[rows_of(u[1]), col(u[0], 1)] for u in units}
    v = {u: qkv_ref[rows_of(u[1]), col(u[0], 2)] for u in units}
    beta = {(h, r): beta_all[rows_of(r), h:h + 1] for h, r in units}
    gc = {(h, r): gcum_all[r][:, DN_HEADS + h:DN_HEADS + h + 1] for h, r in units}
    gr = {(h, r): gcum_rows[r][DN_HEADS + h:DN_HEADS + h + 1, :] for h, r in units}
    decay = {u: jnp.where(m_incl, jnp.exp(jnp.where(m_incl, gc[u] - gr[u], 0.0)), 0.0) for u in units}
    kb = {u: k[u] * beta[u] for u in units}
    vb = {u: v[u] * beta[u] for u in units}
    eg = {u: jnp.exp(gc[u]) for u in units}
    qg = {u: q[u] * eg[u] for u in units}
    kq = {u: _dot_nt(jnp.concatenate([kb[u], q[u]], axis=0).astype(BF16), k[u].astype(BF16)) for u in units}
    lower = {u: jnp.where(m_strict, kq[u][:tb] * decay[u], 0.0) for u in units}
    a_qk = {u: (kq[u][tb:] * decay[u]).astype(BF16) for u in units}

    blk8 = (ri >> 3) == (ci >> 3)
    m8 = {u: jnp.where(blk8, -lower[u], 0.0).astype(BF16) for u in units}
    inv = {u: eye + m8[u].astype(F32) for u in units}
    p = {u: _dot(m8[u], m8[u]).astype(BF16) for u in units}
    inv = {u: inv[u] + _dot(inv[u].astype(BF16), p[u]) for u in units}
    p = {u: _dot(p[u], p[u]).astype(BF16) for u in units}
    inv = {u: inv[u] + _dot(inv[u].astype(BF16), p[u]) for u in units}
    for lg in (3, 4, 5):
        off = ((ri >> (lg + 1)) == (ci >> (lg + 1))) & (((ri >> lg) & 1) == 1) & (((ci >> lg) & 1) == 0)
        inv16 = {u: inv[u].astype(BF16) for u in units}
        cb = {u: _dot(jnp.where(off, lower[u], 0.0).astype(BF16), inv16[u]).astype(BF16) for u in units}
        inv = {u: inv[u] - _dot(inv16[u], cb[u]) for u in units}
    sol = {u: _dot(inv[u].astype(BF16), jnp.concatenate([vb[u], kb[u] * eg[u]], axis=1).astype(BF16))
           for u in units}

    a_sol = {u: _dot(a_qk[u], sol[u].astype(BF16)) for u in units}
    for u in units:
        h, r = u
        q_eff = (qg[u] - a_sol[u][:, DN_DIM:]).astype(BF16)
        for cc in range(tb // DN_CHUNK):
            c = r * (tb // DN_CHUNK) + cc
            rows = slice(cc * DN_CHUNK, (cc + 1) * DN_CHUNK)
            g_last = gc[u][(cc + 1) * DN_CHUNK - 1:(cc + 1) * DN_CHUNK, :]
            k_dec = k[u][rows] * jnp.exp(g_last - gc[u][rows])
            kd_sol = _dot_tn(k_dec.astype(BF16), sol[u][rows].astype(BF16))
            lhs_ref[fill, h, c, 0:DN_CHUNK, :] = q_eff[rows]
            lhs_ref[fill, h, c, DN_CHUNK:, :] = kd_sol[:, DN_DIM:].astype(BF16)
            add_ref[fill, h, c, 0:DN_CHUNK, :] = a_sol[u][rows, :DN_DIM]
            add_ref[fill, h, c, DN_CHUNK:, :] = kd_sol[:, :DN_DIM]
            gam_ref[fill, h * chunks + c:h * chunks + c + 1, :] = jnp.broadcast_to(jnp.exp(g_last), (1, LANES))


def _gdn_scan(j, z_ref, gout_ref, o_ref, state_ref, lhs_ref, add_ref, gam_ref):
    chunks = DN_STEP // DN_CHUNK
    drain = (j + 1) % 2
    out = [[] for _ in range(DN_HEADS)]
    for c in range(chunks):
        for h in range(DN_HEADS):
            state = state_ref[h]
            res = _dot(lhs_ref[drain, h, c], state.astype(BF16))
            add = add_ref[drain, h, c]
            out[h].append(res[:DN_CHUNK] + add[:DN_CHUNK])
            gamma = gam_ref[drain, h * chunks + c:h * chunks + c + 1, :]
            state_ref[h] = state * gamma - res[DN_CHUNK:] + add[DN_CHUNK:]
    for h in range(DN_HEADS):
        o = jnp.concatenate(out[h], axis=0)
        o = o * lax.rsqrt(jnp.mean(o * o, axis=-1, keepdims=True) + EPS) * gout_ref[...]
        zh = z_ref[:, h * DN_DIM:(h + 1) * DN_DIM]
        o_ref[:, h * DN_DIM:(h + 1) * DN_DIM] = (o * (zh * _sigmoid(zh))).astype(o_ref.dtype)


def _gated_deltanet(qkv, z, ab, abt, prow, pcol, g_out, seq):
    tokens = qkv.shape[0]
    ts = DN_STEP
    steps = seq // ts
    total = tokens // ts
    chunks = ts // DN_CHUNK
    fill_step = lambda j: jnp.minimum(j, total - 1)
    drain_step = lambda j: jnp.maximum(j - 1, 0)
    return pl.pallas_call(
        functools.partial(_gdn_kernel, steps_per_seq=steps),
        grid=(total + 1,),
        in_specs=[pl.BlockSpec((ts, 3 * DN_W), lambda j: (fill_step(j), 0)),
                  pl.BlockSpec((ts, DN_W), lambda j: (drain_step(j), 0)),
                  pl.BlockSpec((ts, LANES), lambda j: (fill_step(j), 0)),
                  pl.BlockSpec((1, 2 * SUBLANES, ts), lambda j: (fill_step(j) // steps, 0, fill_step(j) % steps)),
                  pl.BlockSpec((2, LANES), lambda j: (0, 0)),
                  pl.BlockSpec((2 * SUBLANES, LANES), lambda j: (0, 0)),
                  pl.BlockSpec((1, DN_DIM), lambda j: (0, 0))],
        out_specs=pl.BlockSpec((ts, DN_W), lambda j: (drain_step(j), 0)),
        out_shape=jax.ShapeDtypeStruct((tokens, DN_W), BF16),
        scratch_shapes=[pltpu.VMEM((DN_HEADS, DN_DIM, DN_DIM), F32),
                        pltpu.VMEM((2, DN_HEADS, chunks, DN_CHUNK + DN_DIM, DN_DIM), BF16),
                        pltpu.VMEM((2, DN_HEADS, chunks, DN_CHUNK + DN_DIM, DN_DIM), F32),
                        pltpu.VMEM((2, DN_HEADS * chunks, LANES), F32)],
        compiler_params=_params("arbitrary"),
        name="gated_deltanet",
    )(qkv, z, ab, abt, prow, pcol, g_out.reshape(1, DN_DIM))


def _merge_kernel(h_ref, attn_ref, dn_ref, mod_ref, g_ref, wr_ref, wus_ref, wud_ref, wo_ref, out_ref,
                  u_ref, m_ref):
    u_ref[...] = _rms_mod(h_ref[...], g_ref[...], mod_ref[0, 3:4, :], mod_ref[0, 4:5, :]).astype(BF16)
    for n in range(D_MODEL // MXU_DIM):
        cols = slice(n * MXU_DIM, (n + 1) * MXU_DIM)
        r_sb = _dot(u_ref[...], wr_ref[:, cols])
        r_dn = _dot(u_ref[...], wr_ref[:, D_MODEL + n * MXU_DIM:D_MODEL + (n + 1) * MXU_DIM])
        y_sb = _dot(attn_ref[...], wus_ref[:, cols])
        y_dn = _dot(dn_ref[...], wud_ref[:, cols])
        m_ref[:, cols] = (_sigmoid(r_sb) * y_sb + _sigmoid(r_dn) * y_dn).astype(BF16)
    for n in range(D_MODEL // MXU_DIM):
        cols = slice(n * MXU_DIM, (n + 1) * MXU_DIM)
        out_ref[:, cols] = h_ref[:, cols] + mod_ref[0, 5:6, cols] * _dot(m_ref[...], wo_ref[:, cols])


def _merge(h2, attn, dn, mod3, g_mix, w_r, w_up_sb, w_up_dn, w_out, seq):
    tokens = h2.shape[0]
    tm = TOKEN_TILE
    tiles_per_seq = seq // tm
    row = lambda w: pl.BlockSpec((tm, w), lambda i: (i, 0))
    return pl.pallas_call(
        _merge_kernel,
        grid=(tokens // tm,),
        in_specs=[row(D_MODEL), row(SB_W), row(DN_W),
                  pl.BlockSpec((1, N_MOD, D_MODEL), lambda i: (i // tiles_per_seq, 0, 0)),
                  _resident((1, D_MODEL), lambda i: (0, 0)),
                  _resident((D_MODEL, 2 * D_MODEL), lambda i: (0, 0)),
                  _resident((SB_W, D_MODEL), lambda i: (0, 0)),
                  _resident((DN_W, D_MODEL), lambda i: (0, 0)),
                  _resident((D_MODEL, D_MODEL), lambda i: (0, 0))],
        out_specs=row(D_MODEL),
        out_shape=jax.ShapeDtypeStruct((tokens, D_MODEL), F32),
        scratch_shapes=[pltpu.VMEM((tm, D_MODEL), BF16), pltpu.VMEM((tm, D_MODEL), BF16)],
        compiler_params=_params("parallel"),
        name="mixer_merge",
    )(h2, attn, dn, mod3, g_mix.reshape(1, D_MODEL), w_r, w_up_sb, w_up_dn, w_out)


def _layer(h2, mod3, seq, g_ffn1, w_ffn1_in, w_ffn1_out, g_mix, w_in, g_q_sb, g_k_sb, w_conv, a_log,
           dt_bias, g_dn_out, w_up_sb, w_up_dn, w_out, g_ffn2, w_ffn2_in, w_ffn2_out):
    tokens = h2.shape[0]
    bsz = tokens // seq
    h2 = _ffn(h2, mod3, g_ffn1, w_ffn1_in.astype(BF16), w_ffn1_out.astype(BF16), 0, seq)

    gate0 = _PROJ_MAIN_W
    gate1 = gate0 + 2 * DN_HEADS
    w_main = w_in[:, :gate0].astype(BF16)
    w_gate = w_in[:, gate0:gate1]
    w_ab = jnp.pad(w_gate, ((0, 0), (0, LANES - 2 * DN_HEADS))).astype(BF16)
    w_abt = jnp.pad(w_gate.T, ((0, 2 * SUBLANES - 2 * DN_HEADS), (0, 0))).astype(BF16)
    w_r = w_in[:, gate1:].astype(BF16)
    gq_row = jnp.tile(g_q_sb, SB_HEADS).reshape(1, SB_W)
    gk_row = jnp.tile(g_k_sb, SB_HEADS).reshape(1, SB_W)
    q, k, v, qkv_dn, z_dn, ab, abt = _projection(h2, mod3, g_mix, w_main, w_ab, w_abt, gq_row, gk_row,
                                                 w_conv, seq)

    attn = _sb_attention(q.reshape(bsz, seq, SB_W), k.reshape(bsz, seq, SB_W), v.reshape(bsz, seq, SB_W))

    lanes = slice(DN_HEADS, 2 * DN_HEADS)
    prow = jnp.zeros((2, LANES), F32).at[0, lanes].set(a_log).at[1, lanes].set(dt_bias)
    pcol = jnp.zeros((2 * SUBLANES, LANES), F32).at[lanes, 0].set(a_log).at[lanes, 1].set(dt_bias)
    dn = _gated_deltanet(qkv_dn, z_dn, ab, abt, prow, pcol, g_dn_out, seq)

    h2 = _merge(h2, attn.reshape(tokens, SB_W), dn, mod3, g_mix, w_r,
                w_up_sb.astype(BF16), w_up_dn.astype(BF16), w_out.astype(BF16), seq)
    return _ffn(h2, mod3, g_ffn2, w_ffn2_in.astype(BF16), w_ffn2_out.astype(BF16), 6, seq)


def kernel(x, c, w_ada, b_ada, g_ffn1, w_ffn1_in, w_ffn1_out, g_mix, w_in, g_q_sb, g_k_sb, w_conv, a_log,
           dt_bias, g_dn_out, w_up_sb, w_up_dn, w_out, g_ffn2, w_ffn2_in, w_ffn2_out):
    bsz, seq, d = x.shape
    depth = w_ada.shape[0]
    h2 = x.reshape(bsz * seq, d)
    for l in range(depth):
        mod3 = _modulation(c, w_ada[l], b_ada[l]).reshape(bsz, N_MOD, d)
        h2 = _layer(h2, mod3, seq, g_ffn1[l], w_ffn1_in[l], w_ffn1_out[l], g_mix[l], w_in[l], g_q_sb[l],
                    g_k_sb[l], w_conv[l], a_log[l], dt_bias[l], g_dn_out[l], w_up_sb[l], w_up_dn[l],
                    w_out[l], g_ffn2[l], w_ffn2_in[l], w_ffn2_out[l])
    return h2.reshape(bsz, seq, d).astype(x.dtype)
```

```python
import functools

import jax
import jax.numpy as jnp
from jax import lax
from jax.experimental import pallas as pl
from jax.experimental.pallas import tpu as pltpu

F32 = jnp.float32
BF16 = jnp.bfloat16

D_MODEL = 1024
D_FF = 2816
N_MOD = 9
EPS = 1e-6

SB_HEADS = 8
SB_HEAD_DIM = 64
SB_W = SB_HEADS * SB_HEAD_DIM
DN_HEADS = 4
DN_DIM = 128
DN_W = DN_HEADS * DN_DIM
DN_CONV = 4
DN_CHUNK = 64

LANES = 128
SUBLANES = 8
MXU_DIM = 256
VMEM_LIMIT = 56 * 1024 * 1024

TOKEN_TILE = 1024
FF_CHUNK = MXU_DIM
SB_TILE = MXU_DIM
MASKED = -1e30
SB_DEAD = -104.0
DN_STEP = 8 * DN_CHUNK

_dot = functools.partial(jnp.dot, preferred_element_type=F32)


def _dot_nt(a, b):
    return lax.dot_general(a, b, (((1,), (1,)), ((), ())), preferred_element_type=F32)


def _dot_tn(a, b):
    return lax.dot_general(a, b, (((0,), (0,)), ((), ())), preferred_element_type=F32)


def _sigmoid(x):
    return 1.0 / (1.0 + jnp.exp(-x))


def _softplus(x):
    return jnp.maximum(x, 0.0) + jnp.log(1.0 + jnp.exp(-jnp.abs(x)))


def _split_bf16(x, pieces):
    out = []
    for _ in range(pieces - 1):
        p = x.astype(BF16)
        out.append(p)
        x = x - p.astype(F32)
    out.append(x.astype(BF16))
    return out


def _rms_mod(x, gain, shift, scale):
    y = x * lax.rsqrt(jnp.mean(x * x, axis=-1, keepdims=True) + EPS)
    return (y * gain) * (1.0 + scale) + shift


def _resident(shape, index_map):
    return pl.BlockSpec(shape, index_map, pipeline_mode=pl.Buffered(1))


def _params(*semantics):
    return pltpu.CompilerParams(dimension_semantics=semantics, vmem_limit_bytes=VMEM_LIMIT)


def _mod_kernel(c_ref, w_ref, b_ref, o_ref):
    c = c_ref[...]
    cond = (c * _sigmoid(c)).astype(BF16)
    o_ref[...] = _dot(cond, w_ref[...].astype(BF16)) + b_ref[...]


def _modulation(c, w_ada, b_ada):
    bsz = c.shape[0]
    n = w_ada.shape[1]
    tn = D_MODEL
    return pl.pallas_call(
        _mod_kernel,
        grid=(n // tn,),
        in_specs=[pl.BlockSpec((bsz, D_MODEL), lambda j: (0, 0)),
                  pl.BlockSpec((D_MODEL, tn), lambda j: (0, j)),
                  pl.BlockSpec((1, tn), lambda j: (0, j))],
        out_specs=pl.BlockSpec((bsz, tn), lambda j: (0, j)),
        out_shape=jax.ShapeDtypeStruct((bsz, n), F32),
        compiler_params=_params("arbitrary"),
        name="adaln_mod",
    )(c, w_ada, b_ada.reshape(1, n))


def _ffn_kernel(x_ref, mod_ref, g_ref, wa_ref, wb_ref, wo_ref, out_ref, u_ref, h_ref, *, mod_row):
    shift = mod_ref[0, mod_row:mod_row + 1, :]
    scale = mod_ref[0, mod_row + 1:mod_row + 2, :]
    u_ref[...] = _rms_mod(x_ref[...], g_ref[...], shift, scale).astype(BF16)
    for c in range(D_FF // FF_CHUNK):
        cols = slice(c * FF_CHUNK, (c + 1) * FF_CHUNK)
        a = _dot(u_ref[...], wa_ref[:, cols])
        b = _dot(u_ref[...], wb_ref[:, cols])
        h_ref[:, cols] = (a * _sigmoid(a) * b).astype(BF16)
    for n in range(D_MODEL // MXU_DIM):
        cols = slice(n * MXU_DIM, (n + 1) * MXU_DIM)
        gate = mod_ref[0, mod_row + 2:mod_row + 3, cols]
        out_ref[:, cols] = x_ref[:, cols] + (0.5 * gate) * _dot(h_ref[...], wo_ref[:, cols])


def _ffn(x2, mod3, gain, w_in, w_out, mod_row, seq):
    tokens = x2.shape[0]
    tm = TOKEN_TILE
    tiles_per_seq = seq // tm
    return pl.pallas_call(
        functools.partial(_ffn_kernel, mod_row=mod_row),
        grid=(tokens // tm,),
        in_specs=[pl.BlockSpec((tm, D_MODEL), lambda i: (i, 0)),
                  pl.BlockSpec((1, N_MOD, D_MODEL), lambda i: (i // tiles_per_seq, 0, 0)),
                  _resident((1, D_MODEL), lambda i: (0, 0)),
                  _resident((D_MODEL, D_FF), lambda i: (0, 0)),
                  _resident((D_MODEL, D_FF), lambda i: (0, 1)),
                  _resident((D_FF, D_MODEL), lambda i: (0, 0))],
        out_specs=pl.BlockSpec((tm, D_MODEL), lambda i: (i, 0)),
        out_shape=jax.ShapeDtypeStruct((tokens, D_MODEL), F32),
        scratch_shapes=[pltpu.VMEM((tm, D_MODEL), BF16), pltpu.VMEM((tm, D_FF), BF16)],
        compiler_params=_params("parallel"),
        name=f"ffn_row{mod_row}",
    )(x2, mod3, gain.reshape(1, D_MODEL), w_in, w_in, w_out)


_PROJ_Q, _PROJ_K, _PROJ_V = 0, SB_W, 2 * SB_W
_PROJ_DN = 3 * SB_W
_PROJ_Z = _PROJ_DN + 3 * DN_W
_PROJ_MAIN_W = _PROJ_Z + DN_W


def _proj_kernel(h_ref, mod_ref, g_ref, wm_ref, wab_ref, wabt_ref, gq_ref, gk_ref, grp_ref, wc_ref,
                 q_ref, k_ref, v_ref, dn_ref, z_ref, ab_ref, abt_ref, u_ref, xbuf_ref, *, tiles_per_seq):
    tm = TOKEN_TILE
    hist = SUBLANES
    u_ref[...] = _rms_mod(h_ref[...], g_ref[...], mod_ref[0, 3:4, :], mod_ref[0, 4:5, :]).astype(BF16)

    @pl.when(pl.program_id(0) % tiles_per_seq == 0)
    def _():
        xbuf_ref[0:hist, :] = jnp.zeros((hist, 3 * DN_W), F32)

    def head_norm(t, gain_row):
        hi, lo = _split_bf16(t * t, 2)
        ss = _dot(hi, grp_ref[...]) + _dot(lo, grp_ref[...])
        return t * lax.rsqrt(ss * (1.0 / SB_HEAD_DIM) + EPS) * gain_row

    def conv_group(g):
        cols = slice(g * DN_DIM, (g + 1) * DN_DIM)
        acc = xbuf_ref[hist:hist + tm, cols] * wc_ref[DN_CONV - 1:DN_CONV, cols]
        for tap in range(DN_CONV - 1):
            lo = hist - (DN_CONV - 1) + tap
            acc = acc + xbuf_ref[lo:lo + tm, cols] * wc_ref[tap:tap + 1, cols]
        y = acc * _sigmoid(acc)
        if g < 2 * DN_HEADS:
            y = y * lax.rsqrt(jnp.sum(y * y, axis=-1, keepdims=True) + EPS)
        if g < DN_HEADS:
            y = y * (DN_DIM ** -0.5)
        dn_ref[:, cols] = y

    q = _dot(u_ref[...], wm_ref[:, _PROJ_Q:_PROJ_Q + SB_W])
    q_ref[...] = (head_norm(q, gq_ref[...]) * (SB_HEAD_DIM ** -0.5)).astype(BF16)
    k = _dot(u_ref[...], wm_ref[:, _PROJ_K:_PROJ_K + SB_W])
    k_ref[...] = head_norm(k, gk_ref[...]).astype(BF16)
    v_ref[...] = _dot(u_ref[...], wm_ref[:, _PROJ_V:_PROJ_V + SB_W]).astype(BF16)
    for c in range(3 * DN_W // MXU_DIM):
        cols = slice(c * MXU_DIM, (c + 1) * MXU_DIM)
        xbuf_ref[hist:hist + tm, cols] = _dot(
            u_ref[...], wm_ref[:, _PROJ_DN + c * MXU_DIM:_PROJ_DN + (c + 1) * MXU_DIM])
    for g in range(3 * DN_HEADS):
        conv_group(g)
    xbuf_ref[0:hist, :] = xbuf_ref[tm:tm + hist, :]
    z_ref[...] = _dot(u_ref[...], wm_ref[:, _PROJ_Z:_PROJ_Z + DN_W])
    ab_ref[...] = _dot(u_ref[...], wab_ref[...])
    abt_ref[0] = _dot_nt(wabt_ref[...], u_ref[...])


def _projection(h2, mod3, g_mix, w_main, w_ab, w_abt, gq_row, gk_row, w_conv, seq):
    tokens = h2.shape[0]
    bsz = tokens // seq
    tm = TOKEN_TILE
    tiles_per_seq = seq // tm
    grp = jnp.kron(jnp.eye(SB_HEADS, dtype=F32), jnp.ones((SB_HEAD_DIM, SB_HEAD_DIM), F32)).astype(BF16)
    row = lambda w: pl.BlockSpec((tm, w), lambda i: (i, 0))
    return pl.pallas_call(
        functools.partial(_proj_kernel, tiles_per_seq=tiles_per_seq),
        grid=(tokens // tm,),
        in_specs=[row(D_MODEL),
                  pl.BlockSpec((1, N_MOD, D_MODEL), lambda i: (i // tiles_per_seq, 0, 0)),
                  _resident((1, D_MODEL), lambda i: (0, 0)),
                  _resident((D_MODEL, _PROJ_MAIN_W), lambda i: (0, 0)),
                  _resident((D_MODEL, LANES), lambda i: (0, 0)),
                  _resident((2 * SUBLANES, D_MODEL), lambda i: (0, 0)),
                  _resident((1, SB_W), lambda i: (0, 0)),
                  _resident((1, SB_W), lambda i: (0, 0)),
                  _resident((SB_W, SB_W), lambda i: (0, 0)),
                  _resident((DN_CONV, 3 * DN_W), lambda i: (0, 0))],
        out_specs=[row(SB_W), row(SB_W), row(SB_W), row(3 * DN_W), row(DN_W), row(LANES),
                   pl.BlockSpec((1, 2 * SUBLANES, tm), lambda i: (i // tiles_per_seq, 0, i % tiles_per_seq))],
        out_shape=[jax.ShapeDtypeStruct((tokens, SB_W), BF16),
                   jax.ShapeDtypeStruct((tokens, SB_W), BF16),
                   jax.ShapeDtypeStruct((tokens, SB_W), BF16),
                   jax.ShapeDtypeStruct((tokens, 3 * DN_W), F32),
                   jax.ShapeDtypeStruct((tokens, DN_W), F32),
                   jax.ShapeDtypeStruct((tokens, LANES), F32),
                   jax.ShapeDtypeStruct((bsz, 2 * SUBLANES, seq), F32)],
        scratch_shapes=[pltpu.VMEM((tm, D_MODEL), BF16), pltpu.VMEM((tm + SUBLANES, 3 * DN_W), F32)],
        compiler_params=_params("arbitrary"),
        name="mixer_in_proj",
    )(h2, mod3, g_mix.reshape(1, D_MODEL), w_main, w_ab, w_abt, gq_row, gk_row, grp, w_conv)


def _sb_kernel(q_ref, k_ref, v_ref, o_ref, acc_ref, carry_ref, lb_ref, lk_ref, tot_ref, q2_ref):
    t = SB_TILE
    head_a = lax.broadcasted_iota(jnp.int32, (t, LANES), 1) < SB_HEAD_DIM
    suffix = jnp.where(lax.broadcasted_iota(jnp.int32, (t, t), 0) > lax.broadcasted_iota(jnp.int32, (t, t), 1),
                       1.0, 0.0).astype(BF16)
    causal = (lax.broadcasted_iota(jnp.int32, (2 * t, t), 1)
              < (lax.broadcasted_iota(jnp.int32, (2 * t, t), 0) & (t - 1)))

    def logits(kb, slot, diagonal=False):
        k = k_ref[0, pl.ds(pl.multiple_of(kb * t, t), t), :]
        z = _dot_nt(q2_ref[...], k)
        log_beta = jnp.minimum(z, 0.0) - jnp.log(1.0 + jnp.exp(-jnp.abs(z)))
        log_keep = log_beta - z
        if diagonal:
            log_beta = jnp.where(causal, log_beta, MASKED)
            log_keep = jnp.where(causal, log_keep, 0.0)
        lb_ref[slot] = log_beta
        lk_ref[slot] = log_keep.astype(BF16)

    def weigh(kb, slot):
        v = v_ref[0, pl.ds(pl.multiple_of(kb * t, t), t), :]
        log_keep = lk_ref[slot]
        sums = _dot(log_keep, suffix)
        w = jnp.exp(lb_ref[slot] + sums + carry_ref[...])
        tot_ref[slot] = sums[:, 0:1] + log_keep[:, 0:1].astype(F32)
        zero_v = jnp.zeros_like(v)
        v2 = jnp.concatenate([jnp.where(head_a, v, zero_v), jnp.where(head_a, zero_v, v)], axis=0)
        w2 = jnp.concatenate([w[:t], w[t:]], axis=1).astype(BF16)
        acc_ref[...] += _dot(w2, v2)

    def advance(slot):
        carry_ref[...] = carry_ref[...] + tot_ref[slot]

    def more(kb):
        return jnp.logical_and(kb >= 0, jnp.max(carry_ref[...]) > SB_DEAD).astype(jnp.int32)

    def tile(i, _):
        rows = pl.ds(pl.multiple_of(i * t, t), t)
        q = q_ref[0, rows, :]
        zero_q = jnp.zeros_like(q)
        q2_ref[...] = jnp.concatenate([jnp.where(head_a, q, zero_q), jnp.where(head_a, zero_q, q)], axis=0)
        acc_ref[...] = jnp.zeros_like(acc_ref)
        carry_ref[...] = jnp.zeros_like(carry_ref)
        logits(i, 0, diagonal=True)

        @pl.when(i == 0)
        def _():
            weigh(0, 0)

        @pl.when(i == 1)
        def _():
            weigh(1, 0)
            logits(0, 1)
            advance(0)
            weigh(0, 1)

        @pl.when(i >= 2)
        def _():
            weigh(i, 0)
            logits(i - 1, 1)
            advance(0)
            weigh(i - 1, 1)
            logits(i - 2, 0)
            advance(1)
            weigh(i - 2, 0)
            logits(jnp.maximum(i - 3, 0), 1)
            advance(0)

            def step(state):
                n, _ = state
                kb = i - 3 - n
                slot = (n + 1) % 2
                weigh(kb, slot)
                logits(jnp.maximum(kb - 1, 0), 1 - slot)
                advance(slot)
                return n + 1, more(kb - 1)

            lax.while_loop(lambda s: s[1] > 0, step, (jnp.int32(0), more(i - 3)))

        o_ref[0, rows, :] = acc_ref[...].astype(o_ref.dtype)
        return 0

    lax.fori_loop(0, q_ref.shape[1] // t, tile, 0)


def _sb_attention(q, k, v):
    bsz, seq, _ = q.shape
    t = SB_TILE
    pairs = SB_W // LANES
    whole = pl.BlockSpec((1, seq, LANES), lambda b, p: (b, 0, p))
    return pl.pallas_call(
        _sb_kernel,
        grid=(bsz, pairs),
        in_specs=[whole, whole, whole],
        out_specs=whole,
        out_shape=jax.ShapeDtypeStruct((bsz, seq, SB_W), BF16),
        scratch_shapes=[pltpu.VMEM((t, LANES), F32), pltpu.VMEM((2 * t, 1), F32),
                        pltpu.VMEM((2, 2 * t, t), F32), pltpu.VMEM((2, 2 * t, t), BF16),
                        pltpu.VMEM((2, 2 * t, 1), F32), pltpu.VMEM((2 * t, LANES), BF16)],
        compiler_params=_params("parallel", "parallel"),
        name="stickbreak_attn",
    )(q, k, v)


def _gdn_kernel(qkv_ref, z_ref, ab_ref, abt_ref, prow_ref, pcol_ref, gout_ref, o_ref,
                state_ref, lhs_ref, add_ref, gam_ref, *, steps_per_seq):
    ts = DN_STEP
    chunks = ts // DN_CHUNK
    j = pl.program_id(0)
    fill = j % 2

    @pl.when(j == 0)
    def _():
        lhs_ref[1] = jnp.zeros(lhs_ref.shape[1:], lhs_ref.dtype)
        add_ref[1] = jnp.zeros(add_ref.shape[1:], add_ref.dtype)
        gam_ref[1] = jnp.zeros(gam_ref.shape[1:], gam_ref.dtype)

    @pl.when(jnp.logical_or(j == 0, (j + steps_per_seq - 1) % steps_per_seq == 0))
    def _():
        state_ref[...] = jnp.zeros_like(state_ref)

    _gdn_scan(j, z_ref, gout_ref, o_ref, state_ref, lhs_ref, add_ref, gam_ref)

    tb = 2 * DN_CHUNK
    ri = lax.broadcasted_iota(jnp.int32, (tb, tb), 0)
    ci = lax.broadcasted_iota(jnp.int32, (tb, tb), 1)
    same_chunk = (ri >> 6) == (ci >> 6)
    m_incl = same_chunk & (ci <= ri)
    m_strict = same_chunk & (ci < ri)
    eye = jnp.where(ri == ci, 1.0, 0.0)
    tri = jnp.where(m_incl, 1.0, 0.0).astype(BF16)
    tri_t = jnp.where(same_chunk & (ri <= ci), 1.0, 0.0).astype(BF16)
    units = [(h, r) for h in range(DN_HEADS) for r in range(ts // tb)]
    rows_of = lambda r: slice(r * tb, (r + 1) * tb)

    ab = ab_ref[...]
    beta_all = _sigmoid(ab)
    g_all = -jnp.exp(prow_ref[0:1, :]) * _softplus(ab + prow_ref[1:2, :])
    abt = abt_ref[0]
    g_rows = -jnp.exp(pcol_ref[:, 0:1]) * _softplus(abt + pcol_ref[:, 1:2])
    g_all3 = _split_bf16(g_all, 3)
    g_rows3 = _split_bf16(g_rows, 3)
    gcum_all = [sum(_dot(tri, p[rows_of(r)]) for p in g_all3) for r in range(ts // tb)]
    gcum_rows = [sum(_dot(p[:, rows_of(r)], tri_t) for p in g_rows3) for r in range(ts // tb)]

    def col(h, part):
        return slice(part * DN_W + h * DN_DIM, part * DN_W + (h + 1) * DN_DIM)

    q = {u: qkv_ref[rows_of(u[1]), col(u[0], 0)] for u in units}
    k = {u: qkv_ref[rows_of(u[1]), col(u[0], 1)] for u in units}
    v = {u: qkv_ref[rows_of(u[1]), col(u[0], 2)] for u in units}
    beta = {(h, r): beta_all[rows_of(r), h:h + 1] for h, r in units}
    gc = {(h, r): gcum_all[r][:, DN_HEADS + h:DN_HEADS + h + 1] for h, r in units}
    gr = {(h, r): gcum_rows[r][DN_HEADS + h:DN_HEADS + h + 1, :] for h, r in units}
    decay = {u: jnp.where(m_incl, jnp.exp(jnp.where(m_incl, gc[u] - gr[u], 0.0)), 0.0) for u in units}
    kb = {u: k[u] * beta[u] for u in units}
    vb = {u: v[u] * beta[u] for u in units}
    eg = {u: jnp.exp(gc[u]) for u in units}
    qg = {u: q[u] * eg[u] for u in units}
    kq = {u: _dot_nt(jnp.concatenate([kb[u], q[u]], axis=0).astype(BF16), k[u].astype(BF16)) for u in units}
    lower = {u: jnp.where(m_strict, kq[u][:tb] * decay[u], 0.0) for u in units}
    a_qk = {u: (kq[u][tb:] * decay[u]).astype(BF16) for u in units}

    blk8 = (ri >> 3) == (ci >> 3)
    m8 = {u: jnp.where(blk8, -lower[u], 0.0).astype(BF16) for u in units}
    inv = {u: eye + m8[u].astype(F32) for u in units}
    p = {u: _dot(m8[u], m8[u]).astype(BF16) for u in units}
    inv = {u: inv[u] + _dot(inv[u].astype(BF16), p[u]) for u in units}
    p = {u: _dot(p[u], p[u]).astype(BF16) for u in units}
    inv = {u: inv[u] + _dot(inv[u].astype(BF16), p[u]) for u in units}
    for lg in (3, 4, 5):
        off = ((ri >> (lg + 1)) == (ci >> (lg + 1))) & (((ri >> lg) & 1) == 1) & (((ci >> lg) & 1) == 0)
        inv16 = {u: inv[u].astype(BF16) for u in units}
        cb = {u: _dot(jnp.where(off, lower[u], 0.0).astype(BF16), inv16[u]).astype(BF16) for u in units}
        inv = {u: inv[u] - _dot(inv16[u], cb[u]) for u in units}
    sol = {u: _dot(inv[u].astype(BF16), jnp.concatenate([vb[u], kb[u] * eg[u]], axis=1).astype(BF16))
           for u in units}

    a_sol = {u: _dot(a_qk[u], sol[u].astype(BF16)) for u in units}
    for u in units:
        h, r = u
        q_eff = (qg[u] - a_sol[u][:, DN_DIM:]).astype(BF16)
        for cc in range(tb // DN_CHUNK):
            c = r * (tb // DN_CHUNK) + cc
            rows = slice(cc * DN_CHUNK, (cc + 1) * DN_CHUNK)
            g_last = gc[u][(cc + 1) * DN_CHUNK - 1:(cc + 1) * DN_CHUNK, :]
            k_dec = k[u][rows] * jnp.exp(g_last - gc[u][rows])
            kd_sol = _dot_tn(k_dec.astype(BF16), sol[u][rows].astype(BF16))
            lhs_ref[fill, h, c, 0:DN_CHUNK, :] = q_eff[rows]
            lhs_ref[fill, h, c, DN_CHUNK:, :] = kd_sol[:, DN_DIM:].astype(BF16)
            add_ref[fill, h, c, 0:DN_CHUNK, :] = a_sol[u][rows, :DN_DIM]
            add_ref[fill, h, c, DN_CHUNK:, :] = kd_sol[:, :DN_DIM]
            gam_ref[fill, h * chunks + c:h * chunks + c + 1, :] = jnp.broadcast_to(jnp.exp(g_last), (1, LANES))


def _gdn_scan(j, z_ref, gout_ref, o_ref, state_ref, lhs_ref, add_ref, gam_ref):
    chunks = DN_STEP // DN_CHUNK
    drain = (j + 1) % 2
    out = [[] for _ in range(DN_HEADS)]
    for c in range(chunks):
        for h in range(DN_HEADS):
            state = state_ref[h]
            res = _dot(lhs_ref[drain, h, c], state.astype(BF16))
            add = add_ref[drain, h, c]
            out[h].append(res[:DN_CHUNK] + add[:DN_CHUNK])
            gamma = gam_ref[drain, h * chunks + c:h * chunks + c + 1, :]
            state_ref[h] = state * gamma - res[DN_CHUNK:] + add[DN_CHUNK:]
    for h in range(DN_HEADS):
        o = jnp.concatenate(out[h], axis=0)
        o = o * lax.rsqrt(jnp.mean(o * o, axis=-1, keepdims=True) + EPS) * gout_ref[...]
        zh = z_ref[:, h * DN_DIM:(h + 1) * DN_DIM]
        o_ref[:, h * DN_DIM:(h + 1) * DN_DIM] = (o * (zh * _sigmoid(zh))).astype(o_ref.dtype)


def _gated_deltanet(qkv, z, ab, abt, prow, pcol, g_out, seq):
    tokens = qkv.shape[0]
    ts = DN_STEP
    steps = seq // ts
    total = tokens // ts
    chunks = ts // DN_CHUNK
    fill_step = lambda j: jnp.minimum(j, total - 1)
    drain_step = lambda j: jnp.maximum(j - 1, 0)
    return pl.pallas_call(
        functools.partial(_gdn_kernel, steps_per_seq=steps),
        grid=(total + 1,),
        in_specs=[pl.BlockSpec((ts, 3 * DN_W), lambda j: (fill_step(j), 0)),
                  pl.BlockSpec((ts, DN_W), lambda j: (drain_step(j), 0)),
                  pl.BlockSpec((ts, LANES), lambda j: (fill_step(j), 0)),
                  pl.BlockSpec((1, 2 * SUBLANES, ts), lambda j: (fill_step(j) // steps, 0, fill_step(j) % steps)),
                  pl.BlockSpec((2, LANES), lambda j: (0, 0)),
                  pl.BlockSpec((2 * SUBLANES, LANES), lambda j: (0, 0)),
                  pl.BlockSpec((1, DN_DIM), lambda j: (0, 0))],
        out_specs=pl.BlockSpec((ts, DN_W), lambda j: (drain_step(j), 0)),
        out_shape=jax.ShapeDtypeStruct((tokens, DN_W), BF16),
        scratch_shapes=[pltpu.VMEM((DN_HEADS, DN_DIM, DN_DIM), F32),
                        pltpu.VMEM((2, DN_HEADS, chunks, DN_CHUNK + DN_DIM, DN_DIM), BF16),
                        pltpu.VMEM((2, DN_HEADS, chunks, DN_CHUNK + DN_DIM, DN_DIM), F32),
                        pltpu.VMEM((2, DN_HEADS * chunks, LANES), F32)],
        compiler_params=_params("arbitrary"),
        name="gated_deltanet",
    )(qkv, z, ab, abt, prow, pcol, g_out.reshape(1, DN_DIM))


def _merge_kernel(h_ref, attn_ref, dn_ref, mod_ref, g_ref, wr_ref, wus_ref, wud_ref, wo_ref, out_ref,
                  u_ref, m_ref):
    u_ref[...] = _rms_mod(h_ref[...], g_ref[...], mod_ref[0, 3:4, :], mod_ref[0, 4:5, :]).astype(BF16)
    for n in range(D_MODEL // MXU_DIM):
        cols = slice(n * MXU_DIM, (n + 1) * MXU_DIM)
        r_sb = _dot(u_ref[...], wr_ref[:, cols])
        r_dn = _dot(u_ref[...], wr_ref[:, D_MODEL + n * MXU_DIM:D_MODEL + (n + 1) * MXU_DIM])
        y_sb = _dot(attn_ref[...], wus_ref[:, cols])
        y_dn = _dot(dn_ref[...], wud_ref[:, cols])
        m_ref[:, cols] = (_sigmoid(r_sb) * y_sb + _sigmoid(r_dn) * y_dn).astype(BF16)
    for n in range(D_MODEL // MXU_DIM):
        cols = slice(n * MXU_DIM, (n + 1) * MXU_DIM)
        out_ref[:, cols] = h_ref[:, cols] + mod_ref[0, 5:6, cols] * _dot(m_ref[...], wo_ref[:, cols])


def _merge(h2, attn, dn, mod3, g_mix, w_r, w_up_sb, w_up_dn, w_out, seq):
    tokens = h2.shape[0]
    tm = TOKEN_TILE
    tiles_per_seq = seq // tm
    row = lambda w: pl.BlockSpec((tm, w), lambda i: (i, 0))
    return pl.pallas_call(
        _merge_kernel,
        grid=(tokens // tm,),
        in_specs=[row(D_MODEL), row(SB_W), row(DN_W),
                  pl.BlockSpec((1, N_MOD, D_MODEL), lambda i: (i // tiles_per_seq, 0, 0)),
                  _resident((1, D_MODEL), lambda i: (0, 0)),
                  _resident((D_MODEL, 2 * D_MODEL), lambda i: (0, 0)),
                  _resident((SB_W, D_MODEL), lambda i: (0, 0)),
                  _resident((DN_W, D_MODEL), lambda i: (0, 0)),
                  _resident((D_MODEL, D_MODEL), lambda i: (0, 0))],
        out_specs=row(D_MODEL),
        out_shape=jax.ShapeDtypeStruct((tokens, D_MODEL), F32),
        scratch_shapes=[pltpu.VMEM((tm, D_MODEL), BF16), pltpu.VMEM((tm, D_MODEL), BF16)],
        compiler_params=_params("parallel"),
        name="mixer_merge",
    )(h2, attn, dn, mod3, g_mix.reshape(1, D_MODEL), w_r, w_up_sb, w_up_dn, w_out)


def _layer(h2, mod3, seq, g_ffn1, w_ffn1_in, w_ffn1_out, g_mix, w_in, g_q_sb, g_k_sb, w_conv, a_log,
           dt_bias, g_dn_out, w_up_sb, w_up_dn, w_out, g_ffn2, w_ffn2_in, w_ffn2_out):
    tokens = h2.shape[0]
    bsz = tokens // seq
    h2 = _ffn(h2, mod3, g_ffn1, w_ffn1_in.astype(BF16), w_ffn1_out.astype(BF16), 0, seq)

    gate0 = _PROJ_MAIN_W
    gate1 = gate0 + 2 * DN_HEADS
    w_main = w_in[:, :gate0].astype(BF16)
    w_gate = w_in[:, gate0:gate1]
    w_ab = jnp.pad(w_gate, ((0, 0), (0, LANES - 2 * DN_HEADS))).astype(BF16)
    w_abt = jnp.pad(w_gate.T, ((0, 2 * SUBLANES - 2 * DN_HEADS), (0, 0))).astype(BF16)
    w_r = w_in[:, gate1:].astype(BF16)
    gq_row = jnp.tile(g_q_sb, SB_HEADS).reshape(1, SB_W)
    gk_row = jnp.tile(g_k_sb, SB_HEADS).reshape(1, SB_W)
    q, k, v, qkv_dn, z_dn, ab, abt = _projection(h2, mod3, g_mix, w_main, w_ab, w_abt, gq_row, gk_row,
                                                 w_conv, seq)

    attn = _sb_attention(q.reshape(bsz, seq, SB_W), k.reshape(bsz, seq, SB_W), v.reshape(bsz, seq, SB_W))

    lanes = slice(DN_HEADS, 2 * DN_HEADS)
    prow = jnp.zeros((2, LANES), F32).at[0, lanes].set(a_log).at[1, lanes].set(dt_bias)
    pcol = jnp.zeros((2 * SUBLANES, LANES), F32).at[lanes, 0].set(a_log).at[lanes, 1].set(dt_bias)
    dn = _gated_deltanet(qkv_dn, z_dn, ab, abt, prow, pcol, g_dn_out, seq)

    h2 = _merge(h2, attn.reshape(tokens, SB_W), dn, mod3, g_mix, w_r,
                w_up_sb.astype(BF16), w_up_dn.astype(BF16), w_out.astype(BF16), seq)
    return _ffn(h2, mod3, g_ffn2, w_ffn2_in.astype(BF16), w_ffn2_out.astype(BF16), 6, seq)


def kernel(x, c, w_ada, b_ada, g_ffn1, w_ffn1_in, w_ffn1_out, g_mix, w_in, g_q_sb, g_k_sb, w_conv, a_log,
           dt_bias, g_dn_out, w_up_sb, w_up_dn, w_out, g_ffn2, w_ffn2_in, w_ffn2_out):
    bsz, seq, d = x.shape
    depth = w_ada.shape[0]
    h2 = x.reshape(bsz * seq, d)
    for l in range(depth):
        mod3 = _modulation(c, w_ada[l], b_ada[l]).reshape(bsz, N_MOD, d)
        h2 = _layer(h2, mod3, seq, g_ffn1[l], w_ffn1_in[l], w_ffn1_out[l], g_mix[l], w_in[l], g_q_sb[l],
                    g_k_sb[l], w_conv[l], a_log[l], dt_bias[l], g_dn_out[l], w_up_sb[l], w_up_dn[l],
                    w_out[l], g_ffn2[l], w_ffn2_in[l], w_ffn2_out[l])
    return h2.reshape(bsz, seq, d).astype(x.dtype)
```

```python
import functools

import jax
import jax.numpy as jnp
from jax import lax
from jax.experimental import pallas as pl
from jax.experimental.pallas import tpu as pltpu

F32 = jnp.float32
BF16 = jnp.bfloat16

D_MODEL = 1024
D_FF = 2816
N_MOD = 9
EPS = 1e-6

SB_HEADS = 8
SB_HEAD_DIM = 64
SB_W = SB_HEADS * SB_HEAD_DIM
DN_HEADS = 4
DN_DIM = 128
DN_W = DN_HEADS * DN_DIM
DN_CONV = 4
DN_CHUNK = 64

LANES = 128
SUBLANES = 8
MXU_DIM = 256
VMEM_LIMIT = 56 * 1024 * 1024

TOKEN_TILE = 1024
FF_CHUNK = MXU_DIM
SB_TILE = MXU_DIM
MASKED = -1e30
SB_DEAD = -104.0
DN_STEP = 8 * DN_CHUNK

_dot = functools.partial(jnp.dot, preferred_element_type=F32)


def _dot_nt(a, b):
    return lax.dot_general(a, b, (((1,), (1,)), ((), ())), preferred_element_type=F32)


def _dot_tn(a, b):
    return lax.dot_general(a, b, (((0,), (0,)), ((), ())), preferred_element_type=F32)


def _sigmoid(x):
    return 1.0 / (1.0 + jnp.exp(-x))


def _softplus(x):
    return jnp.maximum(x, 0.0) + jnp.log(1.0 + jnp.exp(-jnp.abs(x)))


def _split_bf16(x, pieces):
    out = []
    for _ in range(pieces - 1):
        p = x.astype(BF16)
        out.append(p)
        x = x - p.astype(F32)
    out.append(x.astype(BF16))
    return out


def _rms_mod(x, gain, shift, scale):
    y = x * lax.rsqrt(jnp.mean(x * x, axis=-1, keepdims=True) + EPS)
    return (y * gain) * (1.0 + scale) + shift


def _resident(shape, index_map):
    return pl.BlockSpec(shape, index_map, pipeline_mode=pl.Buffered(1))


def _params(*semantics):
    return pltpu.CompilerParams(dimension_semantics=semantics, vmem_limit_bytes=VMEM_LIMIT)


def _mod_kernel(c_ref, w_ref, b_ref, o_ref):
    c = c_ref[...]
    cond = (c * _sigmoid(c)).astype(BF16)
    o_ref[...] = _dot(cond, w_ref[...].astype(BF16)) + b_ref[...]


def _modulation(c, w_ada, b_ada):
    bsz = c.shape[0]
    n = w_ada.shape[1]
    tn = D_MODEL
    return pl.pallas_call(
        _mod_kernel,
        grid=(n // tn,),
        in_specs=[pl.BlockSpec((bsz, D_MODEL), lambda j: (0, 0)),
                  pl.BlockSpec((D_MODEL, tn), lambda j: (0, j)),
                  pl.BlockSpec((1, tn), lambda j: (0, j))],
        out_specs=pl.BlockSpec((bsz, tn), lambda j: (0, j)),
        out_shape=jax.ShapeDtypeStruct((bsz, n), F32),
        compiler_params=_params("arbitrary"),
        name="adaln_mod",
    )(c, w_ada, b_ada.reshape(1, n))


def _ffn_kernel(x_ref, mod_ref, g_ref, wa_ref, wb_ref, wo_ref, out_ref, u_ref, h_ref, *, mod_row):
    shift = mod_ref[0, mod_row:mod_row + 1, :]
    scale = mod_ref[0, mod_row + 1:mod_row + 2, :]
    u_ref[...] = _rms_mod(x_ref[...], g_ref[...], shift, scale).astype(BF16)
    for c in range(D_FF // FF_CHUNK):
        cols = slice(c * FF_CHUNK, (c + 1) * FF_CHUNK)
        a = _dot(u_ref[...], wa_ref[:, cols])
        b = _dot(u_ref[...], wb_ref[:, cols])
        h_ref[:, cols] = (a * _sigmoid(a) * b).astype(BF16)
    for n in range(D_MODEL // MXU_DIM):
        cols = slice(n * MXU_DIM, (n + 1) * MXU_DIM)
        gate = mod_ref[0, mod_row + 2:mod_row + 3, cols]
        out_ref[:, cols] = x_ref[:, cols] + (0.5 * gate) * _dot(h_ref[...], wo_ref[:, cols])


def _ffn(x2, mod3, gain, w_in, w_out, mod_row, seq):
    tokens = x2.shape[0]
    tm = TOKEN_TILE
    tiles_per_seq = seq // tm
    return pl.pallas_call(
        functools.partial(_ffn_kernel, mod_row=mod_row),
        grid=(tokens // tm,),
        in_specs=[pl.BlockSpec((tm, D_MODEL), lambda i: (i, 0)),
                  pl.BlockSpec((1, N_MOD, D_MODEL), lambda i: (i // tiles_per_seq, 0, 0)),
                  _resident((1, D_MODEL), lambda i: (0, 0)),
                  _resident((D_MODEL, D_FF), lambda i: (0, 0)),
                  _resident((D_MODEL, D_FF), lambda i: (0, 1)),
                  _resident((D_FF, D_MODEL), lambda i: (0, 0))],
        out_specs=pl.BlockSpec((tm, D_MODEL), lambda i: (i, 0)),
        out_shape=jax.ShapeDtypeStruct((tokens, D_MODEL), F32),
        scratch_shapes=[pltpu.VMEM((tm, D_MODEL), BF16), pltpu.VMEM((tm, D_FF), BF16)],
        compiler_params=_params("parallel"),
        name=f"ffn_row{mod_row}",
    )(x2, mod3, gain.reshape(1, D_MODEL), w_in, w_in, w_out)


_PROJ_Q, _PROJ_K, _PROJ_V = 0, SB_W, 2 * SB_W
_PROJ_DN = 3 * SB_W
_PROJ_Z = _PROJ_DN + 3 * DN_W
_PROJ_MAIN_W = _PROJ_Z + DN_W


def _proj_kernel(h_ref, mod_ref, g_ref, wm_ref, wab_ref, wabt_ref, gq_ref, gk_ref, grp_ref, wc_ref,
                 q_ref, k_ref, v_ref, dn_ref, z_ref, ab_ref, abt_ref, u_ref, xbuf_ref, *, tiles_per_seq):
    tm = TOKEN_TILE
    hist = SUBLANES
    u_ref[...] = _rms_mod(h_ref[...], g_ref[...], mod_ref[0, 3:4, :], mod_ref[0, 4:5, :]).astype(BF16)

    @pl.when(pl.program_id(0) % tiles_per_seq == 0)
    def _():
        xbuf_ref[0:hist, :] = jnp.zeros((hist, 3 * DN_W), F32)

    def head_norm(t, gain_row):
        hi, lo = _split_bf16(t * t, 2)
        ss = _dot(hi, grp_ref[...]) + _dot(lo, grp_ref[...])
        return t * lax.rsqrt(ss * (1.0 / SB_HEAD_DIM) + EPS) * gain_row

    def conv_group(g):
        cols = slice(g * DN_DIM, (g + 1) * DN_DIM)
        acc = xbuf_ref[hist:hist + tm, cols] * wc_ref[DN_CONV - 1:DN_CONV, cols]
        for tap in range(DN_CONV - 1):
            lo = hist - (DN_CONV - 1) + tap
            acc = acc + xbuf_ref[lo:lo + tm, cols] * wc_ref[tap:tap + 1, cols]
        y = acc * _sigmoid(acc)
        if g < 2 * DN_HEADS:
            y = y * lax.rsqrt(jnp.sum(y * y, axis=-1, keepdims=True) + EPS)
        if g < DN_HEADS:
            y = y * (DN_DIM ** -0.5)
        dn_ref[:, cols] = y

    q = _dot(u_ref[...], wm_ref[:, _PROJ_Q:_PROJ_Q + SB_W])
    q_ref[...] = (head_norm(q, gq_ref[...]) * (SB_HEAD_DIM ** -0.5)).astype(BF16)
    k = _dot(u_ref[...], wm_ref[:, _PROJ_K:_PROJ_K + SB_W])
    k_ref[...] = head_norm(k, gk_ref[...]).astype(BF16)
    v_ref[...] = _dot(u_ref[...], wm_ref[:, _PROJ_V:_PROJ_V + SB_W]).astype(BF16)
    for c in range(3 * DN_W // MXU_DIM):
        cols = slice(c * MXU_DIM, (c + 1) * MXU_DIM)
        xbuf_ref[hist:hist + tm, cols] = _dot(
            u_ref[...], wm_ref[:, _PROJ_DN + c * MXU_DIM:_PROJ_DN + (c + 1) * MXU_DIM])
    for g in range(3 * DN_HEADS):
        conv_group(g)
    xbuf_ref[0:hist, :] = xbuf_ref[tm:tm + hist, :]
    z_ref[...] = _dot(u_ref[...], wm_ref[:, _PROJ_Z:_PROJ_Z + DN_W])
    ab_ref[...] = _dot(u_ref[...], wab_ref[...])
    abt_ref[0] = _dot_nt(wabt_ref[...], u_ref[...])


def _projection(h2, mod3, g_mix, w_main, w_ab, w_abt, gq_row, gk_row, w_conv, seq):
    tokens = h2.shape[0]
    bsz = tokens // seq
    tm = TOKEN_TILE
    tiles_per_seq = seq // tm
    grp = jnp.kron(jnp.eye(SB_HEADS, dtype=F32), jnp.ones((SB_HEAD_DIM, SB_HEAD_DIM), F32)).astype(BF16)
    row = lambda w: pl.BlockSpec((tm, w), lambda i: (i, 0))
    return pl.pallas_call(
        functools.partial(_proj_kernel, tiles_per_seq=tiles_per_seq),
        grid=(tokens // tm,),
        in_specs=[row(D_MODEL),
                  pl.BlockSpec((1, N_MOD, D_MODEL), lambda i: (i // tiles_per_seq, 0, 0)),
                  _resident((1, D_MODEL), lambda i: (0, 0)),
                  _resident((D_MODEL, _PROJ_MAIN_W), lambda i: (0, 0)),
                  _resident((D_MODEL, LANES), lambda i: (0, 0)),
                  _resident((2 * SUBLANES, D_MODEL), lambda i: (0, 0)),
                  _resident((1, SB_W), lambda i: (0, 0)),
                  _resident((1, SB_W), lambda i: (0, 0)),
                  _resident((SB_W, SB_W), lambda i: (0, 0)),
                  _resident((DN_CONV, 3 * DN_W), lambda i: (0, 0))],
        out_specs=[row(SB_W), row(SB_W), row(SB_W), row(3 * DN_W), row(DN_W), row(LANES),
                   pl.BlockSpec((1, 2 * SUBLANES, tm), lambda i: (i // tiles_per_seq, 0, i % tiles_per_seq))],
        out_shape=[jax.ShapeDtypeStruct((tokens, SB_W), BF16),
                   jax.ShapeDtypeStruct((tokens, SB_W), BF16),
                   jax.ShapeDtypeStruct((tokens, SB_W), BF16),
                   jax.ShapeDtypeStruct((tokens, 3 * DN_W), F32),
                   jax.ShapeDtypeStruct((tokens, DN_W), F32),
                   jax.ShapeDtypeStruct((tokens, LANES), F32),
                   jax.ShapeDtypeStruct((bsz, 2 * SUBLANES, seq), F32)],
        scratch_shapes=[pltpu.VMEM((tm, D_MODEL), BF16), pltpu.VMEM((tm + SUBLANES, 3 * DN_W), F32)],
        compiler_params=_params("arbitrary"),
        name="mixer_in_proj",
    )(h2, mod3, g_mix.reshape(1, D_MODEL), w_main, w_ab, w_abt, gq_row, gk_row, grp, w_conv)


def _sb_kernel(q_ref, k_ref, v_ref, o_ref, acc_ref, carry_ref, lb_ref, lk_ref, q2_ref):
    t = SB_TILE
    head_a = lax.broadcasted_iota(jnp.int32, (t, LANES), 1) < SB_HEAD_DIM
    suffix = jnp.where(lax.broadcasted_iota(jnp.int32, (t, t), 0) > lax.broadcasted_iota(jnp.int32, (t, t), 1),
                       1.0, 0.0).astype(BF16)
    causal = (lax.broadcasted_iota(jnp.int32, (2 * t, t), 1)
              < (lax.broadcasted_iota(jnp.int32, (2 * t, t), 0) & (t - 1)))

    def keys(kb):
        return pl.ds(kb * t, t) if isinstance(kb, int) else pl.ds(pl.multiple_of(kb * t, t), t)

    def logits(i, kb, slot, diagonal=False):
        k = k_ref[0, keys(kb), :]
        z = _dot_nt(q2_ref[i], k)
        log_beta = jnp.minimum(z, 0.0) - jnp.log(1.0 + jnp.exp(-jnp.abs(z)))
        log_keep = log_beta - z
        if diagonal:
            log_beta = jnp.where(causal, log_beta, MASKED)
            log_keep = jnp.where(causal, log_keep, 0.0)
        lb_ref[slot] = log_beta
        lk_ref[slot] = log_keep.astype(BF16)

    def weigh(i, kb, slot):
        v = v_ref[0, keys(kb), :]
        log_keep = lk_ref[slot]
        sums = _dot(log_keep, suffix)
        w = jnp.exp(lb_ref[slot] + sums + carry_ref[i])
        carry_ref[i] = carry_ref[i] + (sums[:, 0:1] + log_keep[:, 0:1].astype(F32))
        zero_v = jnp.zeros_like(v)
        v2 = jnp.concatenate([jnp.where(head_a, v, zero_v), jnp.where(head_a, zero_v, v)], axis=0)
        w2 = jnp.concatenate([w[:t], w[t:]], axis=1).astype(BF16)
        acc_ref[i] += _dot(w2, v2)

    tiles = q_ref.shape[1] // t
    straight = 3
    runs = [min(i + 1, straight) for i in range(tiles)]

    for i in range(tiles):
        q = q_ref[0, i * t:(i + 1) * t, :]
        zero_q = jnp.zeros_like(q)
        q2_ref[i] = jnp.concatenate([jnp.where(head_a, q, zero_q), jnp.where(head_a, zero_q, q)], axis=0)
    acc_ref[...] = jnp.zeros_like(acc_ref)
    carry_ref[...] = jnp.zeros_like(carry_ref)

    logits(0, 0, 0, diagonal=True)
    g = 0
    for i in range(tiles):
        for d in range(runs[i]):
            weigh(i, i - d, g % 2)
            if d + 1 < runs[i]:
                logits(i, i - d - 1, (g + 1) % 2)
            elif i + 1 < tiles:
                logits(i + 1, i + 1, (g + 1) % 2, diagonal=True)
            g += 1

    for i in range(tiles):
        if i >= runs[i]:
            def alive(kb, i=i):
                return jnp.logical_and(kb >= 0, jnp.max(carry_ref[i]) > SB_DEAD).astype(jnp.int32)

            def step(state, i=i, alive=alive):
                kb, _ = state
                logits(i, kb, 0)
                weigh(i, kb, 0)
                return kb - 1, alive(kb - 1)

            first = jnp.int32(i - runs[i])
            lax.while_loop(lambda s: s[1] > 0, step, (first, alive(first)))
        o_ref[0, i * t:(i + 1) * t, :] = acc_ref[i].astype(o_ref.dtype)


def _sb_attention(q, k, v):
    bsz, seq, _ = q.shape
    t = SB_TILE
    pairs = SB_W // LANES
    whole = pl.BlockSpec((1, seq, LANES), lambda b, p: (b, 0, p))
    return pl.pallas_call(
        _sb_kernel,
        grid=(bsz, pairs),
        in_specs=[whole, whole, whole],
        out_specs=whole,
        out_shape=jax.ShapeDtypeStruct((bsz, seq, SB_W), BF16),
        scratch_shapes=[pltpu.VMEM((seq // t, t, LANES), F32), pltpu.VMEM((seq // t, 2 * t, 1), F32),
                        pltpu.VMEM((2, 2 * t, t), F32), pltpu.VMEM((2, 2 * t, t), BF16),
                        pltpu.VMEM((seq // t, 2 * t, LANES), BF16)],
        compiler_params=_params("parallel", "parallel"),
        name="stickbreak_attn",
    )(q, k, v)


def _gdn_kernel(qkv_ref, z_ref, ab_ref, abt_ref, prow_ref, pcol_ref, gout_ref, o_ref,
                state_ref, lhs_ref, add_ref, gam_ref, *, steps_per_seq):
    ts = DN_STEP
    chunks = ts // DN_CHUNK
    j = pl.program_id(0)

    @pl.when(j == 0)
    def _():
        lhs_ref[...] = jnp.zeros_like(lhs_ref)
        add_ref[...] = jnp.zeros_like(add_ref)
        gam_ref[...] = jnp.zeros_like(gam_ref)

    @pl.when(jnp.logical_or(j == 0, (j + steps_per_seq - 1) % steps_per_seq == 0))
    def _():
        state_ref[...] = jnp.zeros_like(state_ref)

    _gdn_scan(z_ref, gout_ref, o_ref, state_ref, lhs_ref, add_ref, gam_ref)

    tb = 2 * DN_CHUNK
    ri = lax.broadcasted_iota(jnp.int32, (tb, tb), 0)
    ci = lax.broadcasted_iota(jnp.int32, (tb, tb), 1)
    same_chunk = (ri >> 6) == (ci >> 6)
    m_incl = same_chunk & (ci <= ri)
    m_strict = same_chunk & (ci < ri)
    eye = jnp.where(ri == ci, 1.0, 0.0)
    tri = jnp.where(m_incl, 1.0, 0.0).astype(BF16)
    tri_t = jnp.where(same_chunk & (ri <= ci), 1.0, 0.0).astype(BF16)
    units = [(h, r) for h in range(DN_HEADS) for r in range(ts // tb)]
    rows_of = lambda r: slice(r * tb, (r + 1) * tb)

    ab = ab_ref[...]
    beta_all = _sigmoid(ab)
    g_all = -jnp.exp(prow_ref[0:1, :]) * _softplus(ab + prow_ref[1:2, :])
    abt = abt_ref[0]
    g_rows = -jnp.exp(pcol_ref[:, 0:1]) * _softplus(abt + pcol_ref[:, 1:2])
    g_all3 = _split_bf16(g_all, 3)
    g_rows3 = _split_bf16(g_rows, 3)
    gcum_all = [sum(_dot(tri, p[rows_of(r)]) for p in g_all3) for r in range(ts // tb)]
    gcum_rows = [sum(_dot(p[:, rows_of(r)], tri_t) for p in g_rows3) for r in range(ts // tb)]

    def col(h, part):
        return slice(part * DN_W + h * DN_DIM, part * DN_W + (h + 1) * DN_DIM)

    q = {u: qkv_ref[rows_of(u[1]), col(u[0], 0)] for u in units}
    k = {u: qkv_ref[rows_of(u[1]), col(u[0], 1)] for u in units}
    v = {u: qkv_ref[rows_of(u[1]), col(u[0], 2)] for u in units}
    beta = {(h, r): beta_all[rows_of(r), h:h + 1] for h, r in units}
    gc = {(h, r): gcum_all[r][:, DN_HEADS + h:DN_HEADS + h + 1] for h, r in units}
    gr = {(h, r): gcum_rows[r][DN_HEADS + h:DN_HEADS + h + 1, :] for h, r in units}
    decay = {u: jnp.where(m_incl, jnp.exp(jnp.where(m_incl, gc[u] - gr[u], 0.0)), 0.0) for u in units}
    kb = {u: k[u] * beta[u] for u in units}
    vb = {u: v[u] * beta[u] for u in units}
    eg = {u: jnp.exp(gc[u]) for u in units}
    qg = {u: q[u] * eg[u] for u in units}
    kq = {u: _dot_nt(jnp.concatenate([kb[u], q[u]], axis=0).astype(BF16), k[u].astype(BF16)) for u in units}
    lower = {u: jnp.where(m_strict, kq[u][:tb] * decay[u], 0.0) for u in units}
    a_qk = {u: (kq[u][tb:] * decay[u]).astype(BF16) for u in units}

    blk8 = (ri >> 3) == (ci >> 3)
    m8 = {u: jnp.where(blk8, -lower[u], 0.0).astype(BF16) for u in units}
    inv = {u: eye + m8[u].astype(F32) for u in units}
    p = {u: _dot(m8[u], m8[u]).astype(BF16) for u in units}
    inv = {u: inv[u] + _dot(inv[u].astype(BF16), p[u]) for u in units}
    p = {u: _dot(p[u], p[u]).astype(BF16) for u in units}
    inv = {u: inv[u] + _dot(inv[u].astype(BF16), p[u]) for u in units}
    for lg in (3, 4, 5):
        off = ((ri >> (lg + 1)) == (ci >> (lg + 1))) & (((ri >> lg) & 1) == 1) & (((ci >> lg) & 1) == 0)
        inv16 = {u: inv[u].astype(BF16) for u in units}
        cb = {u: _dot(jnp.where(off, lower[u], 0.0).astype(BF16), inv16[u]).astype(BF16) for u in units}
        inv = {u: inv[u] - _dot(inv16[u], cb[u]) for u in units}
    sol = {u: _dot(inv[u].astype(BF16), jnp.concatenate([vb[u], kb[u] * eg[u]], axis=1).astype(BF16))
           for u in units}

    a_sol = {u: _dot(a_qk[u], sol[u].astype(BF16)) for u in units}
    for u in units:
        h, r = u
        q_eff = (qg[u] - a_sol[u][:, DN_DIM:]).astype(BF16)
        for cc in range(tb // DN_CHUNK):
            c = r * (tb // DN_CHUNK) + cc
            rows = slice(cc * DN_CHUNK, (cc + 1) * DN_CHUNK)
            g_last = gc[u][(cc + 1) * DN_CHUNK - 1:(cc + 1) * DN_CHUNK, :]
            k_dec = k[u][rows] * jnp.exp(g_last - gc[u][rows])
            kd_sol = _dot_tn(k_dec.astype(BF16), sol[u][rows].astype(BF16))
            lhs_ref[h, c, 0:DN_CHUNK, :] = q_eff[rows]
            lhs_ref[h, c, DN_CHUNK:, :] = kd_sol[:, DN_DIM:].astype(BF16)
            add_ref[h, c, 0:DN_CHUNK, :] = a_sol[u][rows, :DN_DIM]
            add_ref[h, c, DN_CHUNK:, :] = kd_sol[:, :DN_DIM]
            gam_ref[h * chunks + c:h * chunks + c + 1, :] = jnp.broadcast_to(jnp.exp(g_last), (1, LANES))


def _gdn_scan(z_ref, gout_ref, o_ref, state_ref, lhs_ref, add_ref, gam_ref):
    chunks = DN_STEP // DN_CHUNK
    out = [[] for _ in range(DN_HEADS)]
    for c in range(chunks):
        for h in range(DN_HEADS):
            state = state_ref[h]
            res = _dot(lhs_ref[h, c], state.astype(BF16))
            add = add_ref[h, c]
            out[h].append(res[:DN_CHUNK] + add[:DN_CHUNK])
            gamma = gam_ref[h * chunks + c:h * chunks + c + 1, :]
            state_ref[h] = state * gamma - res[DN_CHUNK:] + add[DN_CHUNK:]
    for h in range(DN_HEADS):
        o = jnp.concatenate(out[h], axis=0)
        o = o * lax.rsqrt(jnp.mean(o * o, axis=-1, keepdims=True) + EPS) * gout_ref[...]
        zh = z_ref[:, h * DN_DIM:(h + 1) * DN_DIM]
        o_ref[:, h * DN_DIM:(h + 1) * DN_DIM] = (o * (zh * _sigmoid(zh))).astype(o_ref.dtype)


def _gated_deltanet(qkv, z, ab, abt, prow, pcol, g_out, seq):
    tokens = qkv.shape[0]
    ts = DN_STEP
    steps = seq // ts
    total = tokens // ts
    chunks = ts // DN_CHUNK
    fill_step = lambda j: jnp.minimum(j, total - 1)
    drain_step = lambda j: jnp.maximum(j - 1, 0)
    return pl.pallas_call(
        functools.partial(_gdn_kernel, steps_per_seq=steps),
        grid=(total + 1,),
        in_specs=[pl.BlockSpec((ts, 3 * DN_W), lambda j: (fill_step(j), 0)),
                  pl.BlockSpec((ts, DN_W), lambda j: (drain_step(j), 0)),
                  pl.BlockSpec((ts, LANES), lambda j: (fill_step(j), 0)),
                  pl.BlockSpec((1, 2 * SUBLANES, ts), lambda j: (fill_step(j) // steps, 0, fill_step(j) % steps)),
                  pl.BlockSpec((2, LANES), lambda j: (0, 0)),
                  pl.BlockSpec((2 * SUBLANES, LANES), lambda j: (0, 0)),
                  pl.BlockSpec((1, DN_DIM), lambda j: (0, 0))],
        out_specs=pl.BlockSpec((ts, DN_W), lambda j: (drain_step(j), 0)),
        out_shape=jax.ShapeDtypeStruct((tokens, DN_W), BF16),
        scratch_shapes=[pltpu.VMEM((DN_HEADS, DN_DIM, DN_DIM), F32),
                        pltpu.VMEM((DN_HEADS, chunks, DN_CHUNK + DN_DIM, DN_DIM), BF16),
                        pltpu.VMEM((DN_HEADS, chunks, DN_CHUNK + DN_DIM, DN_DIM), F32),
                        pltpu.VMEM((DN_HEADS * chunks, LANES), F32)],
        compiler_params=_params("arbitrary"),
        name="gated_deltanet",
    )(qkv, z, ab, abt, prow, pcol, g_out.reshape(1, DN_DIM))


def _merge_kernel(h_ref, attn_ref, dn_ref, mod_ref, g_ref, wr_ref, wus_ref, wud_ref, wo_ref, out_ref,
                  u_ref, m_ref):
    u_ref[...] = _rms_mod(h_ref[...], g_ref[...], mod_ref[0, 3:4, :], mod_ref[0, 4:5, :]).astype(BF16)
    for n in range(D_MODEL // MXU_DIM):
        cols = slice(n * MXU_DIM, (n + 1) * MXU_DIM)
        r_sb = _dot(u_ref[...], wr_ref[:, cols])
        r_dn = _dot(u_ref[...], wr_ref[:, D_MODEL + n * MXU_DIM:D_MODEL + (n + 1) * MXU_DIM])
        y_sb = _dot(attn_ref[...], wus_ref[:, cols])
        y_dn = _dot(dn_ref[...], wud_ref[:, cols])
        m_ref[:, cols] = (_sigmoid(r_sb) * y_sb + _sigmoid(r_dn) * y_dn).astype(BF16)
    for n in range(D_MODEL // MXU_DIM):
        cols = slice(n * MXU_DIM, (n + 1) * MXU_DIM)
        out_ref[:, cols] = h_ref[:, cols] + mod_ref[0, 5:6, cols] * _dot(m_ref[...], wo_ref[:, cols])


def _merge(h2, attn, dn, mod3, g_mix, w_r, w_up_sb, w_up_dn, w_out, seq):
    tokens = h2.shape[0]
    tm = TOKEN_TILE
    tiles_per_seq = seq // tm
    row = lambda w: pl.BlockSpec((tm, w), lambda i: (i, 0))
    return pl.pallas_call(
        _merge_kernel,
        grid=(tokens // tm,),
        in_specs=[row(D_MODEL), row(SB_W), row(DN_W),
                  pl.BlockSpec((1, N_MOD, D_MODEL), lambda i: (i // tiles_per_seq, 0, 0)),
                  _resident((1, D_MODEL), lambda i: (0, 0)),
                  _resident((D_MODEL, 2 * D_MODEL), lambda i: (0, 0)),
                  _resident((SB_W, D_MODEL), lambda i: (0, 0)),
                  _resident((DN_W, D_MODEL), lambda i: (0, 0)),
                  _resident((D_MODEL, D_MODEL), lambda i: (0, 0))],
        out_specs=row(D_MODEL),
        out_shape=jax.ShapeDtypeStruct((tokens, D_MODEL), F32),
        scratch_shapes=[pltpu.VMEM((tm, D_MODEL), BF16), pltpu.VMEM((tm, D_MODEL), BF16)],
        compiler_params=_params("parallel"),
        name="mixer_merge",
    )(h2, attn, dn, mod3, g_mix.reshape(1, D_MODEL), w_r, w_up_sb, w_up_dn, w_out)


def _layer(h2, mod3, seq, g_ffn1, w_ffn1_in, w_ffn1_out, g_mix, w_in, g_q_sb, g_k_sb, w_conv, a_log,
           dt_bias, g_dn_out, w_up_sb, w_up_dn, w_out, g_ffn2, w_ffn2_in, w_ffn2_out):
    tokens = h2.shape[0]
    bsz = tokens // seq
    h2 = _ffn(h2, mod3, g_ffn1, w_ffn1_in.astype(BF16), w_ffn1_out.astype(BF16), 0, seq)

    gate0 = _PROJ_MAIN_W
    gate1 = gate0 + 2 * DN_HEADS
    w_main = w_in[:, :gate0].astype(BF16)
    w_gate = w_in[:, gate0:gate1]
    w_ab = jnp.pad(w_gate, ((0, 0), (0, LANES - 2 * DN_HEADS))).astype(BF16)
    w_abt = jnp.pad(w_gate.T, ((0, 2 * SUBLANES - 2 * DN_HEADS), (0, 0))).astype(BF16)
    w_r = w_in[:, gate1:].astype(BF16)
    gq_row = jnp.tile(g_q_sb, SB_HEADS).reshape(1, SB_W)
    gk_row = jnp.tile(g_k_sb, SB_HEADS).reshape(1, SB_W)
    q, k, v, qkv_dn, z_dn, ab, abt = _projection(h2, mod3, g_mix, w_main, w_ab, w_abt, gq_row, gk_row,
                                                 w_conv, seq)

    attn = _sb_attention(q.reshape(bsz, seq, SB_W), k.reshape(bsz, seq, SB_W), v.reshape(bsz, seq, SB_W))

    lanes = slice(DN_HEADS, 2 * DN_HEADS)
    prow = jnp.zeros((2, LANES), F32).at[0, lanes].set(a_log).at[1, lanes].set(dt_bias)
    pcol = jnp.zeros((2 * SUBLANES, LANES), F32).at[lanes, 0].set(a_log).at[lanes, 1].set(dt_bias)
    dn = _gated_deltanet(qkv_dn, z_dn, ab, abt, prow, pcol, g_dn_out, seq)

    h2 = _merge(h2, attn.reshape(tokens, SB_W), dn, mod3, g_mix, w_r,
                w_up_sb.astype(BF16), w_up_dn.astype(BF16), w_out.astype(BF16), seq)
    return _ffn(h2, mod3, g_ffn2, w_ffn2_in.astype(BF16), w_ffn2_out.astype(BF16), 6, seq)


def kernel(x, c, w_ada, b_ada, g_ffn1, w_ffn1_in, w_ffn1_out, g_mix, w_in, g_q_sb, g_k_sb, w_conv, a_log,
           dt_bias, g_dn_out, w_up_sb, w_up_dn, w_out, g_ffn2, w_ffn2_in, w_ffn2_out):
    bsz, seq, d = x.shape
    depth = w_ada.shape[0]
    h2 = x.reshape(bsz * seq, d)
    for l in range(depth):
        mod3 = _modulation(c, w_ada[l], b_ada[l]).reshape(bsz, N_MOD, d)
        h2 = _layer(h2, mod3, seq, g_ffn1[l], w_ffn1_in[l], w_ffn1_out[l], g_mix[l], w_in[l], g_q_sb[l],
                    g_k_sb[l], w_conv[l], a_log[l], dt_bias[l], g_dn_out[l], w_up_sb[l], w_up_dn[l],
                    w_out[l], g_ffn2[l], w_ffn2_in[l], w_ffn2_out[l])
    return h2.reshape(bsz, seq, d).astype(x.dtype)
```

```python
import functools

import jax
import jax.numpy as jnp
from jax import lax
from jax.experimental import pallas as pl
from jax.experimental.pallas import tpu as pltpu

F32 = jnp.float32
BF16 = jnp.bfloat16

D_MODEL = 1024
D_FF = 2816
N_MOD = 9
EPS = 1e-6

SB_HEADS = 8
SB_HEAD_DIM = 64
SB_W = SB_HEADS * SB_HEAD_DIM
DN_HEADS = 4
DN_DIM = 128
DN_W = DN_HEADS * DN_DIM
DN_CONV = 4
DN_CHUNK = 64

LANES = 128
SUBLANES = 8
MXU_DIM = 256
VMEM_LIMIT = 56 * 1024 * 1024

TOKEN_TILE = 1024
FFN_TILE = 512
FF_CHUNK = MXU_DIM
SB_TILE = MXU_DIM
MASKED = -1e30
SB_DEAD = -104.0
DN_STEP = 8 * DN_CHUNK

_dot = functools.partial(jnp.dot, preferred_element_type=F32)


def _dot_nt(a, b):
    return lax.dot_general(a, b, (((1,), (1,)), ((), ())), preferred_element_type=F32)


def _dot_tn(a, b):
    return lax.dot_general(a, b, (((0,), (0,)), ((), ())), preferred_element_type=F32)


def _sigmoid(x):
    return 1.0 / (1.0 + jnp.exp(-x))


def _softplus(x):
    return jnp.maximum(x, 0.0) + jnp.log(1.0 + jnp.exp(-jnp.abs(x)))


def _split_bf16(x, pieces):
    out = []
    for _ in range(pieces - 1):
        p = x.astype(BF16)
        out.append(p)
        x = x - p.astype(F32)
    out.append(x.astype(BF16))
    return out


def _rms_mod(x, gain, shift, scale):
    y = x * lax.rsqrt(jnp.mean(x * x, axis=-1, keepdims=True) + EPS)
    return (y * gain) * (1.0 + scale) + shift


def _resident(shape, index_map):
    return pl.BlockSpec(shape, index_map, pipeline_mode=pl.Buffered(1))


def _params(*semantics):
    return pltpu.CompilerParams(dimension_semantics=semantics, vmem_limit_bytes=VMEM_LIMIT)


def _mod_kernel(c_ref, w_ref, b_ref, o_ref):
    c = c_ref[...]
    cond = (c * _sigmoid(c)).astype(BF16)
    o_ref[...] = _dot(cond, w_ref[...].astype(BF16)) + b_ref[...]


def _modulation(c, w_ada, b_ada):
    bsz = c.shape[0]
    n = w_ada.shape[1]
    tn = D_MODEL
    return pl.pallas_call(
        _mod_kernel,
        grid=(n // tn,),
        in_specs=[pl.BlockSpec((bsz, D_MODEL), lambda j: (0, 0)),
                  pl.BlockSpec((D_MODEL, tn), lambda j: (0, j)),
                  pl.BlockSpec((1, tn), lambda j: (0, j))],
        out_specs=pl.BlockSpec((bsz, tn), lambda j: (0, j)),
        out_shape=jax.ShapeDtypeStruct((bsz, n), F32),
        compiler_params=_params("arbitrary"),
        name="adaln_mod",
    )(c, w_ada, b_ada.reshape(1, n))


def _ffn_kernel(x_ref, mod_ref, g_ref, wa_ref, wb_ref, wo_ref, out_ref, u_ref, h_ref, *, mod_row):
    shift = mod_ref[0, mod_row:mod_row + 1, :]
    scale = mod_ref[0, mod_row + 1:mod_row + 2, :]
    u_ref[...] = _rms_mod(x_ref[...], g_ref[...], shift, scale).astype(BF16)
    for c in range(D_FF // FF_CHUNK):
        cols = slice(c * FF_CHUNK, (c + 1) * FF_CHUNK)
        a = _dot(u_ref[...], wa_ref[:, cols].astype(BF16))
        b = _dot(u_ref[...], wb_ref[:, cols].astype(BF16))
        h_ref[:, cols] = (a * _sigmoid(a) * b).astype(BF16)
    for n in range(D_MODEL // MXU_DIM):
        cols = slice(n * MXU_DIM, (n + 1) * MXU_DIM)
        gate = mod_ref[0, mod_row + 2:mod_row + 3, cols]
        out_ref[:, cols] = x_ref[:, cols] + (0.5 * gate) * _dot(h_ref[...], wo_ref[:, cols].astype(BF16))


def _ffn(x2, mod3, gain, w_in, w_out, mod_row, seq):
    tokens = x2.shape[0]
    tm = FFN_TILE
    tiles_per_seq = seq // tm
    return pl.pallas_call(
        functools.partial(_ffn_kernel, mod_row=mod_row),
        grid=(tokens // tm,),
        in_specs=[pl.BlockSpec((tm, D_MODEL), lambda i: (i, 0)),
                  pl.BlockSpec((1, N_MOD, D_MODEL), lambda i: (i // tiles_per_seq, 0, 0)),
                  _resident((1, D_MODEL), lambda i: (0, 0)),
                  _resident((D_MODEL, D_FF), lambda i: (0, 0)),
                  _resident((D_MODEL, D_FF), lambda i: (0, 1)),
                  _resident((D_FF, D_MODEL), lambda i: (0, 0))],
        out_specs=pl.BlockSpec((tm, D_MODEL), lambda i: (i, 0)),
        out_shape=jax.ShapeDtypeStruct((tokens, D_MODEL), F32),
        scratch_shapes=[pltpu.VMEM((tm, D_MODEL), BF16), pltpu.VMEM((tm, D_FF), BF16)],
        compiler_params=_params("parallel"),
        name=f"ffn_row{mod_row}",
    )(x2, mod3, gain.reshape(1, D_MODEL), w_in, w_in, w_out)


_PROJ_Q, _PROJ_K, _PROJ_V = 0, SB_W, 2 * SB_W
_PROJ_DN = 3 * SB_W
_PROJ_Z = _PROJ_DN + 3 * DN_W
_PROJ_MAIN_W = _PROJ_Z + DN_W
_PROJ_GATE = _PROJ_MAIN_W
_PROJ_R = _PROJ_GATE + 2 * DN_HEADS
_PROJ_W = _PROJ_R + 2 * D_MODEL


def _proj_kernel(h_ref, mod_ref, g_ref, wm_ref, wabt_ref, gq_ref, gk_ref, grp_ref, wc_ref,
                 q_ref, k_ref, v_ref, dn_ref, z_ref, ab_ref, abt_ref, u_ref, xbuf_ref, *, tiles_per_seq):
    tm = TOKEN_TILE
    hist = SUBLANES
    u_ref[...] = _rms_mod(h_ref[...], g_ref[...], mod_ref[0, 3:4, :], mod_ref[0, 4:5, :]).astype(BF16)

    @pl.when(pl.program_id(0) % tiles_per_seq == 0)
    def _():
        xbuf_ref[0:hist, :] = jnp.zeros((hist, 3 * DN_W), F32)

    def head_norm(t, gain_row):
        ss = _dot((t * t).astype(BF16), grp_ref[...])
        return t * lax.rsqrt(ss * (1.0 / SB_HEAD_DIM) + EPS) * gain_row

    def conv_group(g):
        cols = slice(g * DN_DIM, (g + 1) * DN_DIM)
        acc = xbuf_ref[hist:hist + tm, cols] * wc_ref[DN_CONV - 1:DN_CONV, cols]
        for tap in range(DN_CONV - 1):
            lo = hist - (DN_CONV - 1) + tap
            acc = acc + xbuf_ref[lo:lo + tm, cols] * wc_ref[tap:tap + 1, cols]
        y = acc * _sigmoid(acc)
        if g < 2 * DN_HEADS:
            y = y * lax.rsqrt(jnp.sum(y * y, axis=-1, keepdims=True) + EPS)
        if g < DN_HEADS:
            y = y * (DN_DIM ** -0.5)
        dn_ref[:, cols] = y

    q = _dot(u_ref[...], wm_ref[:, _PROJ_Q:_PROJ_Q + SB_W])
    q_ref[...] = (head_norm(q, gq_ref[...]) * (SB_HEAD_DIM ** -0.5)).astype(BF16)
    k = _dot(u_ref[...], wm_ref[:, _PROJ_K:_PROJ_K + SB_W])
    k_ref[...] = head_norm(k, gk_ref[...]).astype(BF16)
    v_ref[...] = _dot(u_ref[...], wm_ref[:, _PROJ_V:_PROJ_V + SB_W]).astype(BF16)
    for c in range(3 * DN_W // MXU_DIM):
        cols = slice(c * MXU_DIM, (c + 1) * MXU_DIM)
        xbuf_ref[hist:hist + tm, cols] = _dot(
            u_ref[...], wm_ref[:, _PROJ_DN + c * MXU_DIM:_PROJ_DN + (c + 1) * MXU_DIM])
    for g in range(3 * DN_HEADS):
        conv_group(g)
    xbuf_ref[0:hist, :] = xbuf_ref[tm:tm + hist, :]
    z_ref[...] = _dot(u_ref[...], wm_ref[:, _PROJ_Z:_PROJ_Z + DN_W])
    gates = wm_ref[:, _PROJ_GATE:_PROJ_R]
    gates = jnp.concatenate([gates, jnp.zeros((D_MODEL, LANES - 2 * DN_HEADS), BF16)], axis=1)
    ab_ref[...] = _dot(u_ref[...], gates)
    abt_ref[0] = _dot_nt(wabt_ref[...], u_ref[...])


def _projection(h2, mod3, g_mix, w_pack, w_abt, gq_row, gk_row, w_conv, seq):
    tokens = h2.shape[0]
    bsz = tokens // seq
    tm = TOKEN_TILE
    tiles_per_seq = seq // tm
    grp = jnp.kron(jnp.eye(SB_HEADS, dtype=F32), jnp.ones((SB_HEAD_DIM, SB_HEAD_DIM), F32)).astype(BF16)
    row = lambda w: pl.BlockSpec((tm, w), lambda i: (i, 0))
    return pl.pallas_call(
        functools.partial(_proj_kernel, tiles_per_seq=tiles_per_seq),
        grid=(tokens // tm,),
        in_specs=[row(D_MODEL),
                  pl.BlockSpec((1, N_MOD, D_MODEL), lambda i: (i // tiles_per_seq, 0, 0)),
                  _resident((1, D_MODEL), lambda i: (0, 0)),
                  _resident((D_MODEL, _PROJ_W), lambda i: (0, 0)),
                  _resident((2 * SUBLANES, D_MODEL), lambda i: (0, 0)),
                  _resident((1, SB_W), lambda i: (0, 0)),
                  _resident((1, SB_W), lambda i: (0, 0)),
                  _resident((SB_W, SB_W), lambda i: (0, 0)),
                  _resident((DN_CONV, 3 * DN_W), lambda i: (0, 0))],
        out_specs=[row(SB_W), row(SB_W), row(SB_W), row(3 * DN_W), row(DN_W), row(LANES),
                   pl.BlockSpec((1, 2 * SUBLANES, tm), lambda i: (i // tiles_per_seq, 0, i % tiles_per_seq))],
        out_shape=[jax.ShapeDtypeStruct((tokens, SB_W), BF16),
                   jax.ShapeDtypeStruct((tokens, SB_W), BF16),
                   jax.ShapeDtypeStruct((tokens, SB_W), BF16),
                   jax.ShapeDtypeStruct((tokens, 3 * DN_W), F32),
                   jax.ShapeDtypeStruct((tokens, DN_W), F32),
                   jax.ShapeDtypeStruct((tokens, LANES), F32),
                   jax.ShapeDtypeStruct((bsz, 2 * SUBLANES, seq), F32)],
        scratch_shapes=[pltpu.VMEM((tm, D_MODEL), BF16), pltpu.VMEM((tm + SUBLANES, 3 * DN_W), F32)],
        compiler_params=_params("arbitrary"),
        name="mixer_in_proj",
    )(h2, mod3, g_mix.reshape(1, D_MODEL), w_pack, w_abt, gq_row, gk_row, grp, w_conv)


def _sb_kernel(q_ref, k_ref, v_ref, o_ref, acc_ref, carry_ref, lb_ref, lk_ref, q2_ref):
    t = SB_TILE
    head_a = lax.broadcasted_iota(jnp.int32, (t, LANES), 1) < SB_HEAD_DIM
    suffix = jnp.where(lax.broadcasted_iota(jnp.int32, (t, t), 0) > lax.broadcasted_iota(jnp.int32, (t, t), 1),
                       1.0, 0.0).astype(BF16)
    causal = (lax.broadcasted_iota(jnp.int32, (2 * t, t), 1)
              < (lax.broadcasted_iota(jnp.int32, (2 * t, t), 0) & (t - 1)))

    def keys(kb):
        return pl.ds(kb * t, t) if isinstance(kb, int) else pl.ds(pl.multiple_of(kb * t, t), t)

    def logits(i, kb, slot, diagonal=False):
        k = k_ref[0, keys(kb), :]
        z = _dot_nt(q2_ref[i], k)
        log_beta = jnp.minimum(z, 0.0) - jnp.log(1.0 + jnp.exp(-jnp.abs(z)))
        log_keep = log_beta - z
        if diagonal:
            log_beta = jnp.where(causal, log_beta, MASKED)
            log_keep = jnp.where(causal, log_keep, 0.0)
        lb_ref[slot] = log_beta
        lk_ref[slot] = log_keep.astype(BF16)

    def weigh(i, kb, slot):
        v = v_ref[0, keys(kb), :]
        log_keep = lk_ref[slot]
        sums = _dot(log_keep, suffix)
        w = jnp.exp(lb_ref[slot] + sums + carry_ref[i])
        carry_ref[i] = carry_ref[i] + (sums[:, 0:1] + log_keep[:, 0:1].astype(F32))
        zero_v = jnp.zeros_like(v)
        v2 = jnp.concatenate([jnp.where(head_a, v, zero_v), jnp.where(head_a, zero_v, v)], axis=0)
        w2 = jnp.concatenate([w[:t], w[t:]], axis=1).astype(BF16)
        acc_ref[i] += _dot(w2, v2)

    tiles = q_ref.shape[1] // t
    straight = 3
    runs = [min(i + 1, straight) for i in range(tiles)]

    for i in range(tiles):
        q = q_ref[0, i * t:(i + 1) * t, :]
        zero_q = jnp.zeros_like(q)
        q2_ref[i] = jnp.concatenate([jnp.where(head_a, q, zero_q), jnp.where(head_a, zero_q, q)], axis=0)
    acc_ref[...] = jnp.zeros_like(acc_ref)
    carry_ref[...] = jnp.zeros_like(carry_ref)

    logits(0, 0, 0, diagonal=True)
    g = 0
    for i in range(tiles):
        for d in range(runs[i]):
            weigh(i, i - d, g % 2)
            if d + 1 < runs[i]:
                logits(i, i - d - 1, (g + 1) % 2)
            elif i + 1 < tiles:
                logits(i + 1, i + 1, (g + 1) % 2, diagonal=True)
            g += 1

    for i in range(tiles):
        if i >= runs[i]:
            def alive(kb, i=i):
                return jnp.logical_and(kb >= 0, jnp.max(carry_ref[i]) > SB_DEAD).astype(jnp.int32)

            def step(state, i=i, alive=alive):
                kb, _ = state
                logits(i, kb, 0)
                weigh(i, kb, 0)
                return kb - 1, alive(kb - 1)

            first = jnp.int32(i - runs[i])
            lax.while_loop(lambda s: s[1] > 0, step, (first, alive(first)))
        o_ref[0, i * t:(i + 1) * t, :] = acc_ref[i].astype(o_ref.dtype)


def _sb_attention(q, k, v):
    bsz, seq, _ = q.shape
    t = SB_TILE
    pairs = SB_W // LANES
    whole = pl.BlockSpec((1, seq, LANES), lambda b, p: (b, 0, p))
    return pl.pallas_call(
        _sb_kernel,
        grid=(bsz, pairs),
        in_specs=[whole, whole, whole],
        out_specs=whole,
        out_shape=jax.ShapeDtypeStruct((bsz, seq, SB_W), BF16),
        scratch_shapes=[pltpu.VMEM((seq // t, t, LANES), F32), pltpu.VMEM((seq // t, 2 * t, 1), F32),
                        pltpu.VMEM((2, 2 * t, t), F32), pltpu.VMEM((2, 2 * t, t), BF16),
                        pltpu.VMEM((seq // t, 2 * t, LANES), BF16)],
        compiler_params=_params("parallel", "parallel"),
        name="stickbreak_attn",
    )(q, k, v)


def _gdn_kernel(qkv_ref, z_ref, ab_ref, abt_ref, prow_ref, pcol_ref, gout_ref, o_ref,
                state_ref, lhs_ref, add_ref, gam_ref, *, steps_per_seq):
    ts = DN_STEP
    chunks = ts // DN_CHUNK
    j = pl.program_id(0)

    @pl.when(j == 0)
    def _():
        lhs_ref[...] = jnp.zeros_like(lhs_ref)
        add_ref[...] = jnp.zeros_like(add_ref)
        gam_ref[...] = jnp.zeros_like(gam_ref)

    @pl.when(jnp.logical_or(j == 0, (j + steps_per_seq - 1) % steps_per_seq == 0))
    def _():
        state_ref[...] = jnp.zeros_like(state_ref)

    _gdn_scan(z_ref, gout_ref, o_ref, state_ref, lhs_ref, add_ref, gam_ref)

    tb = 2 * DN_CHUNK
    ri = lax.broadcasted_iota(jnp.int32, (tb, tb), 0)
    ci = lax.broadcasted_iota(jnp.int32, (tb, tb), 1)
    same_chunk = (ri >> 6) == (ci >> 6)
    m_incl = same_chunk & (ci <= ri)
    m_strict = same_chunk & (ci < ri)
    eye = jnp.where(ri == ci, 1.0, 0.0)
    tri = jnp.where(m_incl, 1.0, 0.0).astype(BF16)
    tri_t = jnp.where(same_chunk & (ri <= ci), 1.0, 0.0).astype(BF16)
    units = [(h, r) for h in range(DN_HEADS) for r in range(ts // tb)]
    rows_of = lambda r: slice(r * tb, (r + 1) * tb)

    ab = ab_ref[...]
    beta_all = _sigmoid(ab)
    g_all = -jnp.exp(prow_ref[0:1, :]) * _softplus(ab + prow_ref[1:2, :])
    abt = abt_ref[0]
    g_rows = -jnp.exp(pcol_ref[:, 0:1]) * _softplus(abt + pcol_ref[:, 1:2])
    g_all3 = _split_bf16(g_all, 3)
    g_rows3 = _split_bf16(g_rows, 3)
    gcum_all = [sum(_dot(tri, p[rows_of(r)]) for p in g_all3) for r in range(ts // tb)]
    gcum_rows = [sum(_dot(p[:, rows_of(r)], tri_t) for p in g_rows3) for r in range(ts // tb)]

    def col(h, part):
        return slice(part * DN_W + h * DN_DIM, part * DN_W + (h + 1) * DN_DIM)

    q = {u: qkv_ref[rows_of(u[1]), col(u[0], 0)] for u in units}
    k = {u: qkv_ref[rows_of(u[1]), col(u[0], 1)] for u in units}
    v = {u: qkv_ref[rows_of(u[1]), col(u[0], 2)] for u in units}
    beta = {(h, r): beta_all[rows_of(r), h:h + 1] for h, r in units}
    gc = {(h, r): gcum_all[r][:, DN_HEADS + h:DN_HEADS + h + 1] for h, r in units}
    gr = {(h, r): gcum_rows[r][DN_HEADS + h:DN_HEADS + h + 1, :] for h, r in units}
    decay = {u: jnp.where(m_incl, jnp.exp(jnp.where(m_incl, gc[u] - gr[u], 0.0)), 0.0) for u in units}
    kb = {u: k[u] * beta[u] for u in units}
    vb = {u: v[u] * beta[u] for u in units}
    eg = {u: jnp.exp(gc[u]) for u in units}
    qg = {u: q[u] * eg[u] for u in units}
    kq = {u: _dot_nt(jnp.concatenate([kb[u], q[u]], axis=0).astype(BF16), k[u].astype(BF16)) for u in units}
    lower = {u: jnp.where(m_strict, kq[u][:tb] * decay[u], 0.0) for u in units}
    a_qk = {u: (kq[u][tb:] * decay[u]).astype(BF16) for u in units}

    blk8 = (ri >> 3) == (ci >> 3)
    m8 = {u: jnp.where(blk8, -lower[u], 0.0).astype(BF16) for u in units}
    inv = {u: eye + m8[u].astype(F32) for u in units}
    p = {u: _dot(m8[u], m8[u]).astype(BF16) for u in units}
    inv = {u: inv[u] + _dot(inv[u].astype(BF16), p[u]) for u in units}
    p = {u: _dot(p[u], p[u]).astype(BF16) for u in units}
    inv = {u: inv[u] + _dot(inv[u].astype(BF16), p[u]) for u in units}
    for lg in (3, 4, 5):
        off = ((ri >> (lg + 1)) == (ci >> (lg + 1))) & (((ri >> lg) & 1) == 1) & (((ci >> lg) & 1) == 0)
        inv16 = {u: inv[u].astype(BF16) for u in units}
        cb = {u: _dot(jnp.where(off, lower[u], 0.0).astype(BF16), inv16[u]).astype(BF16) for u in units}
        inv = {u: inv[u] - _dot(inv16[u], cb[u]) for u in units}
    sol = {u: _dot(inv[u].astype(BF16), jnp.concatenate([vb[u], kb[u] * eg[u]], axis=1).astype(BF16))
           for u in units}

    a_sol = {u: _dot(a_qk[u], sol[u].astype(BF16)) for u in units}
    for u in units:
        h, r = u
        q_eff = (qg[u] - a_sol[u][:, DN_DIM:]).astype(BF16)
        for cc in range(tb // DN_CHUNK):
            c = r * (tb // DN_CHUNK) + cc
            rows = slice(cc * DN_CHUNK, (cc + 1) * DN_CHUNK)
            g_last = gc[u][(cc + 1) * DN_CHUNK - 1:(cc + 1) * DN_CHUNK, :]
            k_dec = k[u][rows] * jnp.exp(g_last - gc[u][rows])
            kd_sol = _dot_tn(k_dec.astype(BF16), sol[u][rows].astype(BF16))
            lhs_ref[h, c, 0:DN_CHUNK, :] = q_eff[rows]
            lhs_ref[h, c, DN_CHUNK:, :] = kd_sol[:, DN_DIM:].astype(BF16)
            add_ref[h, c, 0:DN_CHUNK, :] = a_sol[u][rows, :DN_DIM]
            add_ref[h, c, DN_CHUNK:, :] = kd_sol[:, :DN_DIM]
            gam_ref[h * chunks + c:h * chunks + c + 1, :] = jnp.broadcast_to(jnp.exp(g_last), (1, LANES))


def _gdn_scan(z_ref, gout_ref, o_ref, state_ref, lhs_ref, add_ref, gam_ref):
    chunks = DN_STEP // DN_CHUNK
    out = [[] for _ in range(DN_HEADS)]
    for c in range(chunks):
        for h in range(DN_HEADS):
            state = state_ref[h]
            res = _dot(lhs_ref[h, c], state.astype(BF16))
            add = add_ref[h, c]
            out[h].append(res[:DN_CHUNK] + add[:DN_CHUNK])
            gamma = gam_ref[h * chunks + c:h * chunks + c + 1, :]
            state_ref[h] = state * gamma - res[DN_CHUNK:] + add[DN_CHUNK:]
    for h in range(DN_HEADS):
        o = jnp.concatenate(out[h], axis=0)
        o = o * lax.rsqrt(jnp.mean(o * o, axis=-1, keepdims=True) + EPS) * gout_ref[...]
        zh = z_ref[:, h * DN_DIM:(h + 1) * DN_DIM]
        o_ref[:, h * DN_DIM:(h + 1) * DN_DIM] = (o * (zh * _sigmoid(zh))).astype(o_ref.dtype)


def _gated_deltanet(qkv, z, ab, abt, prow, pcol, g_out, seq):
    tokens = qkv.shape[0]
    ts = DN_STEP
    steps = seq // ts
    total = tokens // ts
    chunks = ts // DN_CHUNK
    fill_step = lambda j: jnp.minimum(j, total - 1)
    drain_step = lambda j: jnp.maximum(j - 1, 0)
    return pl.pallas_call(
        functools.partial(_gdn_kernel, steps_per_seq=steps),
        grid=(total + 1,),
        in_specs=[pl.BlockSpec((ts, 3 * DN_W), lambda j: (fill_step(j), 0)),
                  pl.BlockSpec((ts, DN_W), lambda j: (drain_step(j), 0)),
                  pl.BlockSpec((ts, LANES), lambda j: (fill_step(j), 0)),
                  pl.BlockSpec((1, 2 * SUBLANES, ts), lambda j: (fill_step(j) // steps, 0, fill_step(j) % steps)),
                  pl.BlockSpec((2, LANES), lambda j: (0, 0)),
                  pl.BlockSpec((2 * SUBLANES, LANES), lambda j: (0, 0)),
                  pl.BlockSpec((1, DN_DIM), lambda j: (0, 0))],
        out_specs=pl.BlockSpec((ts, DN_W), lambda j: (drain_step(j), 0)),
        out_shape=jax.ShapeDtypeStruct((tokens, DN_W), BF16),
        scratch_shapes=[pltpu.VMEM((DN_HEADS, DN_DIM, DN_DIM), F32),
                        pltpu.VMEM((DN_HEADS, chunks, DN_CHUNK + DN_DIM, DN_DIM), BF16),
                        pltpu.VMEM((DN_HEADS, chunks, DN_CHUNK + DN_DIM, DN_DIM), F32),
                        pltpu.VMEM((DN_HEADS * chunks, LANES), F32)],
        compiler_params=_params("arbitrary"),
        name="gated_deltanet",
    )(qkv, z, ab, abt, prow, pcol, g_out.reshape(1, DN_DIM))


def _merge_kernel(h_ref, attn_ref, dn_ref, mod_ref, g_ref, wr_ref, wus_ref, wud_ref, wo_ref, out_ref,
                  u_ref, m_ref):
    u_ref[...] = _rms_mod(h_ref[...], g_ref[...], mod_ref[0, 3:4, :], mod_ref[0, 4:5, :]).astype(BF16)
    for n in range(D_MODEL // MXU_DIM):
        cols = slice(n * MXU_DIM, (n + 1) * MXU_DIM)
        r_sb = _dot(u_ref[...], wr_ref[:, _PROJ_R + n * MXU_DIM:_PROJ_R + (n + 1) * MXU_DIM])
        r_dn = _dot(u_ref[...], wr_ref[:, _PROJ_R + D_MODEL + n * MXU_DIM:_PROJ_R + D_MODEL + (n + 1) * MXU_DIM])
        y_sb = _dot(attn_ref[...], wus_ref[:, cols])
        y_dn = _dot(dn_ref[...], wud_ref[:, cols])
        m_ref[:, cols] = (_sigmoid(r_sb) * y_sb + _sigmoid(r_dn) * y_dn).astype(BF16)
    for n in range(D_MODEL // MXU_DIM):
        cols = slice(n * MXU_DIM, (n + 1) * MXU_DIM)
        out_ref[:, cols] = h_ref[:, cols] + mod_ref[0, 5:6, cols] * _dot(m_ref[...], wo_ref[:, cols])


def _merge(h2, attn, dn, mod3, g_mix, w_pack, w_up_sb, w_up_dn, w_out, seq):
    tokens = h2.shape[0]
    tm = TOKEN_TILE
    tiles_per_seq = seq // tm
    row = lambda w: pl.BlockSpec((tm, w), lambda i: (i, 0))
    return pl.pallas_call(
        _merge_kernel,
        grid=(tokens // tm,),
        in_specs=[row(D_MODEL), row(SB_W), row(DN_W),
                  pl.BlockSpec((1, N_MOD, D_MODEL), lambda i: (i // tiles_per_seq, 0, 0)),
                  _resident((1, D_MODEL), lambda i: (0, 0)),
                  _resident((D_MODEL, _PROJ_W), lambda i: (0, 0)),
                  _resident((SB_W, D_MODEL), lambda i: (0, 0)),
                  _resident((DN_W, D_MODEL), lambda i: (0, 0)),
                  _resident((D_MODEL, D_MODEL), lambda i: (0, 0))],
        out_specs=row(D_MODEL),
        out_shape=jax.ShapeDtypeStruct((tokens, D_MODEL), F32),
        scratch_shapes=[pltpu.VMEM((tm, D_MODEL), BF16), pltpu.VMEM((tm, D_MODEL), BF16)],
        compiler_params=_params("parallel"),
        name="mixer_merge",
    )(h2, attn, dn, mod3, g_mix.reshape(1, D_MODEL), w_pack, w_up_sb, w_up_dn, w_out)


def _layer(h2, mod3, seq, layer, g_ffn1, w_ffn1_in, w_ffn1_out, g_mix, w_in, g_q_sb, g_k_sb, w_conv, a_log,
           dt_bias, g_dn_out, w_up_sb, w_up_dn, w_out, g_ffn2, w_ffn2_in, w_ffn2_out):
    tokens = h2.shape[0]
    bsz = tokens // seq
    h2 = _ffn(h2, mod3, g_ffn1, w_ffn1_in, w_ffn1_out, 0, seq)

    w_pack = w_in[layer].astype(BF16)
    w_gate = w_in[layer, :, _PROJ_GATE:_PROJ_R]
    w_abt = jnp.pad(w_gate.T, ((0, 2 * SUBLANES - 2 * DN_HEADS), (0, 0))).astype(BF16)
    gq_row = jnp.tile(g_q_sb, SB_HEADS).reshape(1, SB_W)
    gk_row = jnp.tile(g_k_sb, SB_HEADS).reshape(1, SB_W)
    q, k, v, qkv_dn, z_dn, ab, abt = _projection(h2, mod3, g_mix, w_pack, w_abt, gq_row, gk_row, w_conv, seq)

    attn = _sb_attention(q.reshape(bsz, seq, SB_W), k.reshape(bsz, seq, SB_W), v.reshape(bsz, seq, SB_W))

    lanes = slice(DN_HEADS, 2 * DN_HEADS)
    prow = jnp.zeros((2, LANES), F32).at[0, lanes].set(a_log).at[1, lanes].set(dt_bias)
    pcol = jnp.zeros((2 * SUBLANES, LANES), F32).at[lanes, 0].set(a_log).at[lanes, 1].set(dt_bias)
    dn = _gated_deltanet(qkv_dn, z_dn, ab, abt, prow, pcol, g_dn_out, seq)

    h2 = _merge(h2, attn.reshape(tokens, SB_W), dn, mod3, g_mix, w_pack,
                w_up_sb.astype(BF16), w_up_dn.astype(BF16), w_out.astype(BF16), seq)
    return _ffn(h2, mod3, g_ffn2, w_ffn2_in, w_ffn2_out, 6, seq)


def kernel(x, c, w_ada, b_ada, g_ffn1, w_ffn1_in, w_ffn1_out, g_mix, w_in, g_q_sb, g_k_sb, w_conv, a_log,
           dt_bias, g_dn_out, w_up_sb, w_up_dn, w_out, g_ffn2, w_ffn2_in, w_ffn2_out):
    bsz, seq, d = x.shape
    depth = w_ada.shape[0]
    h2 = x.reshape(bsz * seq, d)
    for l in range(depth):
        mod3 = _modulation(c, w_ada[l], b_ada[l]).reshape(bsz, N_MOD, d)
        h2 = _layer(h2, mod3, seq, l, g_ffn1[l], w_ffn1_in[l], w_ffn1_out[l], g_mix[l], w_in, g_q_sb[l],
                    g_k_sb[l], w_conv[l], a_log[l], dt_bias[l], g_dn_out[l], w_up_sb[l], w_up_dn[l],
                    w_out[l], g_ffn2[l], w_ffn2_in[l], w_ffn2_out[l])
    return h2.reshape(bsz, seq, d).astype(x.dtype)
```

```python
import functools

import jax
import jax.numpy as jnp
from jax import lax
from jax.experimental import pallas as pl
from jax.experimental.pallas import tpu as pltpu

F32 = jnp.float32
BF16 = jnp.bfloat16

D_MODEL = 1024
D_FF = 2816
N_MOD = 9
EPS = 1e-6

SB_HEADS = 8
SB_HEAD_DIM = 64
SB_W = SB_HEADS * SB_HEAD_DIM
DN_HEADS = 4
DN_DIM = 128
DN_W = DN_HEADS * DN_DIM
DN_CONV = 4
DN_CHUNK = 64

LANES = 128
SUBLANES = 8
MXU_DIM = 256
VMEM_LIMIT = 56 * 1024 * 1024

TOKEN_TILE = 1024
FFN_TILE = 512
FF_CHUNK = MXU_DIM
SB_TILE = MXU_DIM
MASKED = -1e30
SB_DEAD = -104.0
DN_STEP = 8 * DN_CHUNK
DN_CHUNK_LOG2 = DN_CHUNK.bit_length() - 1
DN_BASE_LOG2 = 3

_dot = functools.partial(jnp.dot, preferred_element_type=F32)


def _dot_nt(a, b):
    return lax.dot_general(a, b, (((1,), (1,)), ((), ())), preferred_element_type=F32)


def _dot_tn(a, b):
    return lax.dot_general(a, b, (((0,), (0,)), ((), ())), preferred_element_type=F32)


def _sigmoid(x):
    return 1.0 / (1.0 + jnp.exp(-x))


def _softplus(x):
    return jnp.maximum(x, 0.0) + jnp.log(1.0 + jnp.exp(-jnp.abs(x)))


def _split_bf16(x, pieces):
    out = []
    for _ in range(pieces - 1):
        p = x.astype(BF16)
        out.append(p)
        x = x - p.astype(F32)
    out.append(x.astype(BF16))
    return out


def _rms_mod(x, gain, shift, scale):
    y = x * lax.rsqrt(jnp.mean(x * x, axis=-1, keepdims=True) + EPS)
    return (y * gain) * (1.0 + scale) + shift


def _resident(shape, index_map):
    return pl.BlockSpec(shape, index_map, pipeline_mode=pl.Buffered(1))


def _params(*semantics):
    return pltpu.CompilerParams(dimension_semantics=semantics, vmem_limit_bytes=VMEM_LIMIT)


def _mod_kernel(c_ref, w_ref, b_ref, o_ref):
    c = c_ref[...]
    cond = (c * _sigmoid(c)).astype(BF16)
    o_ref[...] = _dot(cond, w_ref[...].astype(BF16)) + b_ref[...]


def _modulation(c, w_ada, b_ada):
    bsz = c.shape[0]
    n = w_ada.shape[1]
    tn = D_MODEL
    return pl.pallas_call(
        _mod_kernel,
        grid=(n // tn,),
        in_specs=[pl.BlockSpec((bsz, D_MODEL), lambda j: (0, 0)),
                  pl.BlockSpec((D_MODEL, tn), lambda j: (0, j)),
                  pl.BlockSpec((1, tn), lambda j: (0, j))],
        out_specs=pl.BlockSpec((bsz, tn), lambda j: (0, j)),
        out_shape=jax.ShapeDtypeStruct((bsz, n), F32),
        compiler_params=_params("arbitrary"),
        name="adaln_mod",
    )(c, w_ada, b_ada.reshape(1, n))


def _ffn_kernel(x_ref, mod_ref, g_ref, wa_ref, wb_ref, wo_ref, out_ref, u_ref, h_ref, *, mod_row):
    shift = mod_ref[0, mod_row:mod_row + 1, :]
    scale = mod_ref[0, mod_row + 1:mod_row + 2, :]
    u_ref[...] = _rms_mod(x_ref[...], g_ref[...], shift, scale).astype(BF16)
    for c in range(D_FF // FF_CHUNK):
        cols = slice(c * FF_CHUNK, (c + 1) * FF_CHUNK)
        a = _dot(u_ref[...], wa_ref[:, cols].astype(BF16))
        b = _dot(u_ref[...], wb_ref[:, cols].astype(BF16))
        h_ref[:, cols] = (a * _sigmoid(a) * b).astype(BF16)
    for n in range(D_MODEL // MXU_DIM):
        cols = slice(n * MXU_DIM, (n + 1) * MXU_DIM)
        gate = mod_ref[0, mod_row + 2:mod_row + 3, cols]
        out_ref[:, cols] = x_ref[:, cols] + (0.5 * gate) * _dot(h_ref[...], wo_ref[:, cols].astype(BF16))


def _ffn(x2, mod3, gain, w_in, w_out, mod_row, seq):
    tokens = x2.shape[0]
    tm = FFN_TILE
    tiles_per_seq = seq // tm
    return pl.pallas_call(
        functools.partial(_ffn_kernel, mod_row=mod_row),
        grid=(tokens // tm,),
        in_specs=[pl.BlockSpec((tm, D_MODEL), lambda i: (i, 0)),
                  pl.BlockSpec((1, N_MOD, D_MODEL), lambda i: (i // tiles_per_seq, 0, 0)),
                  _resident((1, D_MODEL), lambda i: (0, 0)),
                  _resident((D_MODEL, D_FF), lambda i: (0, 0)),
                  _resident((D_MODEL, D_FF), lambda i: (0, 1)),
                  _resident((D_FF, D_MODEL), lambda i: (0, 0))],
        out_specs=pl.BlockSpec((tm, D_MODEL), lambda i: (i, 0)),
        out_shape=jax.ShapeDtypeStruct((tokens, D_MODEL), F32),
        scratch_shapes=[pltpu.VMEM((tm, D_MODEL), BF16), pltpu.VMEM((tm, D_FF), BF16)],
        compiler_params=_params("parallel"),
        name=f"ffn_row{mod_row}",
    )(x2, mod3, gain.reshape(1, D_MODEL), w_in, w_in, w_out)


_PROJ_Q, _PROJ_K, _PROJ_V = 0, SB_W, 2 * SB_W
_PROJ_DN = 3 * SB_W
_PROJ_Z = _PROJ_DN + 3 * DN_W
_PROJ_MAIN_W = _PROJ_Z + DN_W
_PROJ_GATE = _PROJ_MAIN_W
_PROJ_R = _PROJ_GATE + 2 * DN_HEADS
_PROJ_W = _PROJ_R + 2 * D_MODEL


def _proj_kernel(h_ref, mod_ref, g_ref, wm_ref, wabt_ref, gq_ref, gk_ref, grp_ref, wc_ref,
                 q_ref, k_ref, v_ref, dn_ref, z_ref, ab_ref, abt_ref, u_ref, xbuf_ref, *, tiles_per_seq):
    tm = TOKEN_TILE
    hist = SUBLANES
    u_ref[...] = _rms_mod(h_ref[...], g_ref[...], mod_ref[0, 3:4, :], mod_ref[0, 4:5, :]).astype(BF16)

    @pl.when(pl.program_id(0) % tiles_per_seq == 0)
    def _():
        xbuf_ref[0:hist, :] = jnp.zeros((hist, 3 * DN_W), F32)

    def head_norm(t, gain_row):
        ss = _dot((t * t).astype(BF16), grp_ref[...])
        return t * lax.rsqrt(ss * (1.0 / SB_HEAD_DIM) + EPS) * gain_row

    def conv_group(g):
        cols = slice(g * DN_DIM, (g + 1) * DN_DIM)
        acc = xbuf_ref[hist:hist + tm, cols] * wc_ref[DN_CONV - 1:DN_CONV, cols]
        for tap in range(DN_CONV - 1):
            lo = hist - (DN_CONV - 1) + tap
            acc = acc + xbuf_ref[lo:lo + tm, cols] * wc_ref[tap:tap + 1, cols]
        y = acc * _sigmoid(acc)
        if g < 2 * DN_HEADS:
            y = y * lax.rsqrt(jnp.sum(y * y, axis=-1, keepdims=True) + EPS)
        if g < DN_HEADS:
            y = y * (DN_DIM ** -0.5)
        dn_ref[:, cols] = y

    q = _dot(u_ref[...], wm_ref[:, _PROJ_Q:_PROJ_Q + SB_W])
    q_ref[...] = (head_norm(q, gq_ref[...]) * (SB_HEAD_DIM ** -0.5)).astype(BF16)
    k = _dot(u_ref[...], wm_ref[:, _PROJ_K:_PROJ_K + SB_W])
    k_ref[...] = head_norm(k, gk_ref[...]).astype(BF16)
    v_ref[...] = _dot(u_ref[...], wm_ref[:, _PROJ_V:_PROJ_V + SB_W]).astype(BF16)
    for c in range(3 * DN_W // MXU_DIM):
        cols = slice(c * MXU_DIM, (c + 1) * MXU_DIM)
        xbuf_ref[hist:hist + tm, cols] = _dot(
            u_ref[...], wm_ref[:, _PROJ_DN + c * MXU_DIM:_PROJ_DN + (c + 1) * MXU_DIM])
    for g in range(3 * DN_HEADS):
        conv_group(g)
    xbuf_ref[0:hist, :] = xbuf_ref[tm:tm + hist, :]
    z_ref[...] = _dot(u_ref[...], wm_ref[:, _PROJ_Z:_PROJ_Z + DN_W])
    gates = wm_ref[:, _PROJ_GATE:_PROJ_R]
    gates = jnp.concatenate([gates, jnp.zeros((D_MODEL, LANES - 2 * DN_HEADS), BF16)], axis=1)
    ab_ref[...] = _dot(u_ref[...], gates)
    abt_ref[0] = _dot_nt(wabt_ref[...], u_ref[...])


def _projection(h2, mod3, g_mix, w_pack, w_abt, gq_row, gk_row, w_conv, seq):
    tokens = h2.shape[0]
    bsz = tokens // seq
    tm = TOKEN_TILE
    tiles_per_seq = seq // tm
    grp = jnp.kron(jnp.eye(SB_HEADS, dtype=F32), jnp.ones((SB_HEAD_DIM, SB_HEAD_DIM), F32)).astype(BF16)
    row = lambda w: pl.BlockSpec((tm, w), lambda i: (i, 0))
    return pl.pallas_call(
        functools.partial(_proj_kernel, tiles_per_seq=tiles_per_seq),
        grid=(tokens // tm,),
        in_specs=[row(D_MODEL),
                  pl.BlockSpec((1, N_MOD, D_MODEL), lambda i: (i // tiles_per_seq, 0, 0)),
                  _resident((1, D_MODEL), lambda i: (0, 0)),
                  _resident((D_MODEL, _PROJ_W), lambda i: (0, 0)),
                  _resident((2 * SUBLANES, D_MODEL), lambda i: (0, 0)),
                  _resident((1, SB_W), lambda i: (0, 0)),
                  _resident((1, SB_W), lambda i: (0, 0)),
                  _resident((SB_W, SB_W), lambda i: (0, 0)),
                  _resident((DN_CONV, 3 * DN_W), lambda i: (0, 0))],
        out_specs=[row(SB_W), row(SB_W), row(SB_W), row(3 * DN_W), row(DN_W), row(LANES),
                   pl.BlockSpec((1, 2 * SUBLANES, tm), lambda i: (i // tiles_per_seq, 0, i % tiles_per_seq))],
        out_shape=[jax.ShapeDtypeStruct((tokens, SB_W), BF16),
                   jax.ShapeDtypeStruct((tokens, SB_W), BF16),
                   jax.ShapeDtypeStruct((tokens, SB_W), BF16),
                   jax.ShapeDtypeStruct((tokens, 3 * DN_W), F32),
                   jax.ShapeDtypeStruct((tokens, DN_W), F32),
                   jax.ShapeDtypeStruct((tokens, LANES), F32),
                   jax.ShapeDtypeStruct((bsz, 2 * SUBLANES, seq), F32)],
        scratch_shapes=[pltpu.VMEM((tm, D_MODEL), BF16), pltpu.VMEM((tm + SUBLANES, 3 * DN_W), F32)],
        compiler_params=_params("arbitrary"),
        name="mixer_in_proj",
    )(h2, mod3, g_mix.reshape(1, D_MODEL), w_pack, w_abt, gq_row, gk_row, grp, w_conv)


def _sb_kernel(q_ref, k_ref, v_ref, o_ref, acc_ref, carry_ref, lb_ref, lk_ref, q2_ref):
    t = SB_TILE
    head_a = lax.broadcasted_iota(jnp.int32, (t, LANES), 1) < SB_HEAD_DIM
    suffix = jnp.where(lax.broadcasted_iota(jnp.int32, (t, t), 0) > lax.broadcasted_iota(jnp.int32, (t, t), 1),
                       1.0, 0.0).astype(BF16)
    causal = (lax.broadcasted_iota(jnp.int32, (2 * t, t), 1)
              < (lax.broadcasted_iota(jnp.int32, (2 * t, t), 0) & (t - 1)))

    def keys(kb):
        return pl.ds(kb * t, t) if isinstance(kb, int) else pl.ds(pl.multiple_of(kb * t, t), t)

    def logits(i, kb, slot, diagonal=False):
        k = k_ref[0, keys(kb), :]
        z = _dot_nt(q2_ref[i], k)
        log_beta = jnp.minimum(z, 0.0) - jnp.log(1.0 + jnp.exp(-jnp.abs(z)))
        log_keep = log_beta - z
        if diagonal:
            log_beta = jnp.where(causal, log_beta, MASKED)
            log_keep = jnp.where(causal, log_keep, 0.0)
        lb_ref[slot] = log_beta
        lk_ref[slot] = log_keep.astype(BF16)

    def weigh(i, kb, slot):
        v = v_ref[0, keys(kb), :]
        log_keep = lk_ref[slot]
        sums = _dot(log_keep, suffix)
        w = jnp.exp(lb_ref[slot] + sums + carry_ref[i])
        carry_ref[i] = carry_ref[i] + (sums[:, 0:1] + log_keep[:, 0:1].astype(F32))
        zero_v = jnp.zeros_like(v)
        v2 = jnp.concatenate([jnp.where(head_a, v, zero_v), jnp.where(head_a, zero_v, v)], axis=0)
        w2 = jnp.concatenate([w[:t], w[t:]], axis=1).astype(BF16)
        acc_ref[i] += _dot(w2, v2)

    tiles = q_ref.shape[1] // t
    straight = 3
    runs = [min(i + 1, straight) for i in range(tiles)]

    for i in range(tiles):
        q = q_ref[0, i * t:(i + 1) * t, :]
        zero_q = jnp.zeros_like(q)
        q2_ref[i] = jnp.concatenate([jnp.where(head_a, q, zero_q), jnp.where(head_a, zero_q, q)], axis=0)
    acc_ref[...] = jnp.zeros_like(acc_ref)
    carry_ref[...] = jnp.zeros_like(carry_ref)

    logits(0, 0, 0, diagonal=True)
    g = 0
    for i in range(tiles):
        for d in range(runs[i]):
            weigh(i, i - d, g % 2)
            if d + 1 < runs[i]:
                logits(i, i - d - 1, (g + 1) % 2)
            elif i + 1 < tiles:
                logits(i + 1, i + 1, (g + 1) % 2, diagonal=True)
            g += 1

    for i in range(tiles):
        if i >= runs[i]:
            def alive(kb, i=i):
                return jnp.logical_and(kb >= 0, jnp.max(carry_ref[i]) > SB_DEAD).astype(jnp.int32)

            def step(state, i=i, alive=alive):
                kb, _ = state
                logits(i, kb, 0)
                weigh(i, kb, 0)
                return kb - 1, alive(kb - 1)

            first = jnp.int32(i - runs[i])
            lax.while_loop(lambda s: s[1] > 0, step, (first, alive(first)))
        o_ref[0, i * t:(i + 1) * t, :] = acc_ref[i].astype(o_ref.dtype)


def _sb_attention(q, k, v):
    bsz, seq, _ = q.shape
    t = SB_TILE
    pairs = SB_W // LANES
    whole = pl.BlockSpec((1, seq, LANES), lambda b, p: (b, 0, p))
    return pl.pallas_call(
        _sb_kernel,
        grid=(bsz, pairs),
        in_specs=[whole, whole, whole],
        out_specs=whole,
        out_shape=jax.ShapeDtypeStruct((bsz, seq, SB_W), BF16),
        scratch_shapes=[pltpu.VMEM((seq // t, t, LANES), F32), pltpu.VMEM((seq // t, 2 * t, 1), F32),
                        pltpu.VMEM((2, 2 * t, t), F32), pltpu.VMEM((2, 2 * t, t), BF16),
                        pltpu.VMEM((seq // t, 2 * t, LANES), BF16)],
        compiler_params=_params("parallel", "parallel"),
        name="stickbreak_attn",
    )(q, k, v)


def _gdn_kernel(qkv_ref, z_ref, ab_ref, abt_ref, prow_ref, pcol_ref, gout_ref, o_ref,
                state_ref, lhs_ref, add_ref, gam_ref, *, steps_per_seq):
    ts = DN_STEP
    chunks = ts // DN_CHUNK
    j = pl.program_id(0)

    @pl.when(j == 0)
    def _():
        lhs_ref[...] = jnp.zeros_like(lhs_ref)
        add_ref[...] = jnp.zeros_like(add_ref)
        gam_ref[...] = jnp.zeros_like(gam_ref)

    @pl.when(jnp.logical_or(j == 0, (j + steps_per_seq - 1) % steps_per_seq == 0))
    def _():
        state_ref[...] = jnp.zeros_like(state_ref)

    _gdn_scan(z_ref, gout_ref, o_ref, state_ref, lhs_ref, add_ref, gam_ref)

    tb = 2 * DN_CHUNK
    ri = lax.broadcasted_iota(jnp.int32, (tb, tb), 0)
    ci = lax.broadcasted_iota(jnp.int32, (tb, tb), 1)
    same_chunk = (ri >> DN_CHUNK_LOG2) == (ci >> DN_CHUNK_LOG2)
    m_incl = same_chunk & (ci <= ri)
    m_strict = same_chunk & (ci < ri)
    eye = jnp.where(ri == ci, 1.0, 0.0)
    tri = jnp.where(m_incl, 1.0, 0.0).astype(BF16)
    tri_t = jnp.where(same_chunk & (ri <= ci), 1.0, 0.0).astype(BF16)
    units = [(h, r) for h in range(DN_HEADS) for r in range(ts // tb)]
    rows_of = lambda r: slice(r * tb, (r + 1) * tb)

    ab = ab_ref[...]
    beta_all = _sigmoid(ab)
    g_all = -jnp.exp(prow_ref[0:1, :]) * _softplus(ab + prow_ref[1:2, :])
    abt = abt_ref[0]
    g_rows = -jnp.exp(pcol_ref[:, 0:1]) * _softplus(abt + pcol_ref[:, 1:2])
    g_all3 = _split_bf16(g_all, 3)
    g_rows3 = _split_bf16(g_rows, 3)
    gcum_all = [sum(_dot(tri, p[rows_of(r)]) for p in g_all3) for r in range(ts // tb)]
    gcum_rows = [sum(_dot(p[:, rows_of(r)], tri_t) for p in g_rows3) for r in range(ts // tb)]

    def col(h, part):
        return slice(part * DN_W + h * DN_DIM, part * DN_W + (h + 1) * DN_DIM)

    q = {u: qkv_ref[rows_of(u[1]), col(u[0], 0)] for u in units}
    k = {u: qkv_ref[rows_of(u[1]), col(u[0], 1)] for u in units}
    v = {u: qkv_ref[rows_of(u[1]), col(u[0], 2)] for u in units}
    beta = {(h, r): beta_all[rows_of(r), h:h + 1] for h, r in units}
    gc = {(h, r): gcum_all[r][:, DN_HEADS + h:DN_HEADS + h + 1] for h, r in units}
    gr = {(h, r): gcum_rows[r][DN_HEADS + h:DN_HEADS + h + 1, :] for h, r in units}
    decay = {u: jnp.where(m_incl, jnp.exp(jnp.where(m_incl, gc[u] - gr[u], 0.0)), 0.0) for u in units}
    kb = {u: k[u] * beta[u] for u in units}
    vb = {u: v[u] * beta[u] for u in units}
    eg = {u: jnp.exp(gc[u]) for u in units}
    qg = {u: q[u] * eg[u] for u in units}
    kq = {u: _dot_nt(jnp.concatenate([kb[u], q[u]], axis=0).astype(BF16), k[u].astype(BF16)) for u in units}
    lower = {u: jnp.where(m_strict, kq[u][:tb] * decay[u], 0.0) for u in units}
    a_qk = {u: (kq[u][tb:] * decay[u]).astype(BF16) for u in units}

    blk8 = (ri >> DN_BASE_LOG2) == (ci >> DN_BASE_LOG2)
    m8 = {u: jnp.where(blk8, -lower[u], 0.0).astype(BF16) for u in units}
    inv = {u: eye + m8[u].astype(F32) for u in units}
    p = {u: _dot(m8[u], m8[u]).astype(BF16) for u in units}
    inv = {u: inv[u] + _dot(inv[u].astype(BF16), p[u]) for u in units}
    p = {u: _dot(p[u], p[u]).astype(BF16) for u in units}
    inv = {u: inv[u] + _dot(inv[u].astype(BF16), p[u]) for u in units}
    for lg in range(DN_BASE_LOG2, DN_CHUNK_LOG2):
        off = ((ri >> (lg + 1)) == (ci >> (lg + 1))) & (((ri >> lg) & 1) == 1) & (((ci >> lg) & 1) == 0)
        inv16 = {u: inv[u].astype(BF16) for u in units}
        cb = {u: _dot(jnp.where(off, lower[u], 0.0).astype(BF16), inv16[u]).astype(BF16) for u in units}
        inv = {u: inv[u] - _dot(inv16[u], cb[u]) for u in units}
    sol = {u: _dot(inv[u].astype(BF16), jnp.concatenate([vb[u], kb[u] * eg[u]], axis=1).astype(BF16))
           for u in units}

    a_sol = {u: _dot(a_qk[u], sol[u].astype(BF16)) for u in units}
    for u in units:
        h, r = u
        q_eff = (qg[u] - a_sol[u][:, DN_DIM:]).astype(BF16)
        for cc in range(tb // DN_CHUNK):
            c = r * (tb // DN_CHUNK) + cc
            rows = slice(cc * DN_CHUNK, (cc + 1) * DN_CHUNK)
            g_last = gc[u][(cc + 1) * DN_CHUNK - 1:(cc + 1) * DN_CHUNK, :]
            k_dec = k[u][rows] * jnp.exp(g_last - gc[u][rows])
            kd_sol = _dot_tn(k_dec.astype(BF16), sol[u][rows].astype(BF16))
            lhs_ref[h, c, 0:DN_CHUNK, :] = q_eff[rows]
            lhs_ref[h, c, DN_CHUNK:, :] = kd_sol[:, DN_DIM:].astype(BF16)
            add_ref[h, c, 0:DN_CHUNK, :] = a_sol[u][rows, :DN_DIM]
            add_ref[h, c, DN_CHUNK:, :] = kd_sol[:, :DN_DIM]
            gam_ref[h * chunks + c:h * chunks + c + 1, :] = jnp.broadcast_to(jnp.exp(g_last), (1, LANES))


def _gdn_scan(z_ref, gout_ref, o_ref, state_ref, lhs_ref, add_ref, gam_ref):
    chunks = DN_STEP // DN_CHUNK
    out = [[] for _ in range(DN_HEADS)]
    for c in range(chunks):
        for h in range(DN_HEADS):
            state = state_ref[h]
            res = _dot(lhs_ref[h, c], state.astype(BF16))
            add = add_ref[h, c]
            out[h].append(res[:DN_CHUNK] + add[:DN_CHUNK])
            gamma = gam_ref[h * chunks + c:h * chunks + c + 1, :]
            state_ref[h] = state * gamma - res[DN_CHUNK:] + add[DN_CHUNK:]
    for h in range(DN_HEADS):
        o = jnp.concatenate(out[h], axis=0)
        o = o * lax.rsqrt(jnp.mean(o * o, axis=-1, keepdims=True) + EPS) * gout_ref[...]
        zh = z_ref[:, h * DN_DIM:(h + 1) * DN_DIM]
        o_ref[:, h * DN_DIM:(h + 1) * DN_DIM] = (o * (zh * _sigmoid(zh))).astype(o_ref.dtype)


def _gated_deltanet(qkv, z, ab, abt, prow, pcol, g_out, seq):
    tokens = qkv.shape[0]
    ts = DN_STEP
    steps = seq // ts
    total = tokens // ts
    chunks = ts // DN_CHUNK
    fill_step = lambda j: jnp.minimum(j, total - 1)
    drain_step = lambda j: jnp.maximum(j - 1, 0)
    return pl.pallas_call(
        functools.partial(_gdn_kernel, steps_per_seq=steps),
        grid=(total + 1,),
        in_specs=[pl.BlockSpec((ts, 3 * DN_W), lambda j: (fill_step(j), 0)),
                  pl.BlockSpec((ts, DN_W), lambda j: (drain_step(j), 0)),
                  pl.BlockSpec((ts, LANES), lambda j: (fill_step(j), 0)),
                  pl.BlockSpec((1, 2 * SUBLANES, ts), lambda j: (fill_step(j) // steps, 0, fill_step(j) % steps)),
                  pl.BlockSpec((2, LANES), lambda j: (0, 0)),
                  pl.BlockSpec((2 * SUBLANES, LANES), lambda j: (0, 0)),
                  pl.BlockSpec((1, DN_DIM), lambda j: (0, 0))],
        out_specs=pl.BlockSpec((ts, DN_W), lambda j: (drain_step(j), 0)),
        out_shape=jax.ShapeDtypeStruct((tokens, DN_W), BF16),
        scratch_shapes=[pltpu.VMEM((DN_HEADS, DN_DIM, DN_DIM), F32),
                        pltpu.VMEM((DN_HEADS, chunks, DN_CHUNK + DN_DIM, DN_DIM), BF16),
                        pltpu.VMEM((DN_HEADS, chunks, DN_CHUNK + DN_DIM, DN_DIM), F32),
                        pltpu.VMEM((DN_HEADS * chunks, LANES), F32)],
        compiler_params=_params("arbitrary"),
        name="gated_deltanet",
    )(qkv, z, ab, abt, prow, pcol, g_out.reshape(1, DN_DIM))


def _merge_kernel(h_ref, attn_ref, dn_ref, mod_ref, g_ref, wr_ref, wus_ref, wud_ref, wo_ref, out_ref,
                  u_ref, m_ref):
    u_ref[...] = _rms_mod(h_ref[...], g_ref[...], mod_ref[0, 3:4, :], mod_ref[0, 4:5, :]).astype(BF16)
    for n in range(D_MODEL // MXU_DIM):
        cols = slice(n * MXU_DIM, (n + 1) * MXU_DIM)
        r_sb = _dot(u_ref[...], wr_ref[:, _PROJ_R + n * MXU_DIM:_PROJ_R + (n + 1) * MXU_DIM])
        r_dn = _dot(u_ref[...], wr_ref[:, _PROJ_R + D_MODEL + n * MXU_DIM:_PROJ_R + D_MODEL + (n + 1) * MXU_DIM])
        y_sb = _dot(attn_ref[...], wus_ref[:, cols].astype(BF16))
        y_dn = _dot(dn_ref[...], wud_ref[:, cols].astype(BF16))
        m_ref[:, cols] = (_sigmoid(r_sb) * y_sb + _sigmoid(r_dn) * y_dn).astype(BF16)
    for n in range(D_MODEL // MXU_DIM):
        cols = slice(n * MXU_DIM, (n + 1) * MXU_DIM)
        out_ref[:, cols] = h_ref[:, cols] + mod_ref[0, 5:6, cols] * _dot(m_ref[...], wo_ref[:, cols].astype(BF16))


def _merge(h2, attn, dn, mod3, g_mix, w_pack, w_up_sb, w_up_dn, w_out, seq):
    tokens = h2.shape[0]
    tm = TOKEN_TILE
    tiles_per_seq = seq // tm
    row = lambda w: pl.BlockSpec((tm, w), lambda i: (i, 0))
    return pl.pallas_call(
        _merge_kernel,
        grid=(tokens // tm,),
        in_specs=[row(D_MODEL), row(SB_W), row(DN_W),
                  pl.BlockSpec((1, N_MOD, D_MODEL), lambda i: (i // tiles_per_seq, 0, 0)),
                  _resident((1, D_MODEL), lambda i: (0, 0)),
                  _resident((D_MODEL, _PROJ_W), lambda i: (0, 0)),
                  _resident((SB_W, D_MODEL), lambda i: (0, 0)),
                  _resident((DN_W, D_MODEL), lambda i: (0, 0)),
                  _resident((D_MODEL, D_MODEL), lambda i: (0, 0))],
        out_specs=row(D_MODEL),
        out_shape=jax.ShapeDtypeStruct((tokens, D_MODEL), F32),
        scratch_shapes=[pltpu.VMEM((tm, D_MODEL), BF16), pltpu.VMEM((tm, D_MODEL), BF16)],
        compiler_params=_params("parallel"),
        name="mixer_merge",
    )(h2, attn, dn, mod3, g_mix.reshape(1, D_MODEL), w_pack, w_up_sb, w_up_dn, w_out)


def _layer(h2, mod3, seq, layer, g_ffn1, w_ffn1_in, w_ffn1_out, g_mix, w_in, g_q_sb, g_k_sb, w_conv, a_log,
           dt_bias, g_dn_out, w_up_sb, w_up_dn, w_out, g_ffn2, w_ffn2_in, w_ffn2_out):
    tokens = h2.shape[0]
    bsz = tokens // seq
    h2 = _ffn(h2, mod3, g_ffn1, w_ffn1_in, w_ffn1_out, 0, seq)

    w_pack = w_in[layer].astype(BF16)
    w_gate = w_in[layer, :, _PROJ_GATE:_PROJ_R]
    w_abt = jnp.pad(w_gate.T, ((0, 2 * SUBLANES - 2 * DN_HEADS), (0, 0))).astype(BF16)
    gq_row = jnp.tile(g_q_sb, SB_HEADS).reshape(1, SB_W)
    gk_row = jnp.tile(g_k_sb, SB_HEADS).reshape(1, SB_W)
    q, k, v, qkv_dn, z_dn, ab, abt = _projection(h2, mod3, g_mix, w_pack, w_abt, gq_row, gk_row, w_conv, seq)

    attn = _sb_attention(q.reshape(bsz, seq, SB_W), k.reshape(bsz, seq, SB_W), v.reshape(bsz, seq, SB_W))

    decay_params = jnp.stack([a_log, dt_bias])
    prow = jnp.pad(decay_params, ((0, 0), (DN_HEADS, LANES - 2 * DN_HEADS)))
    pcol = jnp.pad(decay_params.T, ((DN_HEADS, 2 * SUBLANES - 2 * DN_HEADS), (0, LANES - 2)))
    dn = _gated_deltanet(qkv_dn, z_dn, ab, abt, prow, pcol, g_dn_out, seq)

    h2 = _merge(h2, attn.reshape(tokens, SB_W), dn, mod3, g_mix, w_pack,
                w_up_sb, w_up_dn, w_out, seq)
    return _ffn(h2, mod3, g_ffn2, w_ffn2_in, w_ffn2_out, 6, seq)


def kernel(x, c, w_ada, b_ada, g_ffn1, w_ffn1_in, w_ffn1_out, g_mix, w_in, g_q_sb, g_k_sb, w_conv, a_log,
           dt_bias, g_dn_out, w_up_sb, w_up_dn, w_out, g_ffn2, w_ffn2_in, w_ffn2_out):
    bsz, seq, d = x.shape
    depth = w_ada.shape[0]
    h2 = x.reshape(bsz * seq, d)
    for l in range(depth):
        mod3 = _modulation(c, w_ada[l], b_ada[l]).reshape(bsz, N_MOD, d)
        h2 = _layer(h2, mod3, seq, l, g_ffn1[l], w_ffn1_in[l], w_ffn1_out[l], g_mix[l], w_in, g_q_sb[l],
                    g_k_sb[l], w_conv[l], a_log[l], dt_bias[l], g_dn_out[l], w_up_sb[l], w_up_dn[l],
                    w_out[l], g_ffn2[l], w_ffn2_in[l], w_ffn2_out[l])
    return h2.reshape(bsz, seq, d).astype(x.dtype)
```

```python
import functools

import jax
import jax.numpy as jnp
from jax import lax
from jax.experimental import pallas as pl
from jax.experimental.pallas import tpu as pltpu

F32 = jnp.float32
BF16 = jnp.bfloat16

D_MODEL = 1024
D_FF = 2816
N_MOD = 9
EPS = 1e-6

SB_HEADS = 8
SB_HEAD_DIM = 64
SB_W = SB_HEADS * SB_HEAD_DIM
DN_HEADS = 4
DN_DIM = 128
DN_W = DN_HEADS * DN_DIM
DN_CONV = 4
DN_CHUNK = 64

LANES = 128
SUBLANES = 8
MXU_DIM = 256
VMEM_LIMIT = 56 * 1024 * 1024

ADA_TILE = 3 * D_MODEL
TOKEN_TILE = 1024
FFN_TILE = 512
FF_CHUNK = MXU_DIM
SB_TILE = MXU_DIM
MASKED = -1e30
SB_DEAD = -104.0
SB_STRAIGHT = 3
DN_STEP = 8 * DN_CHUNK
DN_CHUNK_LOG2 = DN_CHUNK.bit_length() - 1
DN_BASE_LOG2 = 3

_dot = functools.partial(jnp.dot, preferred_element_type=F32)


def _dot_nt(a, b):
    return lax.dot_general(a, b, (((1,), (1,)), ((), ())), preferred_element_type=F32)


def _dot_tn(a, b):
    return lax.dot_general(a, b, (((0,), (0,)), ((), ())), preferred_element_type=F32)


def _sigmoid(x):
    return 1.0 / (1.0 + jnp.exp(-x))


def _softplus(x):
    return jnp.maximum(x, 0.0) + jnp.log(1.0 + jnp.exp(-jnp.abs(x)))


def _split_bf16(x, pieces):
    out = []
    for _ in range(pieces - 1):
        p = x.astype(BF16)
        out.append(p)
        x = x - p.astype(F32)
    out.append(x.astype(BF16))
    return out


def _rms_mod(x, gain, shift, scale):
    y = x * lax.rsqrt(jnp.mean(x * x, axis=-1, keepdims=True) + EPS)
    return (y * gain) * (1.0 + scale) + shift


def _resident(shape, index_map):
    return pl.BlockSpec(shape, index_map, pipeline_mode=pl.Buffered(1))


def _params(*semantics):
    return pltpu.CompilerParams(dimension_semantics=semantics, vmem_limit_bytes=VMEM_LIMIT)


def _mod_kernel(c_ref, w_ref, b_ref, o_ref):
    c = c_ref[...]
    cond = (c * _sigmoid(c)).astype(BF16)
    o_ref[...] = _dot(cond, w_ref[...].astype(BF16)) + b_ref[...]


def _modulation(c, w_ada, b_ada):
    bsz = c.shape[0]
    n = w_ada.shape[1]
    tn = ADA_TILE
    return pl.pallas_call(
        _mod_kernel,
        grid=(n // tn,),
        in_specs=[pl.BlockSpec((bsz, D_MODEL), lambda j: (0, 0)),
                  pl.BlockSpec((D_MODEL, tn), lambda j: (0, j)),
                  pl.BlockSpec((1, tn), lambda j: (0, j))],
        out_specs=pl.BlockSpec((bsz, tn), lambda j: (0, j)),
        out_shape=jax.ShapeDtypeStruct((bsz, n), F32),
        compiler_params=_params("arbitrary"),
        name="adaln_mod",
    )(c, w_ada, b_ada.reshape(1, n))


def _ffn_kernel(x_ref, mod_ref, g_ref, wa_ref, wb_ref, wo_ref, out_ref, u_ref, h_ref, *, mod_row):
    shift = mod_ref[0, mod_row:mod_row + 1, :]
    scale = mod_ref[0, mod_row + 1:mod_row + 2, :]
    u_ref[...] = _rms_mod(x_ref[...], g_ref[...], shift, scale).astype(BF16)
    for c in range(D_FF // FF_CHUNK):
        cols = slice(c * FF_CHUNK, (c + 1) * FF_CHUNK)
        a = _dot(u_ref[...], wa_ref[:, cols].astype(BF16))
        b = _dot(u_ref[...], wb_ref[:, cols].astype(BF16))
        h_ref[:, cols] = (a * _sigmoid(a) * b).astype(BF16)
    for n in range(D_MODEL // MXU_DIM):
        cols = slice(n * MXU_DIM, (n + 1) * MXU_DIM)
        gate = mod_ref[0, mod_row + 2:mod_row + 3, cols]
        out_ref[:, cols] = x_ref[:, cols] + (0.5 * gate) * _dot(h_ref[...], wo_ref[:, cols].astype(BF16))


def _ffn(x2, mod3, gain, w_in, w_out, mod_row, seq):
    tokens = x2.shape[0]
    tm = FFN_TILE
    tiles_per_seq = seq // tm
    return pl.pallas_call(
        functools.partial(_ffn_kernel, mod_row=mod_row),
        grid=(tokens // tm,),
        in_specs=[pl.BlockSpec((tm, D_MODEL), lambda i: (i, 0)),
                  pl.BlockSpec((1, N_MOD, D_MODEL), lambda i: (i // tiles_per_seq, 0, 0)),
                  _resident((1, D_MODEL), lambda i: (0, 0)),
                  _resident((D_MODEL, D_FF), lambda i: (0, 0)),
                  _resident((D_MODEL, D_FF), lambda i: (0, 1)),
                  _resident((D_FF, D_MODEL), lambda i: (0, 0))],
        out_specs=pl.BlockSpec((tm, D_MODEL), lambda i: (i, 0)),
        out_shape=jax.ShapeDtypeStruct((tokens, D_MODEL), F32),
        scratch_shapes=[pltpu.VMEM((tm, D_MODEL), BF16), pltpu.VMEM((tm, D_FF), BF16)],
        compiler_params=_params("parallel"),
        name=f"ffn_row{mod_row}",
    )(x2, mod3, gain.reshape(1, D_MODEL), w_in, w_in, w_out)


_PROJ_Q, _PROJ_K, _PROJ_V = 0, SB_W, 2 * SB_W
_PROJ_DN = 3 * SB_W
_PROJ_Z = _PROJ_DN + 3 * DN_W
_PROJ_MAIN_W = _PROJ_Z + DN_W
_PROJ_GATE = _PROJ_MAIN_W
_PROJ_R = _PROJ_GATE + 2 * DN_HEADS
_PROJ_W = _PROJ_R + 2 * D_MODEL


def _proj_kernel(h_ref, mod_ref, g_ref, wm_ref, wabt_ref, gq_ref, gk_ref, grp_ref, wc_ref,
                 q_ref, k_ref, v_ref, dn_ref, z_ref, ab_ref, abt_ref, u_ref, xbuf_ref, *, tiles_per_seq):
    tm = TOKEN_TILE
    hist = SUBLANES
    u_ref[...] = _rms_mod(h_ref[...], g_ref[...], mod_ref[0, 3:4, :], mod_ref[0, 4:5, :]).astype(BF16)

    @pl.when(pl.program_id(0) % tiles_per_seq == 0)
    def _():
        xbuf_ref[0:hist, :] = jnp.zeros((hist, 3 * DN_W), F32)

    def head_norm(t, gain_row):
        ss = _dot((t * t).astype(BF16), grp_ref[...])
        return t * lax.rsqrt(ss * (1.0 / SB_HEAD_DIM) + EPS) * gain_row

    def conv_group(g):
        cols = slice(g * DN_DIM, (g + 1) * DN_DIM)
        acc = xbuf_ref[hist:hist + tm, cols] * wc_ref[DN_CONV - 1:DN_CONV, cols]
        for tap in range(DN_CONV - 1):
            lo = hist - (DN_CONV - 1) + tap
            acc = acc + xbuf_ref[lo:lo + tm, cols] * wc_ref[tap:tap + 1, cols]
        y = acc * _sigmoid(acc)
        if g < 2 * DN_HEADS:
            y = y * lax.rsqrt(jnp.sum(y * y, axis=-1, keepdims=True) + EPS)
        if g < DN_HEADS:
            y = y * (DN_DIM ** -0.5)
        dn_ref[:, cols] = y

    q = _dot(u_ref[...], wm_ref[:, _PROJ_Q:_PROJ_Q + SB_W])
    q_ref[...] = (head_norm(q, gq_ref[...]) * (SB_HEAD_DIM ** -0.5)).astype(BF16)
    k = _dot(u_ref[...], wm_ref[:, _PROJ_K:_PROJ_K + SB_W])
    k_ref[...] = head_norm(k, gk_ref[...]).astype(BF16)
    v_ref[...] = _dot(u_ref[...], wm_ref[:, _PROJ_V:_PROJ_V + SB_W]).astype(BF16)
    for c in range(3 * DN_W // MXU_DIM):
        cols = slice(c * MXU_DIM, (c + 1) * MXU_DIM)
        xbuf_ref[hist:hist + tm, cols] = _dot(
            u_ref[...], wm_ref[:, _PROJ_DN + c * MXU_DIM:_PROJ_DN + (c + 1) * MXU_DIM])
    for g in range(3 * DN_HEADS):
        conv_group(g)
    xbuf_ref[0:hist, :] = xbuf_ref[tm:tm + hist, :]
    z_ref[...] = _dot(u_ref[...], wm_ref[:, _PROJ_Z:_PROJ_Z + DN_W])
    gates = wm_ref[:, _PROJ_GATE:_PROJ_R]
    gates = jnp.concatenate([gates, jnp.zeros((D_MODEL, LANES - 2 * DN_HEADS), BF16)], axis=1)
    ab_ref[...] = _dot(u_ref[...], gates)
    abt_ref[0] = _dot_nt(wabt_ref[...], u_ref[...])


def _projection(h2, mod3, g_mix, w_pack, w_abt, gq_row, gk_row, w_conv, seq):
    tokens = h2.shape[0]
    bsz = tokens // seq
    tm = TOKEN_TILE
    tiles_per_seq = seq // tm
    grp = jnp.kron(jnp.eye(SB_HEADS, dtype=F32), jnp.ones((SB_HEAD_DIM, SB_HEAD_DIM), F32)).astype(BF16)
    row = lambda w: pl.BlockSpec((tm, w), lambda i: (i, 0))
    return pl.pallas_call(
        functools.partial(_proj_kernel, tiles_per_seq=tiles_per_seq),
        grid=(tokens // tm,),
        in_specs=[row(D_MODEL),
                  pl.BlockSpec((1, N_MOD, D_MODEL), lambda i: (i // tiles_per_seq, 0, 0)),
                  _resident((1, D_MODEL), lambda i: (0, 0)),
                  _resident((D_MODEL, _PROJ_W), lambda i: (0, 0)),
                  _resident((2 * SUBLANES, D_MODEL), lambda i: (0, 0)),
                  _resident((1, SB_W), lambda i: (0, 0)),
                  _resident((1, SB_W), lambda i: (0, 0)),
                  _resident((SB_W, SB_W), lambda i: (0, 0)),
                  _resident((DN_CONV, 3 * DN_W), lambda i: (0, 0))],
        out_specs=[row(SB_W), row(SB_W), row(SB_W), row(3 * DN_W), row(DN_W), row(LANES),
                   pl.BlockSpec((1, 2 * SUBLANES, tm), lambda i: (i // tiles_per_seq, 0, i % tiles_per_seq))],
        out_shape=[jax.ShapeDtypeStruct((tokens, SB_W), BF16),
                   jax.ShapeDtypeStruct((tokens, SB_W), BF16),
                   jax.ShapeDtypeStruct((tokens, SB_W), BF16),
                   jax.ShapeDtypeStruct((tokens, 3 * DN_W), F32),
                   jax.ShapeDtypeStruct((tokens, DN_W), F32),
                   jax.ShapeDtypeStruct((tokens, LANES), F32),
                   jax.ShapeDtypeStruct((bsz, 2 * SUBLANES, seq), F32)],
        scratch_shapes=[pltpu.VMEM((tm, D_MODEL), BF16), pltpu.VMEM((tm + SUBLANES, 3 * DN_W), F32)],
        compiler_params=_params("arbitrary"),
        name="mixer_in_proj",
    )(h2, mod3, g_mix.reshape(1, D_MODEL), w_pack, w_abt, gq_row, gk_row, grp, w_conv)


def _sb_kernel(q_ref, k_ref, v_ref, o_ref, acc_ref, carry_ref, lb_ref, lk_ref, q2_ref):
    t = SB_TILE
    head_a = lax.broadcasted_iota(jnp.int32, (t, LANES), 1) < SB_HEAD_DIM
    suffix = jnp.where(lax.broadcasted_iota(jnp.int32, (t, t), 0) > lax.broadcasted_iota(jnp.int32, (t, t), 1),
                       1.0, 0.0).astype(BF16)
    causal = (lax.broadcasted_iota(jnp.int32, (2 * t, t), 1)
              < (lax.broadcasted_iota(jnp.int32, (2 * t, t), 0) & (t - 1)))

    def keys(kb):
        return pl.ds(kb * t, t) if isinstance(kb, int) else pl.ds(pl.multiple_of(kb * t, t), t)

    def logits(i, kb, slot, diagonal=False):
        k = k_ref[0, keys(kb), :]
        z = _dot_nt(q2_ref[i], k)
        log_beta = jnp.minimum(z, 0.0) - jnp.log(1.0 + jnp.exp(-jnp.abs(z)))
        log_keep = log_beta - z
        if diagonal:
            log_beta = jnp.where(causal, log_beta, MASKED)
            log_keep = jnp.where(causal, log_keep, 0.0)
        lb_ref[slot] = log_beta
        lk_ref[slot] = log_keep.astype(BF16)

    def weigh(i, kb, slot):
        v = v_ref[0, keys(kb), :]
        log_keep = lk_ref[slot]
        sums = _dot(log_keep, suffix)
        w = jnp.exp(lb_ref[slot] + sums + carry_ref[i])
        carry_ref[i] = carry_ref[i] + (sums[:, 0:1] + log_keep[:, 0:1].astype(F32))
        zero_v = jnp.zeros_like(v)
        v2 = jnp.concatenate([jnp.where(head_a, v, zero_v), jnp.where(head_a, zero_v, v)], axis=0)
        w2 = jnp.concatenate([w[:t], w[t:]], axis=1).astype(BF16)
        acc_ref[i] += _dot(w2, v2)

    tiles = q_ref.shape[1] // t
    runs = [min(i + 1, SB_STRAIGHT) for i in range(tiles)]

    for i in range(tiles):
        q = q_ref[0, i * t:(i + 1) * t, :]
        zero_q = jnp.zeros_like(q)
        q2_ref[i] = jnp.concatenate([jnp.where(head_a, q, zero_q), jnp.where(head_a, zero_q, q)], axis=0)
    acc_ref[...] = jnp.zeros_like(acc_ref)
    carry_ref[...] = jnp.zeros_like(carry_ref)

    logits(0, 0, 0, diagonal=True)
    g = 0
    for i in range(tiles):
        for d in range(runs[i]):
            weigh(i, i - d, g % 2)
            if d + 1 < runs[i]:
                logits(i, i - d - 1, (g + 1) % 2)
            elif i + 1 < tiles:
                logits(i + 1, i + 1, (g + 1) % 2, diagonal=True)
            g += 1

    for i in range(tiles):
        if i >= runs[i]:
            def alive(kb, i=i):
                return jnp.logical_and(kb >= 0, jnp.max(carry_ref[i]) > SB_DEAD).astype(jnp.int32)

            def step(state, i=i, alive=alive):
                kb, _ = state
                logits(i, kb, 0)
                weigh(i, kb, 0)
                return kb - 1, alive(kb - 1)

            first = jnp.int32(i - runs[i])
            lax.while_loop(lambda s: s[1] > 0, step, (first, alive(first)))
        o_ref[0, i * t:(i + 1) * t, :] = acc_ref[i].astype(o_ref.dtype)


def _sb_attention(q, k, v):
    bsz, seq, _ = q.shape
    t = SB_TILE
    pairs = SB_W // LANES
    whole = pl.BlockSpec((1, seq, LANES), lambda b, p: (b, 0, p))
    return pl.pallas_call(
        _sb_kernel,
        grid=(bsz, pairs),
        in_specs=[whole, whole, whole],
        out_specs=whole,
        out_shape=jax.ShapeDtypeStruct((bsz, seq, SB_W), BF16),
        scratch_shapes=[pltpu.VMEM((seq // t, t, LANES), F32), pltpu.VMEM((seq // t, 2 * t, 1), F32),
                        pltpu.VMEM((2, 2 * t, t), F32), pltpu.VMEM((2, 2 * t, t), BF16),
                        pltpu.VMEM((seq // t, 2 * t, LANES), BF16)],
        compiler_params=_params("parallel", "parallel"),
        name="stickbreak_attn",
    )(q, k, v)


def _gdn_kernel(qkv_ref, z_ref, ab_ref, abt_ref, prow_ref, pcol_ref, gout_ref, o_ref,
                state_ref, lhs_ref, add_ref, gam_ref, *, steps_per_seq):
    ts = DN_STEP
    chunks = ts // DN_CHUNK
    j = pl.program_id(0)

    @pl.when(j == 0)
    def _():
        lhs_ref[...] = jnp.zeros_like(lhs_ref)
        add_ref[...] = jnp.zeros_like(add_ref)
        gam_ref[...] = jnp.zeros_like(gam_ref)

    @pl.when(jnp.logical_or(j == 0, (j + steps_per_seq - 1) % steps_per_seq == 0))
    def _():
        state_ref[...] = jnp.zeros_like(state_ref)

    _gdn_scan(z_ref, gout_ref, o_ref, state_ref, lhs_ref, add_ref, gam_ref)

    tb = 2 * DN_CHUNK
    ri = lax.broadcasted_iota(jnp.int32, (tb, tb), 0)
    ci = lax.broadcasted_iota(jnp.int32, (tb, tb), 1)
    same_chunk = (ri >> DN_CHUNK_LOG2) == (ci >> DN_CHUNK_LOG2)
    m_incl = same_chunk & (ci <= ri)
    m_strict = same_chunk & (ci < ri)
    eye = jnp.where(ri == ci, 1.0, 0.0)
    tri = jnp.where(m_incl, 1.0, 0.0).astype(BF16)
    tri_t = jnp.where(same_chunk & (ri <= ci), 1.0, 0.0).astype(BF16)
    units = [(h, r) for h in range(DN_HEADS) for r in range(ts // tb)]
    rows_of = lambda r: slice(r * tb, (r + 1) * tb)

    ab = ab_ref[...]
    beta_all = _sigmoid(ab)
    g_all = -jnp.exp(prow_ref[0:1, :]) * _softplus(ab + prow_ref[1:2, :])
    abt = abt_ref[0]
    g_rows = -jnp.exp(pcol_ref[:, 0:1]) * _softplus(abt + pcol_ref[:, 1:2])
    g_all3 = _split_bf16(g_all, 3)
    g_rows3 = _split_bf16(g_rows, 3)
    gcum_all = [sum(_dot(tri, p[rows_of(r)]) for p in g_all3) for r in range(ts // tb)]
    gcum_rows = [sum(_dot(p[:, rows_of(r)], tri_t) for p in g_rows3) for r in range(ts // tb)]

    def col(h, part):
        return slice(part * DN_W + h * DN_DIM, part * DN_W + (h + 1) * DN_DIM)

    q = {u: qkv_ref[rows_of(u[1]), col(u[0], 0)] for u in units}
    k = {u: qkv_ref[rows_of(u[1]), col(u[0], 1)] for u in units}
    v = {u: qkv_ref[rows_of(u[1]), col(u[0], 2)] for u in units}
    beta = {(h, r): beta_all[rows_of(r), h:h + 1] for h, r in units}
    gc = {(h, r): gcum_all[r][:, DN_HEADS + h:DN_HEADS + h + 1] for h, r in units}
    gr = {(h, r): gcum_rows[r][DN_HEADS + h:DN_HEADS + h + 1, :] for h, r in units}
    decay = {u: jnp.where(m_incl, jnp.exp(jnp.where(m_incl, gc[u] - gr[u], 0.0)), 0.0) for u in units}
    kb = {u: k[u] * beta[u] for u in units}
    vb = {u: v[u] * beta[u] for u in units}
    eg = {u: jnp.exp(gc[u]) for u in units}
    qg = {u: q[u] * eg[u] for u in units}
    kq = {u: _dot_nt(jnp.concatenate([kb[u], q[u]], axis=0).astype(BF16), k[u].astype(BF16)) for u in units}
    lower = {u: jnp.where(m_strict, kq[u][:tb] * decay[u], 0.0) for u in units}
    a_qk = {u: (kq[u][tb:] * decay[u]).astype(BF16) for u in units}

    blk8 = (ri >> DN_BASE_LOG2) == (ci >> DN_BASE_LOG2)
    m8 = {u: jnp.where(blk8, -lower[u], 0.0).astype(BF16) for u in units}
    inv = {u: eye + m8[u].astype(F32) for u in units}
    p = {u: _dot(m8[u], m8[u]).astype(BF16) for u in units}
    inv = {u: inv[u] + _dot(inv[u].astype(BF16), p[u]) for u in units}
    p = {u: _dot(p[u], p[u]).astype(BF16) for u in units}
    inv = {u: inv[u] + _dot(inv[u].astype(BF16), p[u]) for u in units}
    for lg in range(DN_BASE_LOG2, DN_CHUNK_LOG2):
        off = ((ri >> (lg + 1)) == (ci >> (lg + 1))) & (((ri >> lg) & 1) == 1) & (((ci >> lg) & 1) == 0)
        inv16 = {u: inv[u].astype(BF16) for u in units}
        cb = {u: _dot(jnp.where(off, lower[u], 0.0).astype(BF16), inv16[u]).astype(BF16) for u in units}
        inv = {u: inv[u] - _dot(inv16[u], cb[u]) for u in units}
    sol = {u: _dot(inv[u].astype(BF16), jnp.concatenate([vb[u], kb[u] * eg[u]], axis=1).astype(BF16))
           for u in units}

    a_sol = {u: _dot(a_qk[u], sol[u].astype(BF16)) for u in units}
    for u in units:
        h, r = u
        q_eff = (qg[u] - a_sol[u][:, DN_DIM:]).astype(BF16)
        for cc in range(tb // DN_CHUNK):
            c = r * (tb // DN_CHUNK) + cc
            rows = slice(cc * DN_CHUNK, (cc + 1) * DN_CHUNK)
            g_last = gc[u][(cc + 1) * DN_CHUNK - 1:(cc + 1) * DN_CHUNK, :]
            k_dec = k[u][rows] * jnp.exp(g_last - gc[u][rows])
            kd_sol = _dot_tn(k_dec.astype(BF16), sol[u][rows].astype(BF16))
            lhs_ref[h, c, 0:DN_CHUNK, :] = q_eff[rows]
            lhs_ref[h, c, DN_CHUNK:, :] = kd_sol[:, DN_DIM:].astype(BF16)
            add_ref[h, c, 0:DN_CHUNK, :] = a_sol[u][rows, :DN_DIM]
            add_ref[h, c, DN_CHUNK:, :] = kd_sol[:, :DN_DIM]
            gam_ref[h * chunks + c:h * chunks + c + 1, :] = jnp.broadcast_to(jnp.exp(g_last), (1, LANES))


def _gdn_scan(z_ref, gout_ref, o_ref, state_ref, lhs_ref, add_ref, gam_ref):
    chunks = DN_STEP // DN_CHUNK
    out = [[] for _ in range(DN_HEADS)]
    for c in range(chunks):
        for h in range(DN_HEADS):
            state = state_ref[h]
            res = _dot(lhs_ref[h, c], state.astype(BF16))
            add = add_ref[h, c]
            out[h].append(res[:DN_CHUNK] + add[:DN_CHUNK])
            gamma = gam_ref[h * chunks + c:h * chunks + c + 1, :]
            state_ref[h] = state * gamma - res[DN_CHUNK:] + add[DN_CHUNK:]
    for h in range(DN_HEADS):
        o = jnp.concatenate(out[h], axis=0)
        o = o * lax.rsqrt(jnp.mean(o * o, axis=-1, keepdims=True) + EPS) * gout_ref[...]
        zh = z_ref[:, h * DN_DIM:(h + 1) * DN_DIM]
        o_ref[:, h * DN_DIM:(h + 1) * DN_DIM] = (o * (zh * _sigmoid(zh))).astype(o_ref.dtype)


def _gated_deltanet(qkv, z, ab, abt, prow, pcol, g_out, seq):
    tokens = qkv.shape[0]
    ts = DN_STEP
    steps = seq // ts
    total = tokens // ts
    chunks = ts // DN_CHUNK
    fill_step = lambda j: jnp.minimum(j, total - 1)
    drain_step = lambda j: jnp.maximum(j - 1, 0)
    return pl.pallas_call(
        functools.partial(_gdn_kernel, steps_per_seq=steps),
        grid=(total + 1,),
        in_specs=[pl.BlockSpec((ts, 3 * DN_W), lambda j: (fill_step(j), 0)),
                  pl.BlockSpec((ts, DN_W), lambda j: (drain_step(j), 0)),
                  pl.BlockSpec((ts, LANES), lambda j: (fill_step(j), 0)),
                  pl.BlockSpec((1, 2 * SUBLANES, ts), lambda j: (fill_step(j) // steps, 0, fill_step(j) % steps)),
                  pl.BlockSpec((2, LANES), lambda j: (0, 0)),
                  pl.BlockSpec((2 * SUBLANES, LANES), lambda j: (0, 0)),
                  pl.BlockSpec((1, DN_DIM), lambda j: (0, 0))],
        out_specs=pl.BlockSpec((ts, DN_W), lambda j: (drain_step(j), 0)),
        out_shape=jax.ShapeDtypeStruct((tokens, DN_W), BF16),
        scratch_shapes=[pltpu.VMEM((DN_HEADS, DN_DIM, DN_DIM), F32),
                        pltpu.VMEM((DN_HEADS, chunks, DN_CHUNK + DN_DIM, DN_DIM), BF16),
                        pltpu.VMEM((DN_HEADS, chunks, DN_CHUNK + DN_DIM, DN_DIM), F32),
                        pltpu.VMEM((DN_HEADS * chunks, LANES), F32)],
        compiler_params=_params("arbitrary"),
        name="gated_deltanet",
    )(qkv, z, ab, abt, prow, pcol, g_out.reshape(1, DN_DIM))


def _merge_kernel(h_ref, attn_ref, dn_ref, mod_ref, g_ref, wr_ref, wus_ref, wud_ref, wo_ref, out_ref,
                  u_ref, m_ref):
    u_ref[...] = _rms_mod(h_ref[...], g_ref[...], mod_ref[0, 3:4, :], mod_ref[0, 4:5, :]).astype(BF16)
    for n in range(D_MODEL // MXU_DIM):
        cols = slice(n * MXU_DIM, (n + 1) * MXU_DIM)
        r_sb = _dot(u_ref[...], wr_ref[:, _PROJ_R + n * MXU_DIM:_PROJ_R + (n + 1) * MXU_DIM])
        r_dn = _dot(u_ref[...], wr_ref[:, _PROJ_R + D_MODEL + n * MXU_DIM:_PROJ_R + D_MODEL + (n + 1) * MXU_DIM])
        y_sb = _dot(attn_ref[...], wus_ref[:, cols].astype(BF16))
        y_dn = _dot(dn_ref[...], wud_ref[:, cols].astype(BF16))
        m_ref[:, cols] = (_sigmoid(r_sb) * y_sb + _sigmoid(r_dn) * y_dn).astype(BF16)
    for n in range(D_MODEL // MXU_DIM):
        cols = slice(n * MXU_DIM, (n + 1) * MXU_DIM)
        out_ref[:, cols] = h_ref[:, cols] + mod_ref[0, 5:6, cols] * _dot(m_ref[...], wo_ref[:, cols].astype(BF16))


def _merge(h2, attn, dn, mod3, g_mix, w_pack, w_up_sb, w_up_dn, w_out, seq):
    tokens = h2.shape[0]
    tm = TOKEN_TILE
    tiles_per_seq = seq // tm
    row = lambda w: pl.BlockSpec((tm, w), lambda i: (i, 0))
    return pl.pallas_call(
        _merge_kernel,
        grid=(tokens // tm,),
        in_specs=[row(D_MODEL), row(SB_W), row(DN_W),
                  pl.BlockSpec((1, N_MOD, D_MODEL), lambda i: (i // tiles_per_seq, 0, 0)),
                  _resident((1, D_MODEL), lambda i: (0, 0)),
                  _resident((D_MODEL, _PROJ_W), lambda i: (0, 0)),
                  _resident((SB_W, D_MODEL), lambda i: (0, 0)),
                  _resident((DN_W, D_MODEL), lambda i: (0, 0)),
                  _resident((D_MODEL, D_MODEL), lambda i: (0, 0))],
        out_specs=row(D_MODEL),
        out_shape=jax.ShapeDtypeStruct((tokens, D_MODEL), F32),
        scratch_shapes=[pltpu.VMEM((tm, D_MODEL), BF16), pltpu.VMEM((tm, D_MODEL), BF16)],
        compiler_params=_params("parallel"),
        name="mixer_merge",
    )(h2, attn, dn, mod3, g_mix.reshape(1, D_MODEL), w_pack, w_up_sb, w_up_dn, w_out)


def _layer(h2, mod3, seq, layer, g_ffn1, w_ffn1_in, w_ffn1_out, g_mix, w_in, g_q_sb, g_k_sb, w_conv, a_log,
           dt_bias, g_dn_out, w_up_sb, w_up_dn, w_out, g_ffn2, w_ffn2_in, w_ffn2_out):
    tokens = h2.shape[0]
    bsz = tokens // seq
    h2 = _ffn(h2, mod3, g_ffn1, w_ffn1_in, w_ffn1_out, 0, seq)

    w_pack = w_in[layer].astype(BF16)
    w_gate = w_in[layer, :, _PROJ_GATE:_PROJ_R]
    w_abt = jnp.pad(w_gate.T, ((0, 2 * SUBLANES - 2 * DN_HEADS), (0, 0))).astype(BF16)
    gq_row = jnp.tile(g_q_sb, SB_HEADS).reshape(1, SB_W)
    gk_row = jnp.tile(g_k_sb, SB_HEADS).reshape(1, SB_W)
    q, k, v, qkv_dn, z_dn, ab, abt = _projection(h2, mod3, g_mix, w_pack, w_abt, gq_row, gk_row, w_conv, seq)

    attn = _sb_attention(q.reshape(bsz, seq, SB_W), k.reshape(bsz, seq, SB_W), v.reshape(bsz, seq, SB_W))

    decay_params = jnp.stack([a_log, dt_bias])
    prow = jnp.pad(decay_params, ((0, 0), (DN_HEADS, LANES - 2 * DN_HEADS)))
    pcol = jnp.pad(decay_params.T, ((DN_HEADS, 2 * SUBLANES - 2 * DN_HEADS), (0, LANES - 2)))
    dn = _gated_deltanet(qkv_dn, z_dn, ab, abt, prow, pcol, g_dn_out, seq)

    h2 = _merge(h2, attn.reshape(tokens, SB_W), dn, mod3, g_mix, w_pack,
                w_up_sb, w_up_dn, w_out, seq)
    return _ffn(h2, mod3, g_ffn2, w_ffn2_in, w_ffn2_out, 6, seq)


def kernel(x, c, w_ada, b_ada, g_ffn1, w_ffn1_in, w_ffn1_out, g_mix, w_in, g_q_sb, g_k_sb, w_conv, a_log,
           dt_bias, g_dn_out, w_up_sb, w_up_dn, w_out, g_ffn2, w_ffn2_in, w_ffn2_out):
    bsz, seq, d = x.shape
    depth = w_ada.shape[0]
    h2 = x.reshape(bsz * seq, d)
    for l in range(depth):
        mod3 = _modulation(c, w_ada[l], b_ada[l]).reshape(bsz, N_MOD, d)
        h2 = _layer(h2, mod3, seq, l, g_ffn1[l], w_ffn1_in[l], w_ffn1_out[l], g_mix[l], w_in, g_q_sb[l],
                    g_k_sb[l], w_conv[l], a_log[l], dt_bias[l], g_dn_out[l], w_up_sb[l], w_up_dn[l],
                    w_out[l], g_ffn2[l], w_ffn2_in[l], w_ffn2_out[l])
    return h2.reshape(bsz, seq, d).astype(x.dtype)
```

```python
import functools

import jax
import jax.numpy as jnp
from jax import lax
from jax.experimental import pallas as pl
from jax.experimental.pallas import tpu as pltpu

F32 = jnp.float32
BF16 = jnp.bfloat16

D_MODEL = 1024
D_FF = 2816
N_MOD = 9
EPS = 1e-6

SB_HEADS = 8
SB_HEAD_DIM = 64
SB_W = SB_HEADS * SB_HEAD_DIM
DN_HEADS = 4
DN_DIM = 128
DN_W = DN_HEADS * DN_DIM
DN_CONV = 4
DN_CHUNK = 64

LANES = 128
SUBLANES = 8
MXU_DIM = 256
VMEM_LIMIT = 56 * 1024 * 1024

ADA_TILE = 3 * D_MODEL
TOKEN_TILE = 1024
FFN_TILE = 512
FF_CHUNK = MXU_DIM
SB_TILE = MXU_DIM
MASKED = -1e30
SB_DEAD = -104.0
SB_STRAIGHT = 3
DN_STEP = 8 * DN_CHUNK
DN_CHUNK_LOG2 = DN_CHUNK.bit_length() - 1
DN_BASE_LOG2 = 3

_dot = functools.partial(jnp.dot, preferred_element_type=F32)


def _dot_nt(a, b):
    return lax.dot_general(a, b, (((1,), (1,)), ((), ())), preferred_element_type=F32)


def _dot_tn(a, b):
    return lax.dot_general(a, b, (((0,), (0,)), ((), ())), preferred_element_type=F32)


def _sigmoid(x):
    return 1.0 / (1.0 + jnp.exp(-x))


def _softplus(x):
    return jnp.maximum(x, 0.0) + jnp.log(1.0 + jnp.exp(-jnp.abs(x)))


def _split_bf16(x, pieces):
    out = []
    for _ in range(pieces - 1):
        p = x.astype(BF16)
        out.append(p)
        x = x - p.astype(F32)
    out.append(x.astype(BF16))
    return out


def _rms_mod(x, gain, shift, scale):
    y = x * lax.rsqrt(jnp.mean(x * x, axis=-1, keepdims=True) + EPS)
    return (y * gain) * (1.0 + scale) + shift


def _resident(shape, index_map):
    return pl.BlockSpec(shape, index_map, pipeline_mode=pl.Buffered(1))


def _params(*semantics):
    return pltpu.CompilerParams(dimension_semantics=semantics, vmem_limit_bytes=VMEM_LIMIT)


def _mod_kernel(c_ref, w_ref, b_ref, o_ref):
    c = c_ref[...]
    cond = (c * _sigmoid(c)).astype(BF16)
    o_ref[...] = _dot(cond, w_ref[...].astype(BF16)) + b_ref[...]


def _modulation(c, w_ada, b_ada):
    bsz = c.shape[0]
    n = w_ada.shape[1]
    tn = ADA_TILE
    return pl.pallas_call(
        _mod_kernel,
        grid=(n // tn,),
        in_specs=[pl.BlockSpec((bsz, D_MODEL), lambda j: (0, 0)),
                  pl.BlockSpec((D_MODEL, tn), lambda j: (0, j)),
                  pl.BlockSpec((1, tn), lambda j: (0, j))],
        out_specs=pl.BlockSpec((bsz, tn), lambda j: (0, j)),
        out_shape=jax.ShapeDtypeStruct((bsz, n), F32),
        compiler_params=_params("arbitrary"),
        name="adaln_mod",
    )(c, w_ada, b_ada.reshape(1, n))


def _ffn_kernel(x_ref, mod_ref, g_ref, wa_ref, wb_ref, wo_ref, out_ref, u_ref, h_ref, *, mod_row):
    shift = mod_ref[0, mod_row:mod_row + 1, :]
    scale = mod_ref[0, mod_row + 1:mod_row + 2, :]
    u_ref[...] = _rms_mod(x_ref[...], g_ref[...], shift, scale).astype(BF16)
    for c in range(D_FF // FF_CHUNK):
        cols = slice(c * FF_CHUNK, (c + 1) * FF_CHUNK)
        a = _dot(u_ref[...], wa_ref[:, cols].astype(BF16))
        b = _dot(u_ref[...], wb_ref[:, cols].astype(BF16))
        h_ref[:, cols] = (a * _sigmoid(a) * b).astype(BF16)
    for n in range(D_MODEL // MXU_DIM):
        cols = slice(n * MXU_DIM, (n + 1) * MXU_DIM)
        gate = mod_ref[0, mod_row + 2:mod_row + 3, cols]
        out_ref[:, cols] = x_ref[:, cols] + (0.5 * gate) * _dot(h_ref[...], wo_ref[:, cols].astype(BF16))


def _ffn(x2, mod3, gain, w_in, w_out, mod_row, seq):
    tokens = x2.shape[0]
    tm = FFN_TILE
    tiles_per_seq = seq // tm
    return pl.pallas_call(
        functools.partial(_ffn_kernel, mod_row=mod_row),
        grid=(tokens // tm,),
        in_specs=[pl.BlockSpec((tm, D_MODEL), lambda i: (i, 0)),
                  pl.BlockSpec((1, N_MOD, D_MODEL), lambda i: (i // tiles_per_seq, 0, 0)),
                  _resident((1, D_MODEL), lambda i: (0, 0)),
                  _resident((D_MODEL, D_FF), lambda i: (0, 0)),
                  _resident((D_MODEL, D_FF), lambda i: (0, 1)),
                  _resident((D_FF, D_MODEL), lambda i: (0, 0))],
        out_specs=pl.BlockSpec((tm, D_MODEL), lambda i: (i, 0)),
        out_shape=jax.ShapeDtypeStruct((tokens, D_MODEL), F32),
        scratch_shapes=[pltpu.VMEM((tm, D_MODEL), BF16), pltpu.VMEM((tm, D_FF), BF16)],
        compiler_params=_params("parallel"),
        name=f"ffn_row{mod_row}",
    )(x2, mod3, gain.reshape(1, D_MODEL), w_in, w_in, w_out)


_PROJ_Q, _PROJ_K, _PROJ_V = 0, SB_W, 2 * SB_W
_PROJ_DN = 3 * SB_W
_PROJ_Z = _PROJ_DN + 3 * DN_W
_PROJ_MAIN_W = _PROJ_Z + DN_W
_PROJ_GATE = _PROJ_MAIN_W
_PROJ_R = _PROJ_GATE + 2 * DN_HEADS
_PROJ_W = _PROJ_R + 2 * D_MODEL


def _proj_kernel(h_ref, mod_ref, g_ref, wm_ref, wabt_ref, gq_ref, gk_ref, grp_ref, wc_ref,
                 q_ref, k_ref, v_ref, dn_ref, z_ref, ab_ref, abt_ref, u_ref, xbuf_ref, *, tiles_per_seq):
    tm = TOKEN_TILE
    hist = SUBLANES
    u_ref[...] = _rms_mod(h_ref[...], g_ref[...], mod_ref[0, 3:4, :], mod_ref[0, 4:5, :]).astype(BF16)

    @pl.when(pl.program_id(0) % tiles_per_seq == 0)
    def _():
        xbuf_ref[0:hist, :] = jnp.zeros((hist, 3 * DN_W), F32)

    def head_norm(t, gain_row):
        sq = (t * t).astype(BF16)
        ss = jnp.concatenate(
            [_dot(sq[:, c * MXU_DIM:(c + 1) * MXU_DIM], grp_ref[c * MXU_DIM:(c + 1) * MXU_DIM, c * MXU_DIM:(c + 1) * MXU_DIM])
             for c in range(SB_W // MXU_DIM)], axis=1)
        return t * lax.rsqrt(ss * (1.0 / SB_HEAD_DIM) + EPS) * gain_row

    def conv_group(g):
        cols = slice(g * DN_DIM, (g + 1) * DN_DIM)
        acc = xbuf_ref[hist:hist + tm, cols] * wc_ref[DN_CONV - 1:DN_CONV, cols]
        for tap in range(DN_CONV - 1):
            lo = hist - (DN_CONV - 1) + tap
            acc = acc + xbuf_ref[lo:lo + tm, cols] * wc_ref[tap:tap + 1, cols]
        y = acc * _sigmoid(acc)
        if g < 2 * DN_HEADS:
            y = y * lax.rsqrt(jnp.sum(y * y, axis=-1, keepdims=True) + EPS)
        if g < DN_HEADS:
            y = y * (DN_DIM ** -0.5)
        dn_ref[:, cols] = y

    q = _dot(u_ref[...], wm_ref[:, _PROJ_Q:_PROJ_Q + SB_W])
    q_ref[...] = (head_norm(q, gq_ref[...]) * (SB_HEAD_DIM ** -0.5)).astype(BF16)
    k = _dot(u_ref[...], wm_ref[:, _PROJ_K:_PROJ_K + SB_W])
    k_ref[...] = head_norm(k, gk_ref[...]).astype(BF16)
    v_ref[...] = _dot(u_ref[...], wm_ref[:, _PROJ_V:_PROJ_V + SB_W]).astype(BF16)
    for c in range(3 * DN_W // MXU_DIM):
        cols = slice(c * MXU_DIM, (c + 1) * MXU_DIM)
        xbuf_ref[hist:hist + tm, cols] = _dot(
            u_ref[...], wm_ref[:, _PROJ_DN + c * MXU_DIM:_PROJ_DN + (c + 1) * MXU_DIM])
    for g in range(3 * DN_HEADS):
        conv_group(g)
    xbuf_ref[0:hist, :] = xbuf_ref[tm:tm + hist, :]
    z_ref[...] = _dot(u_ref[...], wm_ref[:, _PROJ_Z:_PROJ_Z + DN_W])
    gates = wm_ref[:, _PROJ_GATE:_PROJ_R]
    gates = jnp.concatenate([gates, jnp.zeros((D_MODEL, LANES - 2 * DN_HEADS), BF16)], axis=1)
    ab_ref[...] = _dot(u_ref[...], gates)
    abt_ref[0] = _dot_nt(wabt_ref[...], u_ref[...])


def _projection(h2, mod3, g_mix, w_pack, w_abt, gq_row, gk_row, w_conv, seq):
    tokens = h2.shape[0]
    bsz = tokens // seq
    tm = TOKEN_TILE
    tiles_per_seq = seq // tm
    grp = jnp.kron(jnp.eye(SB_HEADS, dtype=F32), jnp.ones((SB_HEAD_DIM, SB_HEAD_DIM), F32)).astype(BF16)
    row = lambda w: pl.BlockSpec((tm, w), lambda i: (i, 0))
    return pl.pallas_call(
        functools.partial(_proj_kernel, tiles_per_seq=tiles_per_seq),
        grid=(tokens // tm,),
        in_specs=[row(D_MODEL),
                  pl.BlockSpec((1, N_MOD, D_MODEL), lambda i: (i // tiles_per_seq, 0, 0)),
                  _resident((1, D_MODEL), lambda i: (0, 0)),
                  _resident((D_MODEL, _PROJ_W), lambda i: (0, 0)),
                  _resident((2 * SUBLANES, D_MODEL), lambda i: (0, 0)),
                  _resident((1, SB_W), lambda i: (0, 0)),
                  _resident((1, SB_W), lambda i: (0, 0)),
                  _resident((SB_W, SB_W), lambda i: (0, 0)),
                  _resident((DN_CONV, 3 * DN_W), lambda i: (0, 0))],
        out_specs=[row(SB_W), row(SB_W), row(SB_W), row(3 * DN_W), row(DN_W), row(LANES),
                   pl.BlockSpec((1, 2 * SUBLANES, tm), lambda i: (i // tiles_per_seq, 0, i % tiles_per_seq))],
        out_shape=[jax.ShapeDtypeStruct((tokens, SB_W), BF16),
                   jax.ShapeDtypeStruct((tokens, SB_W), BF16),
                   jax.ShapeDtypeStruct((tokens, SB_W), BF16),
                   jax.ShapeDtypeStruct((tokens, 3 * DN_W), F32),
                   jax.ShapeDtypeStruct((tokens, DN_W), F32),
                   jax.ShapeDtypeStruct((tokens, LANES), F32),
                   jax.ShapeDtypeStruct((bsz, 2 * SUBLANES, seq), F32)],
        scratch_shapes=[pltpu.VMEM((tm, D_MODEL), BF16), pltpu.VMEM((tm + SUBLANES, 3 * DN_W), F32)],
        compiler_params=_params("arbitrary"),
        name="mixer_in_proj",
    )(h2, mod3, g_mix.reshape(1, D_MODEL), w_pack, w_abt, gq_row, gk_row, grp, w_conv)


def _sb_kernel(q_ref, k_ref, v_ref, o_ref, acc_ref, carry_ref, lb_ref, lk_ref, q2_ref):
    t = SB_TILE
    head_a = lax.broadcasted_iota(jnp.int32, (t, LANES), 1) < SB_HEAD_DIM
    suffix = jnp.where(lax.broadcasted_iota(jnp.int32, (t, t), 0) > lax.broadcasted_iota(jnp.int32, (t, t), 1),
                       1.0, 0.0).astype(BF16)
    causal = (lax.broadcasted_iota(jnp.int32, (2 * t, t), 1)
              < (lax.broadcasted_iota(jnp.int32, (2 * t, t), 0) & (t - 1)))

    def keys(kb):
        return pl.ds(kb * t, t) if isinstance(kb, int) else pl.ds(pl.multiple_of(kb * t, t), t)

    def logits(i, kb, slot, diagonal=False):
        k = k_ref[0, keys(kb), :]
        z = _dot_nt(q2_ref[i], k)
        log_beta = jnp.minimum(z, 0.0) - jnp.log(1.0 + jnp.exp(-jnp.abs(z)))
        log_keep = log_beta - z
        if diagonal:
            log_beta = jnp.where(causal, log_beta, MASKED)
            log_keep = jnp.where(causal, log_keep, 0.0)
        lb_ref[slot] = log_beta
        lk_ref[slot] = log_keep.astype(BF16)

    def weigh(i, kb, slot):
        v = v_ref[0, keys(kb), :]
        log_keep = lk_ref[slot]
        sums = _dot(log_keep, suffix)
        w = jnp.exp(lb_ref[slot] + sums + carry_ref[i])
        carry_ref[i] = carry_ref[i] + (sums[:, 0:1] + log_keep[:, 0:1].astype(F32))
        zero_v = jnp.zeros_like(v)
        v2 = jnp.concatenate([jnp.where(head_a, v, zero_v), jnp.where(head_a, zero_v, v)], axis=0)
        w2 = jnp.concatenate([w[:t], w[t:]], axis=1).astype(BF16)
        acc_ref[i] += _dot(w2, v2)

    tiles = q_ref.shape[1] // t
    runs = [min(i + 1, SB_STRAIGHT) for i in range(tiles)]

    for i in range(tiles):
        q = q_ref[0, i * t:(i + 1) * t, :]
        zero_q = jnp.zeros_like(q)
        q2_ref[i] = jnp.concatenate([jnp.where(head_a, q, zero_q), jnp.where(head_a, zero_q, q)], axis=0)
    acc_ref[...] = jnp.zeros_like(acc_ref)
    carry_ref[...] = jnp.zeros_like(carry_ref)

    logits(0, 0, 0, diagonal=True)
    g = 0
    for i in range(tiles):
        for d in range(runs[i]):
            weigh(i, i - d, g % 2)
            if d + 1 < runs[i]:
                logits(i, i - d - 1, (g + 1) % 2)
            elif i + 1 < tiles:
                logits(i + 1, i + 1, (g + 1) % 2, diagonal=True)
            g += 1

    for i in range(tiles):
        if i >= runs[i]:
            def alive(kb, i=i):
                return jnp.logical_and(kb >= 0, jnp.max(carry_ref[i]) > SB_DEAD).astype(jnp.int32)

            def step(state, i=i, alive=alive):
                kb, _ = state
                logits(i, kb, 0)
                weigh(i, kb, 0)
                return kb - 1, alive(kb - 1)

            first = jnp.int32(i - runs[i])
            lax.while_loop(lambda s: s[1] > 0, step, (first, alive(first)))
        o_ref[0, i * t:(i + 1) * t, :] = acc_ref[i].astype(o_ref.dtype)


def _sb_attention(q, k, v):
    bsz, seq, _ = q.shape
    t = SB_TILE
    pairs = SB_W // LANES
    whole = pl.BlockSpec((1, seq, LANES), lambda b, p: (b, 0, p))
    return pl.pallas_call(
        _sb_kernel,
        grid=(bsz, pairs),
        in_specs=[whole, whole, whole],
        out_specs=whole,
        out_shape=jax.ShapeDtypeStruct((bsz, seq, SB_W), BF16),
        scratch_shapes=[pltpu.VMEM((seq // t, t, LANES), F32), pltpu.VMEM((seq // t, 2 * t, 1), F32),
                        pltpu.VMEM((2, 2 * t, t), F32), pltpu.VMEM((2, 2 * t, t), BF16),
                        pltpu.VMEM((seq // t, 2 * t, LANES), BF16)],
        compiler_params=_params("parallel", "parallel"),
        name="stickbreak_attn",
    )(q, k, v)


def _gdn_kernel(qkv_ref, z_ref, ab_ref, abt_ref, prow_ref, pcol_ref, gout_ref, o_ref,
                state_ref, lhs_ref, add_ref, gam_ref, *, steps_per_seq):
    ts = DN_STEP
    chunks = ts // DN_CHUNK
    j = pl.program_id(0)

    @pl.when(j == 0)
    def _():
        lhs_ref[...] = jnp.zeros_like(lhs_ref)
        add_ref[...] = jnp.zeros_like(add_ref)
        gam_ref[...] = jnp.zeros_like(gam_ref)

    @pl.when(jnp.logical_or(j == 0, (j + steps_per_seq - 1) % steps_per_seq == 0))
    def _():
        state_ref[...] = jnp.zeros_like(state_ref)

    _gdn_scan(z_ref, gout_ref, o_ref, state_ref, lhs_ref, add_ref, gam_ref)

    tb = 2 * DN_CHUNK
    ri = lax.broadcasted_iota(jnp.int32, (tb, tb), 0)
    ci = lax.broadcasted_iota(jnp.int32, (tb, tb), 1)
    same_chunk = (ri >> DN_CHUNK_LOG2) == (ci >> DN_CHUNK_LOG2)
    m_incl = same_chunk & (ci <= ri)
    m_strict = same_chunk & (ci < ri)
    eye = jnp.where(ri == ci, 1.0, 0.0)
    tri = jnp.where(m_incl, 1.0, 0.0).astype(BF16)
    tri_t = jnp.where(same_chunk & (ri <= ci), 1.0, 0.0).astype(BF16)
    units = [(h, r) for h in range(DN_HEADS) for r in range(ts // tb)]
    rows_of = lambda r: slice(r * tb, (r + 1) * tb)

    ab = ab_ref[...]
    beta_all = _sigmoid(ab)
    g_all = -jnp.exp(prow_ref[0:1, :]) * _softplus(ab + prow_ref[1:2, :])
    abt = abt_ref[0]
    g_rows = -jnp.exp(pcol_ref[:, 0:1]) * _softplus(abt + pcol_ref[:, 1:2])
    g_all3 = _split_bf16(g_all, 3)
    g_rows3 = _split_bf16(g_rows, 3)
    gcum_all = [sum(_dot(tri, p[rows_of(r)]) for p in g_all3) for r in range(ts // tb)]
    gcum_rows = [sum(_dot(p[:, rows_of(r)], tri_t) for p in g_rows3) for r in range(ts // tb)]

    def col(h, part):
        return slice(part * DN_W + h * DN_DIM, part * DN_W + (h + 1) * DN_DIM)

    q = {u: qkv_ref[rows_of(u[1]), col(u[0], 0)] for u in units}
    k = {u: qkv_ref[rows_of(u[1]), col(u[0], 1)] for u in units}
    v = {u: qkv_ref[rows_of(u[1]), col(u[0], 2)] for u in units}
    beta = {(h, r): beta_all[rows_of(r), h:h + 1] for h, r in units}
    gc = {(h, r): gcum_all[r][:, DN_HEADS + h:DN_HEADS + h + 1] for h, r in units}
    gr = {(h, r): gcum_rows[r][DN_HEADS + h:DN_HEADS + h + 1, :] for h, r in units}
    decay = {u: jnp.where(m_incl, jnp.exp(jnp.where(m_incl, gc[u] - gr[u], 0.0)), 0.0) for u in units}
    kb = {u: k[u] * beta[u] for u in units}
    vb = {u: v[u] * beta[u] for u in units}
    eg = {u: jnp.exp(gc[u]) for u in units}
    qg = {u: q[u] * eg[u] for u in units}
    kq = {u: _dot_nt(jnp.concatenate([kb[u], q[u]], axis=0).astype(BF16), k[u].astype(BF16)) for u in units}
    lower = {u: jnp.where(m_strict, kq[u][:tb] * decay[u], 0.0) for u in units}
    a_qk = {u: (kq[u][tb:] * decay[u]).astype(BF16) for u in units}

    blk8 = (ri >> DN_BASE_LOG2) == (ci >> DN_BASE_LOG2)
    m8 = {u: jnp.where(blk8, -lower[u], 0.0).astype(BF16) for u in units}
    inv = {u: eye + m8[u].astype(F32) for u in units}
    p = {u: _dot(m8[u], m8[u]).astype(BF16) for u in units}
    inv = {u: inv[u] + _dot(inv[u].astype(BF16), p[u]) for u in units}
    p = {u: _dot(p[u], p[u]).astype(BF16) for u in units}
    inv = {u: inv[u] + _dot(inv[u].astype(BF16), p[u]) for u in units}
    for lg in range(DN_BASE_LOG2, DN_CHUNK_LOG2):
        off = ((ri >> (lg + 1)) == (ci >> (lg + 1))) & (((ri >> lg) & 1) == 1) & (((ci >> lg) & 1) == 0)
        inv16 = {u: inv[u].astype(BF16) for u in units}
        cb = {u: _dot(jnp.where(off, lower[u], 0.0).astype(BF16), inv16[u]).astype(BF16) for u in units}
        inv = {u: inv[u] - _dot(inv16[u], cb[u]) for u in units}
    sol = {u: _dot(inv[u].astype(BF16), jnp.concatenate([vb[u], kb[u] * eg[u]], axis=1).astype(BF16))
           for u in units}

    a_sol = {u: _dot(a_qk[u], sol[u].astype(BF16)) for u in units}
    for u in units:
        h, r = u
        q_eff = (qg[u] - a_sol[u][:, DN_DIM:]).astype(BF16)
        for cc in range(tb // DN_CHUNK):
            c = r * (tb // DN_CHUNK) + cc
            rows = slice(cc * DN_CHUNK, (cc + 1) * DN_CHUNK)
            g_last = gc[u][(cc + 1) * DN_CHUNK - 1:(cc + 1) * DN_CHUNK, :]
            k_dec = k[u][rows] * jnp.exp(g_last - gc[u][rows])
            kd_sol = _dot_tn(k_dec.astype(BF16), sol[u][rows].astype(BF16))
            lhs_ref[h, c, 0:DN_CHUNK, :] = q_eff[rows]
            lhs_ref[h, c, DN_CHUNK:, :] = kd_sol[:, DN_DIM:].astype(BF16)
            add_ref[h, c, 0:DN_CHUNK, :] = a_sol[u][rows, :DN_DIM]
            add_ref[h, c, DN_CHUNK:, :] = kd_sol[:, :DN_DIM]
            gam_ref[h * chunks + c:h * chunks + c + 1, :] = jnp.broadcast_to(jnp.exp(g_last), (1, LANES))


def _gdn_scan(z_ref, gout_ref, o_ref, state_ref, lhs_ref, add_ref, gam_ref):
    chunks = DN_STEP // DN_CHUNK
    out = [[] for _ in range(DN_HEADS)]
    for c in range(chunks):
        for h in range(DN_HEADS):
            state = state_ref[h]
            res = _dot(lhs_ref[h, c], state.astype(BF16))
            add = add_ref[h, c]
            out[h].append(res[:DN_CHUNK] + add[:DN_CHUNK])
            gamma = gam_ref[h * chunks + c:h * chunks + c + 1, :]
            state_ref[h] = state * gamma - res[DN_CHUNK:] + add[DN_CHUNK:]
    for h in range(DN_HEADS):
        o = jnp.concatenate(out[h], axis=0)
        o = o * lax.rsqrt(jnp.mean(o * o, axis=-1, keepdims=True) + EPS) * gout_ref[...]
        zh = z_ref[:, h * DN_DIM:(h + 1) * DN_DIM]
        o_ref[:, h * DN_DIM:(h + 1) * DN_DIM] = (o * (zh * _sigmoid(zh))).astype(o_ref.dtype)


def _gated_deltanet(qkv, z, ab, abt, prow, pcol, g_out, seq):
    tokens = qkv.shape[0]
    ts = DN_STEP
    steps = seq // ts
    total = tokens // ts
    chunks = ts // DN_CHUNK
    fill_step = lambda j: jnp.minimum(j, total - 1)
    drain_step = lambda j: jnp.maximum(j - 1, 0)
    return pl.pallas_call(
        functools.partial(_gdn_kernel, steps_per_seq=steps),
        grid=(total + 1,),
        in_specs=[pl.BlockSpec((ts, 3 * DN_W), lambda j: (fill_step(j), 0)),
                  pl.BlockSpec((ts, DN_W), lambda j: (drain_step(j), 0)),
                  pl.BlockSpec((ts, LANES), lambda j: (fill_step(j), 0)),
                  pl.BlockSpec((1, 2 * SUBLANES, ts), lambda j: (fill_step(j) // steps, 0, fill_step(j) % steps)),
                  pl.BlockSpec((2, LANES), lambda j: (0, 0)),
                  pl.BlockSpec((2 * SUBLANES, LANES), lambda j: (0, 0)),
                  pl.BlockSpec((1, DN_DIM), lambda j: (0, 0))],
        out_specs=pl.BlockSpec((ts, DN_W), lambda j: (drain_step(j), 0)),
        out_shape=jax.ShapeDtypeStruct((tokens, DN_W), BF16),
        scratch_shapes=[pltpu.VMEM((DN_HEADS, DN_DIM, DN_DIM), F32),
                        pltpu.VMEM((DN_HEADS, chunks, DN_CHUNK + DN_DIM, DN_DIM), BF16),
                        pltpu.VMEM((DN_HEADS, chunks, DN_CHUNK + DN_DIM, DN_DIM), F32),
                        pltpu.VMEM((DN_HEADS * chunks, LANES), F32)],
        compiler_params=_params("arbitrary"),
        name="gated_deltanet",
    )(qkv, z, ab, abt, prow, pcol, g_out.reshape(1, DN_DIM))


def _merge_kernel(h_ref, attn_ref, dn_ref, mod_ref, g_ref, wr_ref, wus_ref, wud_ref, wo_ref, out_ref,
                  u_ref, m_ref):
    u_ref[...] = _rms_mod(h_ref[...], g_ref[...], mod_ref[0, 3:4, :], mod_ref[0, 4:5, :]).astype(BF16)
    for n in range(D_MODEL // MXU_DIM):
        cols = slice(n * MXU_DIM, (n + 1) * MXU_DIM)
        r_sb = _dot(u_ref[...], wr_ref[:, _PROJ_R + n * MXU_DIM:_PROJ_R + (n + 1) * MXU_DIM])
        r_dn = _dot(u_ref[...], wr_ref[:, _PROJ_R + D_MODEL + n * MXU_DIM:_PROJ_R + D_MODEL + (n + 1) * MXU_DIM])
        y_sb = _dot(attn_ref[...], wus_ref[:, cols].astype(BF16))
        y_dn = _dot(dn_ref[...], wud_ref[:, cols].astype(BF16))
        m_ref[:, cols] = (_sigmoid(r_sb) * y_sb + _sigmoid(r_dn) * y_dn).astype(BF16)
    for n in range(D_MODEL // MXU_DIM):
        cols = slice(n * MXU_DIM, (n + 1) * MXU_DIM)
        out_ref[:, cols] = h_ref[:, cols] + mod_ref[0, 5:6, cols] * _dot(m_ref[...], wo_ref[:, cols].astype(BF16))


def _merge(h2, attn, dn, mod3, g_mix, w_pack, w_up_sb, w_up_dn, w_out, seq):
    tokens = h2.shape[0]
    tm = TOKEN_TILE
    tiles_per_seq = seq // tm
    row = lambda w: pl.BlockSpec((tm, w), lambda i: (i, 0))
    return pl.pallas_call(
        _merge_kernel,
        grid=(tokens // tm,),
        in_specs=[row(D_MODEL), row(SB_W), row(DN_W),
                  pl.BlockSpec((1, N_MOD, D_MODEL), lambda i: (i // tiles_per_seq, 0, 0)),
                  _resident((1, D_MODEL), lambda i: (0, 0)),
                  _resident((D_MODEL, _PROJ_W), lambda i: (0, 0)),
                  _resident((SB_W, D_MODEL), lambda i: (0, 0)),
                  _resident((DN_W, D_MODEL), lambda i: (0, 0)),
                  _resident((D_MODEL, D_MODEL), lambda i: (0, 0))],
        out_specs=row(D_MODEL),
        out_shape=jax.ShapeDtypeStruct((tokens, D_MODEL), F32),
        scratch_shapes=[pltpu.VMEM((tm, D_MODEL), BF16), pltpu.VMEM((tm, D_MODEL), BF16)],
        compiler_params=_params("parallel"),
        name="mixer_merge",
    )(h2, attn, dn, mod3, g_mix.reshape(1, D_MODEL), w_pack, w_up_sb, w_up_dn, w_out)


def _layer(h2, mod3, seq, layer, g_ffn1, w_ffn1_in, w_ffn1_out, g_mix, w_in, g_q_sb, g_k_sb, w_conv, a_log,
           dt_bias, g_dn_out, w_up_sb, w_up_dn, w_out, g_ffn2, w_ffn2_in, w_ffn2_out):
    tokens = h2.shape[0]
    bsz = tokens // seq
    h2 = _ffn(h2, mod3, g_ffn1, w_ffn1_in, w_ffn1_out, 0, seq)

    w_pack = w_in[layer].astype(BF16)
    w_gate = w_in[layer, :, _PROJ_GATE:_PROJ_R]
    w_abt = jnp.pad(w_gate.T, ((0, 2 * SUBLANES - 2 * DN_HEADS), (0, 0))).astype(BF16)
    gq_row = jnp.tile(g_q_sb, SB_HEADS).reshape(1, SB_W)
    gk_row = jnp.tile(g_k_sb, SB_HEADS).reshape(1, SB_W)
    q, k, v, qkv_dn, z_dn, ab, abt = _projection(h2, mod3, g_mix, w_pack, w_abt, gq_row, gk_row, w_conv, seq)

    attn = _sb_attention(q.reshape(bsz, seq, SB_W), k.reshape(bsz, seq, SB_W), v.reshape(bsz, seq, SB_W))

    decay_params = jnp.stack([a_log, dt_bias])
    prow = jnp.pad(decay_params, ((0, 0), (DN_HEADS, LANES - 2 * DN_HEADS)))
    pcol = jnp.pad(decay_params.T, ((DN_HEADS, 2 * SUBLANES - 2 * DN_HEADS), (0, LANES - 2)))
    dn = _gated_deltanet(qkv_dn, z_dn, ab, abt, prow, pcol, g_dn_out, seq)

    h2 = _merge(h2, attn.reshape(tokens, SB_W), dn, mod3, g_mix, w_pack,
                w_up_sb, w_up_dn, w_out, seq)
    return _ffn(h2, mod3, g_ffn2, w_ffn2_in, w_ffn2_out, 6, seq)


def kernel(x, c, w_ada, b_ada, g_ffn1, w_ffn1_in, w_ffn1_out, g_mix, w_in, g_q_sb, g_k_sb, w_conv, a_log,
           dt_bias, g_dn_out, w_up_sb, w_up_dn, w_out, g_ffn2, w_ffn2_in, w_ffn2_out):
    bsz, seq, d = x.shape
    depth = w_ada.shape[0]
    h2 = x.reshape(bsz * seq, d)
    for l in range(depth):
        mod3 = _modulation(c, w_ada[l], b_ada[l]).reshape(bsz, N_MOD, d)
        h2 = _layer(h2, mod3, seq, l, g_ffn1[l], w_ffn1_in[l], w_ffn1_out[l], g_mix[l], w_in, g_q_sb[l],
                    g_k_sb[l], w_conv[l], a_log[l], dt_bias[l], g_dn_out[l], w_up_sb[l], w_up_dn[l],
                    w_out[l], g_ffn2[l], w_ffn2_in[l], w_ffn2_out[l])
    return h2.reshape(bsz, seq, d).astype(x.dtype)
```

```python
import functools

import jax
import jax.numpy as jnp
from jax import lax
from jax.experimental import pallas as pl
from jax.experimental.pallas import tpu as pltpu

F32 = jnp.float32
BF16 = jnp.bfloat16

D_MODEL = 1024
D_FF = 2816
N_MOD = 9
EPS = 1e-6

SB_HEADS = 8
SB_HEAD_DIM = 64
SB_W = SB_HEADS * SB_HEAD_DIM
DN_HEADS = 4
DN_DIM = 128
DN_W = DN_HEADS * DN_DIM
DN_CONV = 4
DN_CHUNK = 64

LANES = 128
SUBLANES = 8
MXU_DIM = 256
VMEM_LIMIT = 56 * 1024 * 1024

ADA_TILE = 3 * D_MODEL
TOKEN_TILE = 1024
FFN_TILE = 512
FF_CHUNK = MXU_DIM
SB_TILE = MXU_DIM
MASKED = -1e30
SB_DEAD = -104.0
SB_STRAIGHT = 3
DN_STEP = 8 * DN_CHUNK
DN_CHUNK_LOG2 = DN_CHUNK.bit_length() - 1
DN_BASE_LOG2 = 3

_dot = functools.partial(jnp.dot, preferred_element_type=F32)


def _dot_nt(a, b):
    return lax.dot_general(a, b, (((1,), (1,)), ((), ())), preferred_element_type=F32)


def _dot_tn(a, b):
    return lax.dot_general(a, b, (((0,), (0,)), ((), ())), preferred_element_type=F32)


def _sigmoid(x):
    return 1.0 / (1.0 + jnp.exp(-x))


def _softplus(x):
    return jnp.maximum(x, 0.0) + jnp.log(1.0 + jnp.exp(-jnp.abs(x)))


def _split_bf16(x, pieces):
    out = []
    for _ in range(pieces - 1):
        p = x.astype(BF16)
        out.append(p)
        x = x - p.astype(F32)
    out.append(x.astype(BF16))
    return out


def _rms_mod(x, gain, shift, scale):
    y = x * lax.rsqrt(jnp.mean(x * x, axis=-1, keepdims=True) + EPS)
    return (y * gain) * (1.0 + scale) + shift


def _resident(shape, index_map):
    return pl.BlockSpec(shape, index_map, pipeline_mode=pl.Buffered(1))


def _params(*semantics):
    return pltpu.CompilerParams(dimension_semantics=semantics, vmem_limit_bytes=VMEM_LIMIT)


def _mod_kernel(c_ref, w_ref, b_ref, o_ref):
    c = c_ref[...]
    cond = (c * _sigmoid(c)).astype(BF16)
    o_ref[...] = _dot(cond, w_ref[...].astype(BF16)) + b_ref[...]


def _modulation(c, w_ada, b_ada):
    bsz = c.shape[0]
    n = w_ada.shape[1]
    tn = ADA_TILE
    return pl.pallas_call(
        _mod_kernel,
        grid=(n // tn,),
        in_specs=[pl.BlockSpec((bsz, D_MODEL), lambda j: (0, 0)),
                  pl.BlockSpec((D_MODEL, tn), lambda j: (0, j)),
                  pl.BlockSpec((1, tn), lambda j: (0, j))],
        out_specs=pl.BlockSpec((bsz, tn), lambda j: (0, j)),
        out_shape=jax.ShapeDtypeStruct((bsz, n), F32),
        compiler_params=_params("arbitrary"),
        name="adaln_mod",
    )(c, w_ada, b_ada.reshape(1, n))


def _ffn_kernel(x_ref, mod_ref, g_ref, wa_ref, wb_ref, wo_ref, out_ref, u_ref, h_ref, *, mod_row):
    shift = mod_ref[0, mod_row:mod_row + 1, :]
    scale = mod_ref[0, mod_row + 1:mod_row + 2, :]
    u_ref[...] = _rms_mod(x_ref[...], g_ref[...], shift, scale).astype(BF16)
    for c in range(D_FF // FF_CHUNK):
        cols = slice(c * FF_CHUNK, (c + 1) * FF_CHUNK)
        a = _dot(u_ref[...], wa_ref[:, cols].astype(BF16))
        b = _dot(u_ref[...], wb_ref[:, cols].astype(BF16))
        h_ref[:, cols] = (a * _sigmoid(a) * b).astype(BF16)
    for n in range(D_MODEL // MXU_DIM):
        cols = slice(n * MXU_DIM, (n + 1) * MXU_DIM)
        gate = mod_ref[0, mod_row + 2:mod_row + 3, cols]
        out_ref[:, cols] = x_ref[:, cols] + (0.5 * gate) * _dot(h_ref[...], wo_ref[:, cols].astype(BF16))


def _ffn(x2, mod3, gain, w_in, w_out, mod_row, seq):
    tokens = x2.shape[0]
    tm = FFN_TILE
    tiles_per_seq = seq // tm
    return pl.pallas_call(
        functools.partial(_ffn_kernel, mod_row=mod_row),
        grid=(tokens // tm,),
        in_specs=[pl.BlockSpec((tm, D_MODEL), lambda i: (i, 0)),
                  pl.BlockSpec((1, N_MOD, D_MODEL), lambda i: (i // tiles_per_seq, 0, 0)),
                  _resident((1, D_MODEL), lambda i: (0, 0)),
                  _resident((D_MODEL, D_FF), lambda i: (0, 0)),
                  _resident((D_MODEL, D_FF), lambda i: (0, 1)),
                  _resident((D_FF, D_MODEL), lambda i: (0, 0))],
        out_specs=pl.BlockSpec((tm, D_MODEL), lambda i: (i, 0)),
        out_shape=jax.ShapeDtypeStruct((tokens, D_MODEL), F32),
        scratch_shapes=[pltpu.VMEM((tm, D_MODEL), BF16), pltpu.VMEM((tm, D_FF), BF16)],
        compiler_params=_params("parallel"),
        name=f"ffn_row{mod_row}",
    )(x2, mod3, gain.reshape(1, D_MODEL), w_in, w_in, w_out)


_PROJ_Q, _PROJ_K, _PROJ_V = 0, SB_W, 2 * SB_W
_PROJ_DN = 3 * SB_W
_PROJ_Z = _PROJ_DN + 3 * DN_W
_PROJ_MAIN_W = _PROJ_Z + DN_W
_PROJ_GATE = _PROJ_MAIN_W
_PROJ_R = _PROJ_GATE + 2 * DN_HEADS
_PROJ_W = _PROJ_R + 2 * D_MODEL


def _proj_kernel(h_ref, mod_ref, g_ref, wm_ref, wabt_ref, gq_ref, gk_ref, grp_ref, wc_ref,
                 q_ref, k_ref, v_ref, dn_ref, z_ref, ab_ref, abt_ref, u_ref, xbuf_ref, *, tiles_per_seq):
    tm = TOKEN_TILE
    hist = SUBLANES
    u_ref[...] = _rms_mod(h_ref[...], g_ref[...], mod_ref[0, 3:4, :], mod_ref[0, 4:5, :]).astype(BF16)

    @pl.when(pl.program_id(0) % tiles_per_seq == 0)
    def _():
        xbuf_ref[0:hist, :] = jnp.zeros((hist, 3 * DN_W), F32)

    def head_norm(t, gain_row):
        sq = (t * t).astype(BF16)
        ss = jnp.concatenate(
            [_dot(sq[:, c * MXU_DIM:(c + 1) * MXU_DIM], grp_ref[c * MXU_DIM:(c + 1) * MXU_DIM, c * MXU_DIM:(c + 1) * MXU_DIM])
             for c in range(SB_W // MXU_DIM)], axis=1)
        return t * lax.rsqrt(ss * (1.0 / SB_HEAD_DIM) + EPS) * gain_row

    def conv_group(g):
        cols = slice(g * DN_DIM, (g + 1) * DN_DIM)
        acc = xbuf_ref[hist:hist + tm, cols] * wc_ref[DN_CONV - 1:DN_CONV, cols]
        for tap in range(DN_CONV - 1):
            lo = hist - (DN_CONV - 1) + tap
            acc = acc + xbuf_ref[lo:lo + tm, cols] * wc_ref[tap:tap + 1, cols]
        half = 0.5 * acc
        y = half + half * jnp.tanh(half)
        if g < 2 * DN_HEADS:
            y = y * lax.rsqrt(jnp.sum(y * y, axis=-1, keepdims=True) + EPS)
        if g < DN_HEADS:
            y = y * (DN_DIM ** -0.5)
        dn_ref[:, cols] = y

    q = _dot(u_ref[...], wm_ref[:, _PROJ_Q:_PROJ_Q + SB_W])
    q_ref[...] = (head_norm(q, gq_ref[...]) * (SB_HEAD_DIM ** -0.5)).astype(BF16)
    k = _dot(u_ref[...], wm_ref[:, _PROJ_K:_PROJ_K + SB_W])
    k_ref[...] = head_norm(k, gk_ref[...]).astype(BF16)
    v_ref[...] = _dot(u_ref[...], wm_ref[:, _PROJ_V:_PROJ_V + SB_W]).astype(BF16)
    for c in range(3 * DN_W // MXU_DIM):
        cols = slice(c * MXU_DIM, (c + 1) * MXU_DIM)
        xbuf_ref[hist:hist + tm, cols] = _dot(
            u_ref[...], wm_ref[:, _PROJ_DN + c * MXU_DIM:_PROJ_DN + (c + 1) * MXU_DIM])
    for g in range(3 * DN_HEADS):
        conv_group(g)
    xbuf_ref[0:hist, :] = xbuf_ref[tm:tm + hist, :]
    z_ref[...] = _dot(u_ref[...], wm_ref[:, _PROJ_Z:_PROJ_Z + DN_W])
    gates = wm_ref[:, _PROJ_GATE:_PROJ_R]
    gates = jnp.concatenate([gates, jnp.zeros((D_MODEL, LANES - 2 * DN_HEADS), BF16)], axis=1)
    ab_ref[...] = _dot(u_ref[...], gates)
    abt_ref[0] = _dot_nt(wabt_ref[...], u_ref[...])


def _projection(h2, mod3, g_mix, w_pack, w_abt, gq_row, gk_row, w_conv, seq):
    tokens = h2.shape[0]
    bsz = tokens // seq
    tm = TOKEN_TILE
    tiles_per_seq = seq // tm
    grp = jnp.kron(jnp.eye(SB_HEADS, dtype=F32), jnp.ones((SB_HEAD_DIM, SB_HEAD_DIM), F32)).astype(BF16)
    row = lambda w: pl.BlockSpec((tm, w), lambda i: (i, 0))
    return pl.pallas_call(
        functools.partial(_proj_kernel, tiles_per_seq=tiles_per_seq),
        grid=(tokens // tm,),
        in_specs=[row(D_MODEL),
                  pl.BlockSpec((1, N_MOD, D_MODEL), lambda i: (i // tiles_per_seq, 0, 0)),
                  _resident((1, D_MODEL), lambda i: (0, 0)),
                  _resident((D_MODEL, _PROJ_W), lambda i: (0, 0)),
                  _resident((2 * SUBLANES, D_MODEL), lambda i: (0, 0)),
                  _resident((1, SB_W), lambda i: (0, 0)),
                  _resident((1, SB_W), lambda i: (0, 0)),
                  _resident((SB_W, SB_W), lambda i: (0, 0)),
                  _resident((DN_CONV, 3 * DN_W), lambda i: (0, 0))],
        out_specs=[row(SB_W), row(SB_W), row(SB_W), row(3 * DN_W), row(DN_W), row(LANES),
                   pl.BlockSpec((1, 2 * SUBLANES, tm), lambda i: (i // tiles_per_seq, 0, i % tiles_per_seq))],
        out_shape=[jax.ShapeDtypeStruct((tokens, SB_W), BF16),
                   jax.ShapeDtypeStruct((tokens, SB_W), BF16),
                   jax.ShapeDtypeStruct((tokens, SB_W), BF16),
                   jax.ShapeDtypeStruct((tokens, 3 * DN_W), F32),
                   jax.ShapeDtypeStruct((tokens, DN_W), F32),
                   jax.ShapeDtypeStruct((tokens, LANES), F32),
                   jax.ShapeDtypeStruct((bsz, 2 * SUBLANES, seq), F32)],
        scratch_shapes=[pltpu.VMEM((tm, D_MODEL), BF16), pltpu.VMEM((tm + SUBLANES, 3 * DN_W), F32)],
        compiler_params=_params("arbitrary"),
        name="mixer_in_proj",
    )(h2, mod3, g_mix.reshape(1, D_MODEL), w_pack, w_abt, gq_row, gk_row, grp, w_conv)


def _sb_kernel(q_ref, k_ref, v_ref, o_ref, acc_ref, carry_ref, lb_ref, lk_ref, q2_ref):
    t = SB_TILE
    head_a = lax.broadcasted_iota(jnp.int32, (t, LANES), 1) < SB_HEAD_DIM
    suffix = jnp.where(lax.broadcasted_iota(jnp.int32, (t, t), 0) > lax.broadcasted_iota(jnp.int32, (t, t), 1),
                       1.0, 0.0).astype(BF16)
    causal = (lax.broadcasted_iota(jnp.int32, (2 * t, t), 1)
              < (lax.broadcasted_iota(jnp.int32, (2 * t, t), 0) & (t - 1)))

    def keys(kb):
        return pl.ds(kb * t, t) if isinstance(kb, int) else pl.ds(pl.multiple_of(kb * t, t), t)

    def logits(i, kb, slot, diagonal=False):
        k = k_ref[0, keys(kb), :]
        z = _dot_nt(q2_ref[i], k)
        log_beta = jnp.minimum(z, 0.0) - jnp.log(1.0 + jnp.exp(-jnp.abs(z)))
        log_keep = log_beta - z
        if diagonal:
            log_beta = jnp.where(causal, log_beta, MASKED)
            log_keep = jnp.where(causal, log_keep, 0.0)
        lb_ref[slot] = log_beta
        lk_ref[slot] = log_keep.astype(BF16)

    def weigh(i, kb, slot):
        v = v_ref[0, keys(kb), :]
        log_keep = lk_ref[slot]
        sums = _dot(log_keep, suffix)
        w = jnp.exp(lb_ref[slot] + sums + carry_ref[i])
        carry_ref[i] = carry_ref[i] + (sums[:, 0:1] + log_keep[:, 0:1].astype(F32))
        zero_v = jnp.zeros_like(v)
        v2 = jnp.concatenate([jnp.where(head_a, v, zero_v), jnp.where(head_a, zero_v, v)], axis=0)
        w2 = jnp.concatenate([w[:t], w[t:]], axis=1).astype(BF16)
        acc_ref[i] += _dot(w2, v2)

    tiles = q_ref.shape[1] // t
    runs = [min(i + 1, SB_STRAIGHT) for i in range(tiles)]

    for i in range(tiles):
        q = q_ref[0, i * t:(i + 1) * t, :]
        zero_q = jnp.zeros_like(q)
        q2_ref[i] = jnp.concatenate([jnp.where(head_a, q, zero_q), jnp.where(head_a, zero_q, q)], axis=0)
    acc_ref[...] = jnp.zeros_like(acc_ref)
    carry_ref[...] = jnp.zeros_like(carry_ref)

    logits(0, 0, 0, diagonal=True)
    g = 0
    for i in range(tiles):
        for d in range(runs[i]):
            weigh(i, i - d, g % 2)
            if d + 1 < runs[i]:
                logits(i, i - d - 1, (g + 1) % 2)
            elif i + 1 < tiles:
                logits(i + 1, i + 1, (g + 1) % 2, diagonal=True)
            g += 1

    for i in range(tiles):
        if i >= runs[i]:
            def alive(kb, i=i):
                return jnp.logical_and(kb >= 0, jnp.max(carry_ref[i]) > SB_DEAD).astype(jnp.int32)

            def step(state, i=i, alive=alive):
                kb, _ = state
                logits(i, kb, 0)
                weigh(i, kb, 0)
                return kb - 1, alive(kb - 1)

            first = jnp.int32(i - runs[i])
            lax.while_loop(lambda s: s[1] > 0, step, (first, alive(first)))
        o_ref[0, i * t:(i + 1) * t, :] = acc_ref[i].astype(o_ref.dtype)


def _sb_attention(q, k, v):
    bsz, seq, _ = q.shape
    t = SB_TILE
    pairs = SB_W // LANES
    whole = pl.BlockSpec((1, seq, LANES), lambda b, p: (b, 0, p))
    return pl.pallas_call(
        _sb_kernel,
        grid=(bsz, pairs),
        in_specs=[whole, whole, whole],
        out_specs=whole,
        out_shape=jax.ShapeDtypeStruct((bsz, seq, SB_W), BF16),
        scratch_shapes=[pltpu.VMEM((seq // t, t, LANES), F32), pltpu.VMEM((seq // t, 2 * t, 1), F32),
                        pltpu.VMEM((2, 2 * t, t), F32), pltpu.VMEM((2, 2 * t, t), BF16),
                        pltpu.VMEM((seq // t, 2 * t, LANES), BF16)],
        compiler_params=_params("parallel", "parallel"),
        name="stickbreak_attn",
    )(q, k, v)


def _gdn_kernel(qkv_ref, z_ref, ab_ref, abt_ref, prow_ref, pcol_ref, gout_ref, o_ref,
                state_ref, lhs_ref, add_ref, gam_ref, *, steps_per_seq):
    ts = DN_STEP
    chunks = ts // DN_CHUNK
    j = pl.program_id(0)

    @pl.when(j == 0)
    def _():
        lhs_ref[...] = jnp.zeros_like(lhs_ref)
        add_ref[...] = jnp.zeros_like(add_ref)
        gam_ref[...] = jnp.zeros_like(gam_ref)

    @pl.when(jnp.logical_or(j == 0, (j + steps_per_seq - 1) % steps_per_seq == 0))
    def _():
        state_ref[...] = jnp.zeros_like(state_ref)

    _gdn_scan(z_ref, gout_ref, o_ref, state_ref, lhs_ref, add_ref, gam_ref)

    tb = 2 * DN_CHUNK
    ri = lax.broadcasted_iota(jnp.int32, (tb, tb), 0)
    ci = lax.broadcasted_iota(jnp.int32, (tb, tb), 1)
    same_chunk = (ri >> DN_CHUNK_LOG2) == (ci >> DN_CHUNK_LOG2)
    m_incl = same_chunk & (ci <= ri)
    m_strict = same_chunk & (ci < ri)
    eye = jnp.where(ri == ci, 1.0, 0.0)
    tri = jnp.where(m_incl, 1.0, 0.0).astype(BF16)
    tri_t = jnp.where(same_chunk & (ri <= ci), 1.0, 0.0).astype(BF16)
    units = [(h, r) for h in range(DN_HEADS) for r in range(ts // tb)]
    rows_of = lambda r: slice(r * tb, (r + 1) * tb)

    ab = ab_ref[...]
    beta_all = _sigmoid(ab)
    g_all = -jnp.exp(prow_ref[0:1, :]) * _softplus(ab + prow_ref[1:2, :])
    abt = abt_ref[0]
    g_rows = -jnp.exp(pcol_ref[:, 0:1]) * _softplus(abt + pcol_ref[:, 1:2])
    g_all3 = _split_bf16(g_all, 3)
    g_rows3 = _split_bf16(g_rows, 3)
    gcum_all = [sum(_dot(tri, p[rows_of(r)]) for p in g_all3) for r in range(ts // tb)]
    gcum_rows = [sum(_dot(p[:, rows_of(r)], tri_t) for p in g_rows3) for r in range(ts // tb)]

    def col(h, part):
        return slice(part * DN_W + h * DN_DIM, part * DN_W + (h + 1) * DN_DIM)

    q = {u: qkv_ref[rows_of(u[1]), col(u[0], 0)] for u in units}
    k = {u: qkv_ref[rows_of(u[1]), col(u[0], 1)] for u in units}
    v = {u: qkv_ref[rows_of(u[1]), col(u[0], 2)] for u in units}
    beta = {(h, r): beta_all[rows_of(r), h:h + 1] for h, r in units}
    gc = {(h, r): gcum_all[r][:, DN_HEADS + h:DN_HEADS + h + 1] for h, r in units}
    gr = {(h, r): gcum_rows[r][DN_HEADS + h:DN_HEADS + h + 1, :] for h, r in units}
    decay = {u: jnp.where(m_incl, jnp.exp(jnp.where(m_incl, gc[u] - gr[u], 0.0)), 0.0) for u in units}
    kb = {u: k[u] * beta[u] for u in units}
    vb = {u: v[u] * beta[u] for u in units}
    eg = {u: jnp.exp(gc[u]) for u in units}
    qg = {u: q[u] * eg[u] for u in units}
    kq = {u: _dot_nt(jnp.concatenate([kb[u], q[u]], axis=0).astype(BF16), k[u].astype(BF16)) for u in units}
    lower = {u: jnp.where(m_strict, kq[u][:tb] * decay[u], 0.0) for u in units}
    a_qk = {u: (kq[u][tb:] * decay[u]).astype(BF16) for u in units}

    blk8 = (ri >> DN_BASE_LOG2) == (ci >> DN_BASE_LOG2)
    m8 = {u: jnp.where(blk8, -lower[u], 0.0).astype(BF16) for u in units}
    inv = {u: eye + m8[u].astype(F32) for u in units}
    p = {u: _dot(m8[u], m8[u]).astype(BF16) for u in units}
    inv = {u: inv[u] + _dot(inv[u].astype(BF16), p[u]) for u in units}
    p = {u: _dot(p[u], p[u]).astype(BF16) for u in units}
    inv = {u: inv[u] + _dot(inv[u].astype(BF16), p[u]) for u in units}
    for lg in range(DN_BASE_LOG2, DN_CHUNK_LOG2):
        off = ((ri >> (lg + 1)) == (ci >> (lg + 1))) & (((ri >> lg) & 1) == 1) & (((ci >> lg) & 1) == 0)
        inv16 = {u: inv[u].astype(BF16) for u in units}
        cb = {u: _dot(jnp.where(off, lower[u], 0.0).astype(BF16), inv16[u]).astype(BF16) for u in units}
        inv = {u: inv[u] - _dot(inv16[u], cb[u]) for u in units}
    sol = {u: _dot(inv[u].astype(BF16), jnp.concatenate([vb[u], kb[u] * eg[u]], axis=1).astype(BF16))
           for u in units}

    a_sol = {u: _dot(a_qk[u], sol[u].astype(BF16)) for u in units}
    for u in units:
        h, r = u
        q_eff = (qg[u] - a_sol[u][:, DN_DIM:]).astype(BF16)
        for cc in range(tb // DN_CHUNK):
            c = r * (tb // DN_CHUNK) + cc
            rows = slice(cc * DN_CHUNK, (cc + 1) * DN_CHUNK)
            g_last = gc[u][(cc + 1) * DN_CHUNK - 1:(cc + 1) * DN_CHUNK, :]
            k_dec = k[u][rows] * jnp.exp(g_last - gc[u][rows])
            kd_sol = _dot_tn(k_dec.astype(BF16), sol[u][rows].astype(BF16))
            lhs_ref[h, c, 0:DN_CHUNK, :] = q_eff[rows]
            lhs_ref[h, c, DN_CHUNK:, :] = kd_sol[:, DN_DIM:].astype(BF16)
            add_ref[h, c, 0:DN_CHUNK, :] = a_sol[u][rows, :DN_DIM]
            add_ref[h, c, DN_CHUNK:, :] = kd_sol[:, :DN_DIM]
            gam_ref[h * chunks + c:h * chunks + c + 1, :] = jnp.broadcast_to(jnp.exp(g_last), (1, LANES))


def _gdn_scan(z_ref, gout_ref, o_ref, state_ref, lhs_ref, add_ref, gam_ref):
    chunks = DN_STEP // DN_CHUNK
    out = [[] for _ in range(DN_HEADS)]
    for c in range(chunks):
        for h in range(DN_HEADS):
            state = state_ref[h]
            res = _dot(lhs_ref[h, c], state.astype(BF16))
            add = add_ref[h, c]
            out[h].append(res[:DN_CHUNK] + add[:DN_CHUNK])
            gamma = gam_ref[h * chunks + c:h * chunks + c + 1, :]
            state_ref[h] = state * gamma - res[DN_CHUNK:] + add[DN_CHUNK:]
    for h in range(DN_HEADS):
        o = jnp.concatenate(out[h], axis=0)
        o = o * lax.rsqrt(jnp.mean(o * o, axis=-1, keepdims=True) + EPS) * gout_ref[...]
        zh = z_ref[:, h * DN_DIM:(h + 1) * DN_DIM]
        o_ref[:, h * DN_DIM:(h + 1) * DN_DIM] = (o * (zh * _sigmoid(zh))).astype(o_ref.dtype)


def _gated_deltanet(qkv, z, ab, abt, prow, pcol, g_out, seq):
    tokens = qkv.shape[0]
    ts = DN_STEP
    steps = seq // ts
    total = tokens // ts
    chunks = ts // DN_CHUNK
    fill_step = lambda j: jnp.minimum(j, total - 1)
    drain_step = lambda j: jnp.maximum(j - 1, 0)
    return pl.pallas_call(
        functools.partial(_gdn_kernel, steps_per_seq=steps),
        grid=(total + 1,),
        in_specs=[pl.BlockSpec((ts, 3 * DN_W), lambda j: (fill_step(j), 0)),
                  pl.BlockSpec((ts, DN_W), lambda j: (drain_step(j), 0)),
                  pl.BlockSpec((ts, LANES), lambda j: (fill_step(j), 0)),
                  pl.BlockSpec((1, 2 * SUBLANES, ts), lambda j: (fill_step(j) // steps, 0, fill_step(j) % steps)),
                  pl.BlockSpec((2, LANES), lambda j: (0, 0)),
                  pl.BlockSpec((2 * SUBLANES, LANES), lambda j: (0, 0)),
                  pl.BlockSpec((1, DN_DIM), lambda j: (0, 0))],
        out_specs=pl.BlockSpec((ts, DN_W), lambda j: (drain_step(j), 0)),
        out_shape=jax.ShapeDtypeStruct((tokens, DN_W), BF16),
        scratch_shapes=[pltpu.VMEM((DN_HEADS, DN_DIM, DN_DIM), F32),
                        pltpu.VMEM((DN_HEADS, chunks, DN_CHUNK + DN_DIM, DN_DIM), BF16),
                        pltpu.VMEM((DN_HEADS, chunks, DN_CHUNK + DN_DIM, DN_DIM), F32),
                        pltpu.VMEM((DN_HEADS * chunks, LANES), F32)],
        compiler_params=_params("arbitrary"),
        name="gated_deltanet",
    )(qkv, z, ab, abt, prow, pcol, g_out.reshape(1, DN_DIM))


def _merge_kernel(h_ref, attn_ref, dn_ref, mod_ref, g_ref, wr_ref, wus_ref, wud_ref, wo_ref, out_ref,
                  u_ref, m_ref):
    u_ref[...] = _rms_mod(h_ref[...], g_ref[...], mod_ref[0, 3:4, :], mod_ref[0, 4:5, :]).astype(BF16)
    for n in range(D_MODEL // MXU_DIM):
        cols = slice(n * MXU_DIM, (n + 1) * MXU_DIM)
        r_sb = _dot(u_ref[...], wr_ref[:, _PROJ_R + n * MXU_DIM:_PROJ_R + (n + 1) * MXU_DIM])
        r_dn = _dot(u_ref[...], wr_ref[:, _PROJ_R + D_MODEL + n * MXU_DIM:_PROJ_R + D_MODEL + (n + 1) * MXU_DIM])
        y_sb = _dot(attn_ref[...], wus_ref[:, cols].astype(BF16))
        y_dn = _dot(dn_ref[...], wud_ref[:, cols].astype(BF16))
        m_ref[:, cols] = (_sigmoid(r_sb) * y_sb + _sigmoid(r_dn) * y_dn).astype(BF16)
    for n in range(D_MODEL // MXU_DIM):
        cols = slice(n * MXU_DIM, (n + 1) * MXU_DIM)
        out_ref[:, cols] = h_ref[:, cols] + mod_ref[0, 5:6, cols] * _dot(m_ref[...], wo_ref[:, cols].astype(BF16))


def _merge(h2, attn, dn, mod3, g_mix, w_pack, w_up_sb, w_up_dn, w_out, seq):
    tokens = h2.shape[0]
    tm = TOKEN_TILE
    tiles_per_seq = seq // tm
    row = lambda w: pl.BlockSpec((tm, w), lambda i: (i, 0))
    return pl.pallas_call(
        _merge_kernel,
        grid=(tokens // tm,),
        in_specs=[row(D_MODEL), row(SB_W), row(DN_W),
                  pl.BlockSpec((1, N_MOD, D_MODEL), lambda i: (i // tiles_per_seq, 0, 0)),
                  _resident((1, D_MODEL), lambda i: (0, 0)),
                  _resident((D_MODEL, _PROJ_W), lambda i: (0, 0)),
                  _resident((SB_W, D_MODEL), lambda i: (0, 0)),
                  _resident((DN_W, D_MODEL), lambda i: (0, 0)),
                  _resident((D_MODEL, D_MODEL), lambda i: (0, 0))],
        out_specs=row(D_MODEL),
        out_shape=jax.ShapeDtypeStruct((tokens, D_MODEL), F32),
        scratch_shapes=[pltpu.VMEM((tm, D_MODEL), BF16), pltpu.VMEM((tm, D_MODEL), BF16)],
        compiler_params=_params("parallel"),
        name="mixer_merge",
    )(h2, attn, dn, mod3, g_mix.reshape(1, D_MODEL), w_pack, w_up_sb, w_up_dn, w_out)


def _layer(h2, mod3, seq, layer, g_ffn1, w_ffn1_in, w_ffn1_out, g_mix, w_in, g_q_sb, g_k_sb, w_conv, a_log,
           dt_bias, g_dn_out, w_up_sb, w_up_dn, w_out, g_ffn2, w_ffn2_in, w_ffn2_out):
    tokens = h2.shape[0]
    bsz = tokens // seq
    h2 = _ffn(h2, mod3, g_ffn1, w_ffn1_in, w_ffn1_out, 0, seq)

    w_pack = w_in[layer].astype(BF16)
    w_gate = w_in[layer, :, _PROJ_GATE:_PROJ_R]
    w_abt = jnp.pad(w_gate.T, ((0, 2 * SUBLANES - 2 * DN_HEADS), (0, 0))).astype(BF16)
    gq_row = jnp.tile(g_q_sb, SB_HEADS).reshape(1, SB_W)
    gk_row = jnp.tile(g_k_sb, SB_HEADS).reshape(1, SB_W)
    q, k, v, qkv_dn, z_dn, ab, abt = _projection(h2, mod3, g_mix, w_pack, w_abt, gq_row, gk_row, w_conv, seq)

    attn = _sb_attention(q.reshape(bsz, seq, SB_W), k.reshape(bsz, seq, SB_W), v.reshape(bsz, seq, SB_W))

    decay_params = jnp.stack([a_log, dt_bias])
    prow = jnp.pad(decay_params, ((0, 0), (DN_HEADS, LANES - 2 * DN_HEADS)))
    pcol = jnp.pad(decay_params.T, ((DN_HEADS, 2 * SUBLANES - 2 * DN_HEADS), (0, LANES - 2)))
    dn = _gated_deltanet(qkv_dn, z_dn, ab, abt, prow, pcol, g_dn_out, seq)

    h2 = _merge(h2, attn.reshape(tokens, SB_W), dn, mod3, g_mix, w_pack,
                w_up_sb, w_up_dn, w_out, seq)
    return _ffn(h2, mod3, g_ffn2, w_ffn2_in, w_ffn2_out, 6, seq)


def kernel(x, c, w_ada, b_ada, g_ffn1, w_ffn1_in, w_ffn1_out, g_mix, w_in, g_q_sb, g_k_sb, w_conv, a_log,
           dt_bias, g_dn_out, w_up_sb, w_up_dn, w_out, g_ffn2, w_ffn2_in, w_ffn2_out):
    bsz, seq, d = x.shape
    depth = w_ada.shape[0]
    h2 = x.reshape(bsz * seq, d)
    for l in range(depth):
        mod3 = _modulation(c, w_ada[l], b_ada[l]).reshape(bsz, N_MOD, d)
        h2 = _layer(h2, mod3, seq, l, g_ffn1[l], w_ffn1_in[l], w_ffn1_out[l], g_mix[l], w_in, g_q_sb[l],
                    g_k_sb[l], w_conv[l], a_log[l], dt_bias[l], g_dn_out[l], w_up_sb[l], w_up_dn[l],
                    w_out[l], g_ffn2[l], w_ffn2_in[l], w_ffn2_out[l])
    return h2.reshape(bsz, seq, d).astype(x.dtype)
```

```python
import functools

import jax
import jax.numpy as jnp
from jax import lax
from jax.experimental import pallas as pl
from jax.experimental.pallas import tpu as pltpu

F32 = jnp.float32
BF16 = jnp.bfloat16

D_MODEL = 1024
D_FF = 2816
N_MOD = 9
EPS = 1e-6

SB_HEADS = 8
SB_HEAD_DIM = 64
SB_W = SB_HEADS * SB_HEAD_DIM
DN_HEADS = 4
DN_DIM = 128
DN_W = DN_HEADS * DN_DIM
DN_CONV = 4
DN_CHUNK = 64

LANES = 128
SUBLANES = 8
MXU_DIM = 256
VMEM_LIMIT = 56 * 1024 * 1024

ADA_TILE = 3 * D_MODEL
TOKEN_TILE = 1024
FFN_TILE = 512
FF_CHUNK = MXU_DIM
SB_TILE = MXU_DIM
MASKED = -1e30
SB_DEAD = -104.0
SB_STRAIGHT = 3
DN_STEP = 8 * DN_CHUNK
DN_CHUNK_LOG2 = DN_CHUNK.bit_length() - 1
DN_BASE_LOG2 = 3

_dot = functools.partial(jnp.dot, preferred_element_type=F32)


def _dot_nt(a, b):
    return lax.dot_general(a, b, (((1,), (1,)), ((), ())), preferred_element_type=F32)


def _dot_tn(a, b):
    return lax.dot_general(a, b, (((0,), (0,)), ((), ())), preferred_element_type=F32)


def _sigmoid(x):
    return 0.5 + 0.5 * jnp.tanh(0.5 * x)


def _softplus(x):
    return jnp.maximum(x, 0.0) + jnp.log(1.0 + jnp.exp(-jnp.abs(x)))


def _split_bf16(x, pieces):
    out = []
    for _ in range(pieces - 1):
        p = x.astype(BF16)
        out.append(p)
        x = x - p.astype(F32)
    out.append(x.astype(BF16))
    return out


def _rms_mod(x, gain, shift, scale):
    y = x * lax.rsqrt(jnp.mean(x * x, axis=-1, keepdims=True) + EPS)
    return (y * gain) * (1.0 + scale) + shift


def _resident(shape, index_map):
    return pl.BlockSpec(shape, index_map, pipeline_mode=pl.Buffered(1))


def _params(*semantics):
    return pltpu.CompilerParams(dimension_semantics=semantics, vmem_limit_bytes=VMEM_LIMIT)


def _mod_kernel(c_ref, w_ref, b_ref, o_ref):
    c = c_ref[...]
    cond = (c * _sigmoid(c)).astype(BF16)
    o_ref[...] = _dot(cond, w_ref[...].astype(BF16)) + b_ref[...]


def _modulation(c, w_ada, b_ada):
    bsz = c.shape[0]
    n = w_ada.shape[1]
    tn = ADA_TILE
    return pl.pallas_call(
        _mod_kernel,
        grid=(n // tn,),
        in_specs=[pl.BlockSpec((bsz, D_MODEL), lambda j: (0, 0)),
                  pl.BlockSpec((D_MODEL, tn), lambda j: (0, j)),
                  pl.BlockSpec((1, tn), lambda j: (0, j))],
        out_specs=pl.BlockSpec((bsz, tn), lambda j: (0, j)),
        out_shape=jax.ShapeDtypeStruct((bsz, n), F32),
        compiler_params=_params("arbitrary"),
        name="adaln_mod",
    )(c, w_ada, b_ada.reshape(1, n))


def _ffn_kernel(x_ref, mod_ref, g_ref, wa_ref, wb_ref, wo_ref, out_ref, u_ref, h_ref, *, mod_row):
    shift = mod_ref[0, mod_row:mod_row + 1, :]
    scale = mod_ref[0, mod_row + 1:mod_row + 2, :]
    u_ref[...] = _rms_mod(x_ref[...], g_ref[...], shift, scale).astype(BF16)
    for c in range(D_FF // FF_CHUNK):
        cols = slice(c * FF_CHUNK, (c + 1) * FF_CHUNK)
        a = _dot(u_ref[...], wa_ref[:, cols].astype(BF16))
        b = _dot(u_ref[...], wb_ref[:, cols].astype(BF16))
        h_ref[:, cols] = (a * _sigmoid(a) * b).astype(BF16)
    for n in range(D_MODEL // MXU_DIM):
        cols = slice(n * MXU_DIM, (n + 1) * MXU_DIM)
        gate = mod_ref[0, mod_row + 2:mod_row + 3, cols]
        out_ref[:, cols] = x_ref[:, cols] + (0.5 * gate) * _dot(h_ref[...], wo_ref[:, cols].astype(BF16))


def _ffn(x2, mod3, gain, w_in, w_out, mod_row, seq):
    tokens = x2.shape[0]
    tm = FFN_TILE
    tiles_per_seq = seq // tm
    return pl.pallas_call(
        functools.partial(_ffn_kernel, mod_row=mod_row),
        grid=(tokens // tm,),
        in_specs=[pl.BlockSpec((tm, D_MODEL), lambda i: (i, 0)),
                  pl.BlockSpec((1, N_MOD, D_MODEL), lambda i: (i // tiles_per_seq, 0, 0)),
                  _resident((1, D_MODEL), lambda i: (0, 0)),
                  _resident((D_MODEL, D_FF), lambda i: (0, 0)),
                  _resident((D_MODEL, D_FF), lambda i: (0, 1)),
                  _resident((D_FF, D_MODEL), lambda i: (0, 0))],
        out_specs=pl.BlockSpec((tm, D_MODEL), lambda i: (i, 0)),
        out_shape=jax.ShapeDtypeStruct((tokens, D_MODEL), F32),
        scratch_shapes=[pltpu.VMEM((tm, D_MODEL), BF16), pltpu.VMEM((tm, D_FF), BF16)],
        compiler_params=_params("parallel"),
        name=f"ffn_row{mod_row}",
    )(x2, mod3, gain.reshape(1, D_MODEL), w_in, w_in, w_out)


_PROJ_Q, _PROJ_K, _PROJ_V = 0, SB_W, 2 * SB_W
_PROJ_DN = 3 * SB_W
_PROJ_Z = _PROJ_DN + 3 * DN_W
_PROJ_MAIN_W = _PROJ_Z + DN_W
_PROJ_GATE = _PROJ_MAIN_W
_PROJ_R = _PROJ_GATE + 2 * DN_HEADS
_PROJ_W = _PROJ_R + 2 * D_MODEL


def _proj_kernel(h_ref, mod_ref, g_ref, wm_ref, wabt_ref, gq_ref, gk_ref, grp_ref, wc_ref,
                 q_ref, k_ref, v_ref, dn_ref, z_ref, ab_ref, abt_ref, u_ref, xbuf_ref, *, tiles_per_seq):
    tm = TOKEN_TILE
    hist = SUBLANES
    u_ref[...] = _rms_mod(h_ref[...], g_ref[...], mod_ref[0, 3:4, :], mod_ref[0, 4:5, :]).astype(BF16)

    @pl.when(pl.program_id(0) % tiles_per_seq == 0)
    def _():
        xbuf_ref[0:hist, :] = jnp.zeros((hist, 3 * DN_W), F32)

    def head_norm(t, gain_row):
        sq = (t * t).astype(BF16)
        ss = jnp.concatenate(
            [_dot(sq[:, c * MXU_DIM:(c + 1) * MXU_DIM], grp_ref[c * MXU_DIM:(c + 1) * MXU_DIM, c * MXU_DIM:(c + 1) * MXU_DIM])
             for c in range(SB_W // MXU_DIM)], axis=1)
        return t * lax.rsqrt(ss * (1.0 / SB_HEAD_DIM) + EPS) * gain_row

    def conv_group(g):
        cols = slice(g * DN_DIM, (g + 1) * DN_DIM)
        acc = xbuf_ref[hist:hist + tm, cols] * wc_ref[DN_CONV - 1:DN_CONV, cols]
        for tap in range(DN_CONV - 1):
            lo = hist - (DN_CONV - 1) + tap
            acc = acc + xbuf_ref[lo:lo + tm, cols] * wc_ref[tap:tap + 1, cols]
        half = 0.5 * acc
        y = half + half * jnp.tanh(half)
        if g < 2 * DN_HEADS:
            y = y * lax.rsqrt(jnp.sum(y * y, axis=-1, keepdims=True) + EPS)
        if g < DN_HEADS:
            y = y * (DN_DIM ** -0.5)
        dn_ref[:, cols] = y

    q = _dot(u_ref[...], wm_ref[:, _PROJ_Q:_PROJ_Q + SB_W])
    q_ref[...] = (head_norm(q, gq_ref[...]) * (SB_HEAD_DIM ** -0.5)).astype(BF16)
    k = _dot(u_ref[...], wm_ref[:, _PROJ_K:_PROJ_K + SB_W])
    k_ref[...] = head_norm(k, gk_ref[...]).astype(BF16)
    v_ref[...] = _dot(u_ref[...], wm_ref[:, _PROJ_V:_PROJ_V + SB_W]).astype(BF16)
    for c in range(3 * DN_W // MXU_DIM):
        cols = slice(c * MXU_DIM, (c + 1) * MXU_DIM)
        xbuf_ref[hist:hist + tm, cols] = _dot(
            u_ref[...], wm_ref[:, _PROJ_DN + c * MXU_DIM:_PROJ_DN + (c + 1) * MXU_DIM])
    for g in range(3 * DN_HEADS):
        conv_group(g)
    xbuf_ref[0:hist, :] = xbuf_ref[tm:tm + hist, :]
    z_ref[...] = _dot(u_ref[...], wm_ref[:, _PROJ_Z:_PROJ_Z + DN_W])
    gates = wm_ref[:, _PROJ_GATE:_PROJ_R]
    gates = jnp.concatenate([gates, jnp.zeros((D_MODEL, LANES - 2 * DN_HEADS), BF16)], axis=1)
    ab_ref[...] = _dot(u_ref[...], gates)
    abt_ref[0] = _dot_nt(wabt_ref[...], u_ref[...])


def _projection(h2, mod3, g_mix, w_pack, w_abt, gq_row, gk_row, w_conv, seq):
    tokens = h2.shape[0]
    bsz = tokens // seq
    tm = TOKEN_TILE
    tiles_per_seq = seq // tm
    grp = jnp.kron(jnp.eye(SB_HEADS, dtype=F32), jnp.ones((SB_HEAD_DIM, SB_HEAD_DIM), F32)).astype(BF16)
    row = lambda w: pl.BlockSpec((tm, w), lambda i: (i, 0))
    return pl.pallas_call(
        functools.partial(_proj_kernel, tiles_per_seq=tiles_per_seq),
        grid=(tokens // tm,),
        in_specs=[row(D_MODEL),
                  pl.BlockSpec((1, N_MOD, D_MODEL), lambda i: (i // tiles_per_seq, 0, 0)),
                  _resident((1, D_MODEL), lambda i: (0, 0)),
                  _resident((D_MODEL, _PROJ_W), lambda i: (0, 0)),
                  _resident((2 * SUBLANES, D_MODEL), lambda i: (0, 0)),
                  _resident((1, SB_W), lambda i: (0, 0)),
                  _resident((1, SB_W), lambda i: (0, 0)),
                  _resident((SB_W, SB_W), lambda i: (0, 0)),
                  _resident((DN_CONV, 3 * DN_W), lambda i: (0, 0))],
        out_specs=[row(SB_W), row(SB_W), row(SB_W), row(3 * DN_W), row(DN_W), row(LANES),
                   pl.BlockSpec((1, 2 * SUBLANES, tm), lambda i: (i // tiles_per_seq, 0, i % tiles_per_seq))],
        out_shape=[jax.ShapeDtypeStruct((tokens, SB_W), BF16),
                   jax.ShapeDtypeStruct((tokens, SB_W), BF16),
                   jax.ShapeDtypeStruct((tokens, SB_W), BF16),
                   jax.ShapeDtypeStruct((tokens, 3 * DN_W), F32),
                   jax.ShapeDtypeStruct((tokens, DN_W), F32),
                   jax.ShapeDtypeStruct((tokens, LANES), F32),
                   jax.ShapeDtypeStruct((bsz, 2 * SUBLANES, seq), F32)],
        scratch_shapes=[pltpu.VMEM((tm, D_MODEL), BF16), pltpu.VMEM((tm + SUBLANES, 3 * DN_W), F32)],
        compiler_params=_params("arbitrary"),
        name="mixer_in_proj",
    )(h2, mod3, g_mix.reshape(1, D_MODEL), w_pack, w_abt, gq_row, gk_row, grp, w_conv)


def _sb_kernel(q_ref, k_ref, v_ref, o_ref, acc_ref, carry_ref, lb_ref, lk_ref, q2_ref):
    t = SB_TILE
    head_a = lax.broadcasted_iota(jnp.int32, (t, LANES), 1) < SB_HEAD_DIM
    suffix = jnp.where(lax.broadcasted_iota(jnp.int32, (t, t), 0) > lax.broadcasted_iota(jnp.int32, (t, t), 1),
                       1.0, 0.0).astype(BF16)
    causal = (lax.broadcasted_iota(jnp.int32, (2 * t, t), 1)
              < (lax.broadcasted_iota(jnp.int32, (2 * t, t), 0) & (t - 1)))

    def keys(kb):
        return pl.ds(kb * t, t) if isinstance(kb, int) else pl.ds(pl.multiple_of(kb * t, t), t)

    def logits(i, kb, slot, diagonal=False):
        k = k_ref[0, keys(kb), :]
        z = _dot_nt(q2_ref[i], k)
        log_beta = jnp.minimum(z, 0.0) - jnp.log(1.0 + jnp.exp(-jnp.abs(z)))
        log_keep = log_beta - z
        if diagonal:
            log_beta = jnp.where(causal, log_beta, MASKED)
            log_keep = jnp.where(causal, log_keep, 0.0)
        lb_ref[slot] = log_beta
        lk_ref[slot] = log_keep.astype(BF16)

    def weigh(i, kb, slot):
        v = v_ref[0, keys(kb), :]
        log_keep = lk_ref[slot]
        sums = _dot(log_keep, suffix)
        w = jnp.exp(lb_ref[slot] + sums + carry_ref[i])
        carry_ref[i] = carry_ref[i] + (sums[:, 0:1] + log_keep[:, 0:1].astype(F32))
        zero_v = jnp.zeros_like(v)
        v2 = jnp.concatenate([jnp.where(head_a, v, zero_v), jnp.where(head_a, zero_v, v)], axis=0)
        w2 = jnp.concatenate([w[:t], w[t:]], axis=1).astype(BF16)
        acc_ref[i] += _dot(w2, v2)

    tiles = q_ref.shape[1] // t
    runs = [min(i + 1, SB_STRAIGHT) for i in range(tiles)]

    for i in range(tiles):
        q = q_ref[0, i * t:(i + 1) * t, :]
        zero_q = jnp.zeros_like(q)
        q2_ref[i] = jnp.concatenate([jnp.where(head_a, q, zero_q), jnp.where(head_a, zero_q, q)], axis=0)
    acc_ref[...] = jnp.zeros_like(acc_ref)
    carry_ref[...] = jnp.zeros_like(carry_ref)

    logits(0, 0, 0, diagonal=True)
    g = 0
    for i in range(tiles):
        for d in range(runs[i]):
            weigh(i, i - d, g % 2)
            if d + 1 < runs[i]:
                logits(i, i - d - 1, (g + 1) % 2)
            elif i + 1 < tiles:
                logits(i + 1, i + 1, (g + 1) % 2, diagonal=True)
            g += 1

    for i in range(tiles):
        if i >= runs[i]:
            def alive(kb, i=i):
                return jnp.logical_and(kb >= 0, jnp.max(carry_ref[i]) > SB_DEAD).astype(jnp.int32)

            def step(state, i=i, alive=alive):
                kb, _ = state
                logits(i, kb, 0)
                weigh(i, kb, 0)
                return kb - 1, alive(kb - 1)

            first = jnp.int32(i - runs[i])
            lax.while_loop(lambda s: s[1] > 0, step, (first, alive(first)))
        o_ref[0, i * t:(i + 1) * t, :] = acc_ref[i].astype(o_ref.dtype)


def _sb_attention(q, k, v):
    bsz, seq, _ = q.shape
    t = SB_TILE
    pairs = SB_W // LANES
    whole = pl.BlockSpec((1, seq, LANES), lambda b, p: (b, 0, p))
    return pl.pallas_call(
        _sb_kernel,
        grid=(bsz, pairs),
        in_specs=[whole, whole, whole],
        out_specs=whole,
        out_shape=jax.ShapeDtypeStruct((bsz, seq, SB_W), BF16),
        scratch_shapes=[pltpu.VMEM((seq // t, t, LANES), F32), pltpu.VMEM((seq // t, 2 * t, 1), F32),
                        pltpu.VMEM((2, 2 * t, t), F32), pltpu.VMEM((2, 2 * t, t), BF16),
                        pltpu.VMEM((seq // t, 2 * t, LANES), BF16)],
        compiler_params=_params("parallel", "parallel"),
        name="stickbreak_attn",
    )(q, k, v)


def _gdn_kernel(qkv_ref, z_ref, ab_ref, abt_ref, prow_ref, pcol_ref, gout_ref, o_ref,
                state_ref, lhs_ref, add_ref, gam_ref, *, steps_per_seq):
    ts = DN_STEP
    chunks = ts // DN_CHUNK
    j = pl.program_id(0)

    @pl.when(j == 0)
    def _():
        lhs_ref[...] = jnp.zeros_like(lhs_ref)
        add_ref[...] = jnp.zeros_like(add_ref)
        gam_ref[...] = jnp.zeros_like(gam_ref)

    @pl.when(jnp.logical_or(j == 0, (j + steps_per_seq - 1) % steps_per_seq == 0))
    def _():
        state_ref[...] = jnp.zeros_like(state_ref)

    _gdn_scan(z_ref, gout_ref, o_ref, state_ref, lhs_ref, add_ref, gam_ref)

    tb = 2 * DN_CHUNK
    ri = lax.broadcasted_iota(jnp.int32, (tb, tb), 0)
    ci = lax.broadcasted_iota(jnp.int32, (tb, tb), 1)
    same_chunk = (ri >> DN_CHUNK_LOG2) == (ci >> DN_CHUNK_LOG2)
    m_incl = same_chunk & (ci <= ri)
    m_strict = same_chunk & (ci < ri)
    eye = jnp.where(ri == ci, 1.0, 0.0)
    tri = jnp.where(m_incl, 1.0, 0.0).astype(BF16)
    tri_t = jnp.where(same_chunk & (ri <= ci), 1.0, 0.0).astype(BF16)
    units = [(h, r) for h in range(DN_HEADS) for r in range(ts // tb)]
    rows_of = lambda r: slice(r * tb, (r + 1) * tb)

    ab = ab_ref[...]
    beta_all = _sigmoid(ab)
    g_all = -jnp.exp(prow_ref[0:1, :]) * _softplus(ab + prow_ref[1:2, :])
    abt = abt_ref[0]
    g_rows = -jnp.exp(pcol_ref[:, 0:1]) * _softplus(abt + pcol_ref[:, 1:2])
    g_all3 = _split_bf16(g_all, 3)
    g_rows3 = _split_bf16(g_rows, 3)
    gcum_all = [sum(_dot(tri, p[rows_of(r)]) for p in g_all3) for r in range(ts // tb)]
    gcum_rows = [sum(_dot(p[:, rows_of(r)], tri_t) for p in g_rows3) for r in range(ts // tb)]

    def col(h, part):
        return slice(part * DN_W + h * DN_DIM, part * DN_W + (h + 1) * DN_DIM)

    q = {u: qkv_ref[rows_of(u[1]), col(u[0], 0)] for u in units}
    k = {u: qkv_ref[rows_of(u[1]), col(u[0], 1)] for u in units}
    v = {u: qkv_ref[rows_of(u[1]), col(u[0], 2)] for u in units}
    beta = {(h, r): beta_all[rows_of(r), h:h + 1] for h, r in units}
    gc = {(h, r): gcum_all[r][:, DN_HEADS + h:DN_HEADS + h + 1] for h, r in units}
    gr = {(h, r): gcum_rows[r][DN_HEADS + h:DN_HEADS + h + 1, :] for h, r in units}
    decay = {u: jnp.where(m_incl, jnp.exp(jnp.where(m_incl, gc[u] - gr[u], 0.0)), 0.0) for u in units}
    kb = {u: k[u] * beta[u] for u in units}
    vb = {u: v[u] * beta[u] for u in units}
    eg = {u: jnp.exp(gc[u]) for u in units}
    qg = {u: q[u] * eg[u] for u in units}
    kq = {u: _dot_nt(jnp.concatenate([kb[u], q[u]], axis=0).astype(BF16), k[u].astype(BF16)) for u in units}
    lower = {u: jnp.where(m_strict, kq[u][:tb] * decay[u], 0.0) for u in units}
    a_qk = {u: (kq[u][tb:] * decay[u]).astype(BF16) for u in units}

    blk8 = (ri >> DN_BASE_LOG2) == (ci >> DN_BASE_LOG2)
    m8 = {u: jnp.where(blk8, -lower[u], 0.0).astype(BF16) for u in units}
    inv = {u: eye + m8[u].astype(F32) for u in units}
    p = {u: _dot(m8[u], m8[u]).astype(BF16) for u in units}
    inv = {u: inv[u] + _dot(inv[u].astype(BF16), p[u]) for u in units}
    p = {u: _dot(p[u], p[u]).astype(BF16) for u in units}
    inv = {u: inv[u] + _dot(inv[u].astype(BF16), p[u]) for u in units}
    for lg in range(DN_BASE_LOG2, DN_CHUNK_LOG2):
        off = ((ri >> (lg + 1)) == (ci >> (lg + 1))) & (((ri >> lg) & 1) == 1) & (((ci >> lg) & 1) == 0)
        inv16 = {u: inv[u].astype(BF16) for u in units}
        cb = {u: _dot(jnp.where(off, lower[u], 0.0).astype(BF16), inv16[u]).astype(BF16) for u in units}
        inv = {u: inv[u] - _dot(inv16[u], cb[u]) for u in units}
    sol = {u: _dot(inv[u].astype(BF16), jnp.concatenate([vb[u], kb[u] * eg[u]], axis=1).astype(BF16))
           for u in units}

    a_sol = {u: _dot(a_qk[u], sol[u].astype(BF16)) for u in units}
    for u in units:
        h, r = u
        q_eff = (qg[u] - a_sol[u][:, DN_DIM:]).astype(BF16)
        for cc in range(tb // DN_CHUNK):
            c = r * (tb // DN_CHUNK) + cc
            rows = slice(cc * DN_CHUNK, (cc + 1) * DN_CHUNK)
            g_last = gc[u][(cc + 1) * DN_CHUNK - 1:(cc + 1) * DN_CHUNK, :]
            k_dec = k[u][rows] * jnp.exp(g_last - gc[u][rows])
            kd_sol = _dot_tn(k_dec.astype(BF16), sol[u][rows].astype(BF16))
            lhs_ref[h, c, 0:DN_CHUNK, :] = q_eff[rows]
            lhs_ref[h, c, DN_CHUNK:, :] = kd_sol[:, DN_DIM:].astype(BF16)
            add_ref[h, c, 0:DN_CHUNK, :] = a_sol[u][rows, :DN_DIM]
            add_ref[h, c, DN_CHUNK:, :] = kd_sol[:, :DN_DIM]
            gam_ref[h * chunks + c:h * chunks + c + 1, :] = jnp.broadcast_to(jnp.exp(g_last), (1, LANES))


def _gdn_scan(z_ref, gout_ref, o_ref, state_ref, lhs_ref, add_ref, gam_ref):
    chunks = DN_STEP // DN_CHUNK
    out = [[] for _ in range(DN_HEADS)]
    for c in range(chunks):
        for h in range(DN_HEADS):
            state = state_ref[h]
            res = _dot(lhs_ref[h, c], state.astype(BF16))
            add = add_ref[h, c]
            out[h].append(res[:DN_CHUNK] + add[:DN_CHUNK])
            gamma = gam_ref[h * chunks + c:h * chunks + c + 1, :]
            state_ref[h] = state * gamma - res[DN_CHUNK:] + add[DN_CHUNK:]
    for h in range(DN_HEADS):
        o = jnp.concatenate(out[h], axis=0)
        o = o * lax.rsqrt(jnp.mean(o * o, axis=-1, keepdims=True) + EPS) * gout_ref[...]
        zh = z_ref[:, h * DN_DIM:(h + 1) * DN_DIM]
        o_ref[:, h * DN_DIM:(h + 1) * DN_DIM] = (o * (zh * _sigmoid(zh))).astype(o_ref.dtype)


def _gated_deltanet(qkv, z, ab, abt, prow, pcol, g_out, seq):
    tokens = qkv.shape[0]
    ts = DN_STEP
    steps = seq // ts
    total = tokens // ts
    chunks = ts // DN_CHUNK
    fill_step = lambda j: jnp.minimum(j, total - 1)
    drain_step = lambda j: jnp.maximum(j - 1, 0)
    return pl.pallas_call(
        functools.partial(_gdn_kernel, steps_per_seq=steps),
        grid=(total + 1,),
        in_specs=[pl.BlockSpec((ts, 3 * DN_W), lambda j: (fill_step(j), 0)),
                  pl.BlockSpec((ts, DN_W), lambda j: (drain_step(j), 0)),
                  pl.BlockSpec((ts, LANES), lambda j: (fill_step(j), 0)),
                  pl.BlockSpec((1, 2 * SUBLANES, ts), lambda j: (fill_step(j) // steps, 0, fill_step(j) % steps)),
                  pl.BlockSpec((2, LANES), lambda j: (0, 0)),
                  pl.BlockSpec((2 * SUBLANES, LANES), lambda j: (0, 0)),
                  pl.BlockSpec((1, DN_DIM), lambda j: (0, 0))],
        out_specs=pl.BlockSpec((ts, DN_W), lambda j: (drain_step(j), 0)),
        out_shape=jax.ShapeDtypeStruct((tokens, DN_W), BF16),
        scratch_shapes=[pltpu.VMEM((DN_HEADS, DN_DIM, DN_DIM), F32),
                        pltpu.VMEM((DN_HEADS, chunks, DN_CHUNK + DN_DIM, DN_DIM), BF16),
                        pltpu.VMEM((DN_HEADS, chunks, DN_CHUNK + DN_DIM, DN_DIM), F32),
                        pltpu.VMEM((DN_HEADS * chunks, LANES), F32)],
        compiler_params=_params("arbitrary"),
        name="gated_deltanet",
    )(qkv, z, ab, abt, prow, pcol, g_out.reshape(1, DN_DIM))


def _merge_kernel(h_ref, attn_ref, dn_ref, mod_ref, g_ref, wr_ref, wus_ref, wud_ref, wo_ref, out_ref,
                  u_ref, m_ref):
    u_ref[...] = _rms_mod(h_ref[...], g_ref[...], mod_ref[0, 3:4, :], mod_ref[0, 4:5, :]).astype(BF16)
    for n in range(D_MODEL // MXU_DIM):
        cols = slice(n * MXU_DIM, (n + 1) * MXU_DIM)
        r_sb = _dot(u_ref[...], wr_ref[:, _PROJ_R + n * MXU_DIM:_PROJ_R + (n + 1) * MXU_DIM])
        r_dn = _dot(u_ref[...], wr_ref[:, _PROJ_R + D_MODEL + n * MXU_DIM:_PROJ_R + D_MODEL + (n + 1) * MXU_DIM])
        y_sb = _dot(attn_ref[...], wus_ref[:, cols].astype(BF16))
        y_dn = _dot(dn_ref[...], wud_ref[:, cols].astype(BF16))
        m_ref[:, cols] = (_sigmoid(r_sb) * y_sb + _sigmoid(r_dn) * y_dn).astype(BF16)
    for n in range(D_MODEL // MXU_DIM):
        cols = slice(n * MXU_DIM, (n + 1) * MXU_DIM)
        out_ref[:, cols] = h_ref[:, cols] + mod_ref[0, 5:6, cols] * _dot(m_ref[...], wo_ref[:, cols].astype(BF16))


def _merge(h2, attn, dn, mod3, g_mix, w_pack, w_up_sb, w_up_dn, w_out, seq):
    tokens = h2.shape[0]
    tm = TOKEN_TILE
    tiles_per_seq = seq // tm
    row = lambda w: pl.BlockSpec((tm, w), lambda i: (i, 0))
    return pl.pallas_call(
        _merge_kernel,
        grid=(tokens // tm,),
        in_specs=[row(D_MODEL), row(SB_W), row(DN_W),
                  pl.BlockSpec((1, N_MOD, D_MODEL), lambda i: (i // tiles_per_seq, 0, 0)),
                  _resident((1, D_MODEL), lambda i: (0, 0)),
                  _resident((D_MODEL, _PROJ_W), lambda i: (0, 0)),
                  _resident((SB_W, D_MODEL), lambda i: (0, 0)),
                  _resident((DN_W, D_MODEL), lambda i: (0, 0)),
                  _resident((D_MODEL, D_MODEL), lambda i: (0, 0))],
        out_specs=row(D_MODEL),
        out_shape=jax.ShapeDtypeStruct((tokens, D_MODEL), F32),
        scratch_shapes=[pltpu.VMEM((tm, D_MODEL), BF16), pltpu.VMEM((tm, D_MODEL), BF16)],
        compiler_params=_params("parallel"),
        name="mixer_merge",
    )(h2, attn, dn, mod3, g_mix.reshape(1, D_MODEL), w_pack, w_up_sb, w_up_dn, w_out)


def _layer(h2, mod3, seq, layer, g_ffn1, w_ffn1_in, w_ffn1_out, g_mix, w_in, g_q_sb, g_k_sb, w_conv, a_log,
           dt_bias, g_dn_out, w_up_sb, w_up_dn, w_out, g_ffn2, w_ffn2_in, w_ffn2_out):
    tokens = h2.shape[0]
    bsz = tokens // seq
    h2 = _ffn(h2, mod3, g_ffn1, w_ffn1_in, w_ffn1_out, 0, seq)

    w_pack = w_in[layer].astype(BF16)
    w_gate = w_in[layer, :, _PROJ_GATE:_PROJ_R]
    w_abt = jnp.pad(w_gate.T, ((0, 2 * SUBLANES - 2 * DN_HEADS), (0, 0))).astype(BF16)
    gq_row = jnp.tile(g_q_sb, SB_HEADS).reshape(1, SB_W)
    gk_row = jnp.tile(g_k_sb, SB_HEADS).reshape(1, SB_W)
    q, k, v, qkv_dn, z_dn, ab, abt = _projection(h2, mod3, g_mix, w_pack, w_abt, gq_row, gk_row, w_conv, seq)

    attn = _sb_attention(q.reshape(bsz, seq, SB_W), k.reshape(bsz, seq, SB_W), v.reshape(bsz, seq, SB_W))

    decay_params = jnp.stack([a_log, dt_bias])
    prow = jnp.pad(decay_params, ((0, 0), (DN_HEADS, LANES - 2 * DN_HEADS)))
    pcol = jnp.pad(decay_params.T, ((DN_HEADS, 2 * SUBLANES - 2 * DN_HEADS), (0, LANES - 2)))
    dn = _gated_deltanet(qkv_dn, z_dn, ab, abt, prow, pcol, g_dn_out, seq)

    h2 = _merge(h2, attn.reshape(tokens, SB_W), dn, mod3, g_mix, w_pack,
                w_up_sb, w_up_dn, w_out, seq)
    return _ffn(h2, mod3, g_ffn2, w_ffn2_in, w_ffn2_out, 6, seq)


def kernel(x, c, w_ada, b_ada, g_ffn1, w_ffn1_in, w_ffn1_out, g_mix, w_in, g_q_sb, g_k_sb, w_conv, a_log,
           dt_bias, g_dn_out, w_up_sb, w_up_dn, w_out, g_ffn2, w_ffn2_in, w_ffn2_out):
    bsz, seq, d = x.shape
    depth = w_ada.shape[0]
    h2 = x.reshape(bsz * seq, d)
    for l in range(depth):
        mod3 = _modulation(c, w_ada[l], b_ada[l]).reshape(bsz, N_MOD, d)
        h2 = _layer(h2, mod3, seq, l, g_ffn1[l], w_ffn1_in[l], w_ffn1_out[l], g_mix[l], w_in, g_q_sb[l],
                    g_k_sb[l], w_conv[l], a_log[l], dt_bias[l], g_dn_out[l], w_up_sb[l], w_up_dn[l],
                    w_out[l], g_ffn2[l], w_ffn2_in[l], w_ffn2_out[l])
    return h2.reshape(bsz, seq, d).astype(x.dtype)
```
